```python
import math
import jax, jax.numpy as jnp
from jax import lax
import numpy as np

D_MODEL = 1024
BATCH = 4
SEQ = 4096
DEPTH = 2
DEC_BATCH = 32
DEC_SEQ = 4
PAST_LEN = 8192
PAGE_SIZE = 128

N_EVEN = (DEPTH + 1) // 2
N_ODD = DEPTH // 2
EPS = 1e-6

A_HEAD_DIM = 64
A_HEADS_PER_GROUP = 4
A_GROUPS = ((128, 1), (512, 4), (2048, 16))
A_N_GROUPS = len(A_GROUPS)
A_HEADS = A_N_GROUPS * A_HEADS_PER_GROUP
A_OUT = A_HEADS_PER_GROUP * A_HEAD_DIM
A_QKV = 3 * A_HEADS * A_HEAD_DIM
A_BLOCK = 128
A_SCALE = A_HEAD_DIM ** -0.5
N_BUCKETS = 32
BUCKET_MAX_DIST = 2048

B_HEAD_DIM = 64
B_HEADS = 12
B_WIDTH = B_HEADS * B_HEAD_DIM
DECAY_LORA = 64
AAA_LORA = 64
GATE_LORA = 160
B_COLS = 3 * B_WIDTH + DECAY_LORA + AAA_LORA + GATE_LORA
E_COLS = A_QKV + B_COLS
MIX_WIDTH = A_OUT + B_WIDTH
LNX_EPS = 64e-5

C_EXPAND = 128
C_HEADS = D_MODEL // C_EXPAND
C_FDIM = C_HEADS * C_EXPAND
C_VDIM_HEAD = D_MODEL // C_HEADS
C_COLS = 2 * C_FDIM + 2 * D_MODEL
C_CHUNK = 64

D_FF = 4 * D_MODEL

kernel_name = 'hybrid_dilated_rwkv7_hgrn2_step'


def rmsnorm(x, g):
    xf = x.astype(jnp.float32)
    y = xf * lax.rsqrt(jnp.mean(xf * xf, -1, keepdims=True) + EPS)
    return (y * g.astype(jnp.float32)).astype(x.dtype)


def t5_bucket(dist):
    max_exact = N_BUCKETS // 2
    d = jnp.maximum(dist, 1).astype(jnp.float32)
    large = max_exact + (jnp.log(d / max_exact) / math.log(BUCKET_MAX_DIST / max_exact)
                         * (N_BUCKETS - max_exact)).astype(jnp.int32)
    large = jnp.minimum(large, N_BUCKETS - 1)
    return jnp.where(dist < max_exact, dist, large)


def dilated_block(q, kv, pos0, bias, window, dil):
    T = q.shape[1]
    nk = window // dil + 1
    t = jnp.arange(T)
    j = jnp.arange(nk)
    rows = window + t[:, None] - j[None, :] * dil
    valid = (pos0 + t[:, None] - j[None, :] * dil) >= 0
    kvg = kv[:, rows].astype(jnp.float32)
    logits = jnp.einsum('bthd,btjhd->bhtj', q.astype(jnp.float32), kvg[:, :, :, 0]) * A_SCALE
    logits = logits + bias.T[None, :, None, :]
    logits = jnp.where(valid[None, None], logits, -jnp.inf)
    lse = jax.nn.logsumexp(logits, axis=-1)
    p = jnp.exp(logits - lse[..., None])
    o = jnp.einsum('bhtj,btjhd->bthd', p, kvg[:, :, :, 1])
    return o, lse


def attend_group(q, kv_full, pos0, bias, window, dil):
    B_, T, H, Dh = q.shape
    if T <= A_BLOCK:
        return dilated_block(q, kv_full, pos0, bias, window, dil)
    nb = T // A_BLOCK

    def one(n):
        s = n * A_BLOCK
        qb = lax.dynamic_slice_in_dim(q, s, A_BLOCK, axis=1)
        kvb = lax.dynamic_slice_in_dim(kv_full, s, window + A_BLOCK, axis=1)
        return dilated_block(qb, kvb, pos0 + s, bias, window, dil)

    o, lse = lax.map(one, jnp.arange(nb))
    o = jnp.moveaxis(o, 0, 1).reshape(B_, T, H, Dh)
    lse = jnp.moveaxis(lse, 0, 2).reshape(B_, H, T)
    return o, lse


def mixer_a(qkv, pos0, rel_bias, fronts):
    B_, T = qkv.shape[:2]
    outs, lses, rows = [], [], []
    for g, (window, dil) in enumerate(A_GROUPS):
        hs = slice(g * A_HEADS_PER_GROUP, (g + 1) * A_HEADS_PER_GROUP)
        q = qkv[:, :, 0, hs]
        kv_new = qkv[:, :, 1:, hs]
        kv_full = jnp.concatenate([fronts[g].astype(kv_new.dtype), kv_new], axis=1)
        nk = window // dil + 1
        bias = rel_bias[t5_bucket(jnp.arange(nk) * dil)][:, hs].astype(jnp.float32)
        o, lse = attend_group(q, kv_full, pos0, bias, window, dil)
        outs.append(o)
        lses.append(lse)
        rows.append(kv_new[:, -min(window, T):])
    wts = jax.nn.softmax(jnp.stack(lses), axis=0)
    out = jnp.einsum('gbht,gbthd->bthd', wts, jnp.stack(outs))
    return out.reshape(B_, T, A_OUT), rows


def mixer_b(pb, shift_prev, s0, mu, w0, w2, a0, a2, g2, k_k, k_a, r_k, lnx_w, lnx_b):
    f32 = jnp.float32
    B_, T, _ = pb.shape
    pb = pb.astype(f32)
    prev = jnp.concatenate([shift_prev.astype(f32)[:, None], pb[:, :-1]], axis=1)
    xs = pb + (prev - pb) * mu.astype(f32)
    W = B_WIDTH
    r = xs[..., :W]
    k = xs[..., W:2 * W]
    v = xs[..., 2 * W:3 * W]
    wd = xs[..., 3 * W:3 * W + DECAY_LORA]
    ad = xs[..., 3 * W + DECAY_LORA:3 * W + DECAY_LORA + AAA_LORA]
    gd = xs[..., 3 * W + DECAY_LORA + AAA_LORA:]
    w = -jax.nn.softplus(-(w0.astype(f32) + jnp.tanh(wd) @ w2.astype(f32))) - 0.5
    decay = jnp.exp(-jnp.exp(w))
    a = jax.nn.sigmoid(a0.astype(f32) + ad @ a2.astype(f32))
    g = jax.nn.sigmoid(gd) @ g2.astype(f32)
    heads = lambda t: t.reshape(B_, T, B_HEADS, B_HEAD_DIM)
    kk = heads(k * k_k.astype(f32))
    kk = kk / jnp.maximum(jnp.sqrt(jnp.sum(kk * kk, -1, keepdims=True)), 1e-12)
    k = k * (1.0 + (a - 1.0) * k_a.astype(f32))
    r_h, k_h, v_h, d_h, a_h = heads(r), heads(k), heads(v), heads(decay), heads(a)
    tm = lambda t: jnp.moveaxis(t, 1, 0)

    def step(S, inp):
        r_t, d_t, k_t, v_t, kk_t, a_t = inp
        sa = jnp.einsum('bhvk,bhk->bhv', S, -kk_t)
        S = (S * d_t[:, :, None, :] + sa[..., None] * (kk_t * a_t)[:, :, None, :]
             + v_t[..., None] * k_t[:, :, None, :])
        return S, jnp.einsum('bhvk,bhk->bhv', S, r_t)

    S, y = lax.scan(step, s0.astype(f32), (tm(r_h), tm(d_h), tm(k_h), tm(v_h), tm(kk), tm(a_h)))
    y = tm(y)
    mean = jnp.mean(y, -1, keepdims=True)
    var = jnp.mean(jnp.square(y - mean), -1, keepdims=True)
    y = ((y - mean) * lax.rsqrt(var + LNX_EPS)).reshape(B_, T, W) * lnx_w.astype(f32) + lnx_b.astype(f32)
    bonus = jnp.sum(r_h * k_h * r_k.astype(f32), -1, keepdims=True) * v_h
    out = (y + bonus.reshape(B_, T, W)) * g
    return out, pb[:, -1], S


def hgrn_lower_bounds(lb_raw):
    sm = jax.nn.softmax(lb_raw.astype(jnp.float32), axis=0)
    return jnp.cumsum(sm, axis=0) - sm[0]


def gla_chunks(q, k, v, logf, s0, chunk):
    B_, T, H, Dk = q.shape
    Dv = v.shape[-1]
    n = T // chunk
    to_chunks = lambda t: jnp.moveaxis(t.reshape(B_, n, chunk, H, t.shape[-1]), 1, 0)
    mask = jnp.tril(jnp.ones((chunk, chunk), bool))[None, :, :, None, None]

    def step(S, inp):
        qc, kc, vc, lc = inp
        b = jnp.cumsum(lc, axis=1)
        diff = b[:, :, None] - b[:, None, :]
        dec = jnp.exp(jnp.where(mask, diff, -jnp.inf))
        att = jnp.einsum('bthk,bshk,btshk->bhts', qc, kc, dec)
        o = (jnp.einsum('bthk,bhkv->bthv', qc * jnp.exp(b), S)
             + jnp.einsum('bhts,bshv->bthv', att, vc))
        bl = b[:, -1]
        S = S * jnp.exp(bl)[..., None] + jnp.einsum('bshk,bshv->bhkv', kc * jnp.exp(bl[:, None] - b), vc)
        return S, o

    S, o = lax.scan(step, s0, (to_chunks(q), to_chunks(k), to_chunks(v), to_chunks(logf)))
    return jnp.moveaxis(o, 0, 1).reshape(B_, T, H, Dv), S


def mixer_c(pc, s0, lb, norm_g):
    f32 = jnp.float32
    B_, T, _ = pc.shape
    pc = pc.astype(f32)
    q = jax.nn.silu(pc[..., :C_FDIM])
    fg = lb + (1.0 - lb) * jax.nn.sigmoid(pc[..., C_FDIM:2 * C_FDIM])
    k = 1.0 - fg
    logf = jnp.log(fg)
    i = pc[..., 2 * C_FDIM:2 * C_FDIM + D_MODEL]
    gate = pc[..., 2 * C_FDIM + D_MODEL:]
    hk = lambda t: t.reshape(B_, T, C_HEADS, C_EXPAND)
    chunk = C_CHUNK if T % C_CHUNK == 0 else T
    o, S = gla_chunks(hk(q), hk(k), i.reshape(B_, T, C_HEADS, C_VDIM_HEAD), hk(logf), s0.astype(f32), chunk)
    o = rmsnorm(o.reshape(B_, T, D_MODEL), norm_g) * jax.nn.silu(gate)
    return o, S


def trunk(x, pos0, a_fronts, b_shift0, b_state0, c_state0, p):
    B_, T, _ = x.shape
    dt = x.dtype
    lb_all = hgrn_lower_bounds(p['c_lb_raw'])
    kv_rows = [[] for _ in A_GROUPS]
    shifts, wkvs, cstates = [], [], []
    for layer in range(DEPTH):
        h = rmsnorm(x, p['norm_mix'][layer])
        if layer % 2 == 0:
            e = layer // 2
            proj = h @ p['e_w_in'][e]
            qkv = proj[..., :A_QKV].reshape(B_, T, 3, A_HEADS, A_HEAD_DIM)
            a_out, rows = mixer_a(qkv, pos0, p['rel_bias'], [f[e] for f in a_fronts])
            b_out, shift, wkv = mixer_b(proj[..., A_QKV:], b_shift0[e], b_state0[e], p['e_mu'][e],
                                        p['e_w0'][e], p['e_w2'][e], p['e_a0'][e], p['e_a2'][e],
                                        p['e_g2'][e], p['e_k_k'][e], p['e_k_a'][e], p['e_r_k'][e],
                                        p['e_lnx_w'][e], p['e_lnx_b'][e])
            for g in range(A_N_GROUPS):
                kv_rows[g].append(rows[g])
            shifts.append(shift)
            wkvs.append(wkv)
            mix = jnp.concatenate([a_out, b_out], -1).astype(dt) @ p['e_w_out'][e]
        else:
            o = layer // 2
            c_out, sc = mixer_c(h @ p['c_w_in'][o], c_state0[o], lb_all[layer], p['c_norm'][o])
            cstates.append(sc)
            mix = c_out.astype(dt) @ p['c_w_out'][o]
        x = x + mix.astype(dt)
        h = rmsnorm(x, p['norm_mlp'][layer])
        x = x + (jnp.square(jax.nn.relu(h @ p['mlp_up'][layer])) @ p['mlp_down'][layer]).astype(dt)
    y = rmsnorm(x, p['norm_final'])
    return y, [jnp.stack(r) for r in kv_rows], jnp.stack(shifts), jnp.stack(wkvs), jnp.stack(cstates)


def setup_inputs(seed: int = 0) -> dict:
    key = jax.random.key(seed)
    ks = iter(jax.random.split(key, 40))
    nrm = lambda shape, s: jax.random.normal(next(ks), shape, jnp.float32) * s
    L = [min(w, PAST_LEN) for w, _ in A_GROUPS]
    d = {}
    d['x_prompt'] = nrm((BATCH, SEQ, D_MODEL), 1.0)
    d['x_sample'] = nrm((DEC_BATCH, DEC_SEQ, D_MODEL), 1.0)
    d['cache_a0'] = nrm((N_EVEN, DEC_BATCH, L[0], 2, A_HEADS_PER_GROUP, A_HEAD_DIM), 1.0)
    d['cache_a1'] = nrm((N_EVEN, DEC_BATCH, L[1], 2, A_HEADS_PER_GROUP, A_HEAD_DIM), 1.0)
    d['cache_a2'] = nrm((N_EVEN, DEC_BATCH, L[2], 2, A_HEADS_PER_GROUP, A_HEAD_DIM), 1.0)
    d['state_b_shift'] = nrm((N_EVEN, DEC_BATCH, B_COLS), 1.0)
    d['state_b_wkv'] = nrm((N_EVEN, DEC_BATCH, B_HEADS, B_HEAD_DIM, B_HEAD_DIM), 0.5)
    d['state_c'] = nrm((N_ODD, DEC_BATCH, C_HEADS, C_EXPAND, C_VDIM_HEAD), 0.5)
    d['rel_bias'] = nrm((N_BUCKETS, A_HEADS), 0.5)
    d['norm_mix'] = 1.0 + nrm((DEPTH, D_MODEL), 0.05)
    d['norm_mlp'] = 1.0 + nrm((DEPTH, D_MODEL), 0.05)
    d['norm_final'] = 1.0 + nrm((D_MODEL,), 0.05)
    d['e_w_in'] = nrm((N_EVEN, D_MODEL, E_COLS), D_MODEL ** -0.5)
    d['e_mu'] = jax.random.uniform(next(ks), (N_EVEN, B_COLS), jnp.float32)
    d['e_w0'] = nrm((N_EVEN, B_WIDTH), 0.5)
    d['e_w2'] = nrm((N_EVEN, DECAY_LORA, B_WIDTH), DECAY_LORA ** -0.5)
    d['e_a0'] = nrm((N_EVEN, B_WIDTH), 0.5)
    d['e_a2'] = nrm((N_EVEN, AAA_LORA, B_WIDTH), AAA_LORA ** -0.5)
    d['e_g2'] = nrm((N_EVEN, GATE_LORA, B_WIDTH), GATE_LORA ** -0.5)
    d['e_k_k'] = 0.85 + nrm((N_EVEN, B_WIDTH), 0.05)
    d['e_k_a'] = 1.0 + nrm((N_EVEN, B_WIDTH), 0.05)
    d['e_r_k'] = nrm((N_EVEN, B_HEADS, B_HEAD_DIM), 0.1)
    d['e_lnx_w'] = 1.0 + nrm((N_EVEN, B_WIDTH), 0.05)
    d['e_lnx_b'] = nrm((N_EVEN, B_WIDTH), 0.02)
    d['e_w_out'] = nrm((N_EVEN, MIX_WIDTH, D_MODEL), MIX_WIDTH ** -0.5)
    d['c_w_in'] = nrm((N_ODD, D_MODEL, C_COLS), D_MODEL ** -0.5)
    d['c_lb_raw'] = nrm((DEPTH, C_FDIM), 0.5)
    d['c_norm'] = 1.0 + nrm((N_ODD, D_MODEL), 0.05)
    d['c_w_out'] = nrm((N_ODD, D_MODEL, D_MODEL), D_MODEL ** -0.5)
    d['mlp_up'] = nrm((DEPTH, D_MODEL, D_FF), D_MODEL ** -0.5)
    d['mlp_down'] = nrm((DEPTH, D_FF, D_MODEL), D_FF ** -0.5)
    return d


def reference(x_prompt, x_sample, cache_a0, cache_a1, cache_a2, state_b_shift, state_b_wkv, state_c,
              rel_bias, norm_mix, norm_mlp, norm_final, e_w_in, e_mu, e_w0, e_w2, e_a0, e_a2, e_g2,
              e_k_k, e_k_a, e_r_k, e_lnx_w, e_lnx_b, e_w_out, c_w_in, c_lb_raw, c_norm, c_w_out,
              mlp_up, mlp_down):
    p = dict(rel_bias=rel_bias, norm_mix=norm_mix, norm_mlp=norm_mlp, norm_final=norm_final,
             e_w_in=e_w_in, e_mu=e_mu, e_w0=e_w0, e_w2=e_w2, e_a0=e_a0, e_a2=e_a2, e_g2=e_g2,
             e_k_k=e_k_k, e_k_a=e_k_a, e_r_k=e_r_k, e_lnx_w=e_lnx_w, e_lnx_b=e_lnx_b, e_w_out=e_w_out,
             c_w_in=c_w_in, c_lb_raw=c_lb_raw, c_norm=c_norm, c_w_out=c_w_out,
             mlp_up=mlp_up, mlp_down=mlp_down)
    dt = x_prompt.dtype
    bp = x_prompt.shape[0]
    p_fronts = [jnp.zeros((N_EVEN, bp, w, 2, A_HEADS_PER_GROUP, A_HEAD_DIM), dt) for w, _ in A_GROUPS]
    y_prompt, p_kv, p_shift, p_wkv, p_c = trunk(
        x_prompt, 0, p_fronts,
        jnp.zeros((N_EVEN, bp, B_COLS), jnp.float32),
        jnp.zeros((N_EVEN, bp, B_HEADS, B_HEAD_DIM, B_HEAD_DIM), jnp.float32),
        jnp.zeros((N_ODD, bp, C_HEADS, C_EXPAND, C_VDIM_HEAD), jnp.float32), p)
    s_fronts = [jnp.pad(c, ((0, 0), (0, 0), (w - c.shape[2], 0), (0, 0), (0, 0), (0, 0)))
                for c, (w, _) in zip((cache_a0, cache_a1, cache_a2), A_GROUPS)]
    y_sample, s_kv, s_shift, s_wkv, s_c = trunk(
        x_sample, PAST_LEN, s_fronts, state_b_shift, state_b_wkv, state_c, p)
    p_a0, p_a1, p_a2 = p_kv
    s_a0, s_a1, s_a2 = s_kv
    return (y_prompt, y_sample, p_a0, p_a1, p_a2, p_shift, p_wkv, p_c,
            s_a0, s_a1, s_a2, s_shift, s_wkv, s_c)
```

```python
import functools
import math

import jax
import jax.numpy as jnp
import numpy as np
from jax import lax
from jax.experimental import pallas as pl
from jax.experimental.pallas import tpu as pltpu

F32 = jnp.float32
BF16 = jnp.bfloat16

D_MODEL = 1024
D_FF = 4 * D_MODEL
EPS = 1e-6
LNX_EPS = 64e-5
HEAD = 64
A_GROUPS = ((128, 1), (512, 4), (2048, 16))
A_KEYS = 128
A_QKV = 2304
A_OUT = 256
N_BUCKETS = 32
BUCKET_MAX_DIST = 2048
B_WIDTH = 768
B_PAIRS = B_WIDTH // 128
LORA_PAD = 512
C_FDIM = 1024
C_HEADS = 8
NEG = -1e30
DECAY_SCALE = math.exp(-0.5)
GLA_SUB = 16
EXP_CLAMP = 80.0

VMEM_LIMIT = 56 * 1024 * 1024

NN = (((1,), (0,)), ((), ()))
NT = (((1,), (1,)), ((), ()))


def _dg(a, b, dims):
    return lax.dot_general(a, b, dims, preferred_element_type=F32)


def _split2(x):
    hi = x.astype(BF16)
    lo = (x - hi.astype(F32)).astype(BF16)
    return hi, lo


def _split3(x):
    hi = x.astype(BF16)
    r1 = x - hi.astype(F32)
    mid = r1.astype(BF16)
    lo = (r1 - mid.astype(F32)).astype(BF16)
    return hi, mid, lo


def _mm(a, b, dims=NN, passes=3):
    if passes == 1:
        return _dg(a.astype(BF16), b.astype(BF16), dims)
    ah, al = _split2(a)
    bh, bl = _split2(b)
    return _dg(ah, bh, dims) + (_dg(ah, bl, dims) + _dg(al, bh, dims))


def _mm_exact_lhs(a_bf16, b, dims=NN):
    bh, bm, bl = _split3(b)
    return _dg(a_bf16, bh, dims) + (_dg(a_bf16, bm, dims) + _dg(a_bf16, bl, dims))


def _mm_exact_rhs(a, b_bf16, dims=NN):
    ah, am, al = _split3(a)
    return _dg(ah, b_bf16, dims) + (_dg(am, b_bf16, dims) + _dg(al, b_bf16, dims))


def _stack_rows_128(parts):
    n = sum(p.shape[0] for p in parts)
    if n < 128:
        parts = list(parts) + [jnp.zeros((128 - n, 128), F32)]
    return jnp.concatenate(parts, axis=0) if len(parts) > 1 else parts[0]


def _sigmoid(x):
    return 1.0 / (1.0 + jnp.exp(-x))


def _rms(x, g):
    ms = jnp.mean(x * x, axis=-1, keepdims=True)
    return x * lax.rsqrt(ms + EPS) * g


def _const_spec(shape):
    nd = len(shape)
    return pl.BlockSpec(shape, lambda *_: (0,) * nd)


def _params(n_grid):
    return pltpu.CompilerParams(dimension_semantics=("arbitrary",) * n_grid,
                                vmem_limit_bytes=VMEM_LIMIT)


def _norm_proj_kernel(x_ref, g_ref, *refs):
    n = len(refs) // 2
    h = _rms(x_ref[...], g_ref[...]).astype(BF16)
    for w_ref, o_ref in zip(refs[:n], refs[n:]):
        o_ref[...] = jnp.dot(h, w_ref[...], preferred_element_type=F32)


def _norm_proj(x, g, ws, tm=256):
    m = x.shape[0]
    return pl.pallas_call(
        _norm_proj_kernel,
        grid=(m // tm,),
        in_specs=[pl.BlockSpec((tm, D_MODEL), lambda i: (i, 0)), _const_spec((1, D_MODEL))]
        + [_const_spec(w.shape) for w in ws],
        out_specs=[pl.BlockSpec((tm, w.shape[1]), lambda i: (i, 0)) for w in ws],
        out_shape=[jax.ShapeDtypeStruct((m, w.shape[1]), F32) for w in ws],
        compiler_params=_params(1),
        name="norm_proj",
    )(x, g.reshape(1, D_MODEL), *ws)


def _t5_bucket_np(dist):
    max_exact = N_BUCKETS // 2
    d = np.maximum(dist, 1).astype(np.float32)
    large = max_exact + (np.log(d / np.float32(max_exact)) / np.float32(math.log(BUCKET_MAX_DIST / max_exact))
                         * np.float32(N_BUCKETS - max_exact)).astype(np.int32)
    large = np.minimum(large, N_BUCKETS - 1)
    return np.where(dist < max_exact, dist, large).astype(np.int32)


def _bias_from_buckets(idx, rb_ref, head):
    acc = jnp.full(idx.shape, NEG, F32)
    for b in range(N_BUCKETS):
        acc = jnp.where(idx == b, rb_ref[b, head], acc)
    return acc


def _attn_prompt_kernel(rb_ref, bkt_ref, q_ref, kp_ref, kc_ref, vp_ref, vc_ref, o_ref, l_ref, bias_scr, *, group):
    first = (pl.program_id(0) == 0) & (pl.program_id(1) == 0) & (pl.program_id(2) == 0)

    @pl.when(first)
    def _():
        idx = bkt_ref[...]
        for h in range(4):
            bias_scr[h] = _bias_from_buckets(idx, rb_ref, group * 4 + h)

    prev_pen = jnp.where(pl.program_id(2) == 0, NEG, 0.0).astype(F32)
    q = (q_ref[...] * (HEAD ** -0.5)).astype(BF16)
    kp = kp_ref[...].astype(BF16)
    kc = kc_ref[...].astype(BF16)
    vp = vp_ref[...].astype(BF16)
    vc = vc_ref[...].astype(BF16)
    outs, lses = [], []
    for h in range(4):
        sl = slice(h * HEAD, (h + 1) * HEAD)
        bias = bias_scr[h]
        s_p = _dg(q[:, sl], kp[:, sl], NT) + (bias[:, :A_KEYS] + prev_pen)
        s_c = _dg(q[:, sl], kc[:, sl], NT) + bias[:, A_KEYS:]
        m = jnp.maximum(jnp.max(s_p, axis=-1, keepdims=True), jnp.max(s_c, axis=-1, keepdims=True))
        p_p = jnp.exp(s_p - m)
        p_c = jnp.exp(s_c - m)
        den = jnp.sum(p_p, axis=-1, keepdims=True) + jnp.sum(p_c, axis=-1, keepdims=True)
        o = _dg(p_p.astype(BF16), vp[:, sl], NN) + _dg(p_c.astype(BF16), vc[:, sl], NN)
        outs.append(o / den)
        lses.append(jnp.broadcast_to(m + jnp.log(den), o.shape))
    o_ref[...] = jnp.concatenate(outs, axis=-1)
    l_ref[...] = jnp.concatenate(lses, axis=-1)


def _prompt_bucket_map(dil):
    qi = np.arange(A_KEYS)[:, None]
    ki = np.arange(2 * A_KEYS)[None, :]
    j = qi + A_KEYS - ki
    return np.where((j >= 0) & (j <= A_KEYS), _t5_bucket_np(np.clip(j, 0, A_KEYS) * dil), -1).astype(np.int32)


def _attn_prompt_group(qkv, rel_bias, group):
    bsz, t, _ = qkv.shape
    dil = A_GROUPS[group][1]
    n = t // dil
    nblk = n // A_KEYS
    view = qkv.reshape(bsz, n, dil * A_QKV)
    cols = A_QKV // A_OUT

    def spec(part, prev):
        def idx(b, r, i):
            return (b, jnp.maximum(i - 1, 0) if prev else i, r * cols + part * 3 + group)
        return pl.BlockSpec((None, A_KEYS, A_OUT), idx)

    out_spec = pl.BlockSpec((None, A_KEYS, A_OUT), lambda b, r, i: (b, i, r))
    o, l = pl.pallas_call(
        functools.partial(_attn_prompt_kernel, group=group),
        grid=(bsz, dil, nblk),
        in_specs=[pl.BlockSpec(memory_space=pltpu.SMEM), _const_spec((A_KEYS, 2 * A_KEYS)),
                  spec(0, False), spec(1, True), spec(1, False), spec(2, True), spec(2, False)],
        out_specs=[out_spec, out_spec],
        out_shape=[jax.ShapeDtypeStruct((bsz, n, dil * A_OUT), F32)] * 2,
        scratch_shapes=[pltpu.VMEM((4, A_KEYS, 2 * A_KEYS), F32)],
        compiler_params=_params(3),
        name=f"attn_prompt_g{group}",
    )(rel_bias, jnp.asarray(_prompt_bucket_map(dil)), view, view, view, view, view)
    return o.reshape(bsz, t, A_OUT), l.reshape(bsz, t, A_OUT)


def _merge_kernel(o0, o1, o2, l0, l1, l2, out_ref):
    m = jnp.maximum(jnp.maximum(l0[...], l1[...]), l2[...])
    w0 = jnp.exp(l0[...] - m)
    w1 = jnp.exp(l1[...] - m)
    w2 = jnp.exp(l2[...] - m)
    out_ref[...] = (w0 * o0[...] + w1 * o1[...] + w2 * o2[...]) / (w0 + w1 + w2)


def _merge_groups(os, ls, tm=512):
    m = os[0].shape[0]
    spec = pl.BlockSpec((tm, A_OUT), lambda i: (i, 0))
    return pl.pallas_call(
        _merge_kernel, grid=(m // tm,), in_specs=[spec] * 6, out_specs=spec,
        out_shape=jax.ShapeDtypeStruct((m, A_OUT), F32), compiler_params=_params(1), name="attn_merge",
    )(*os, *ls)


def _attn_sample_kernel(rb_ref, bc0, bc1, bc2, bn0, bn1, bn2, qkv_ref, c0_ref, c1_ref, c2_ref, out_ref,
                        bias_c0, bias_c1, bias_c2, bias_n):
    bias_c = (bias_c0, bias_c1, bias_c2)

    @pl.when(pl.program_id(0) == 0)
    def _():
        for g, (bc, bn) in enumerate(((bc0, bn0), (bc1, bn1), (bc2, bn2))):
            for h in range(4):
                bias_c[g][h] = _bias_from_buckets(bc[...], rb_ref, g * 4 + h)
                bias_n[g * 4 + h] = _bias_from_buckets(bn[...], rb_ref, g * 4 + h)

    qkv = qkv_ref[...]
    outs, lses = [], []
    for g, c_ref in enumerate((c0_ref, c1_ref, c2_ref)):
        cache = c_ref[...].astype(BF16)
        for h in range(4):
            col = (g * 4 + h) * HEAD
            q = (qkv[:, col:col + HEAD] * (HEAD ** -0.5)).astype(BF16)
            kn = qkv[:, 768 + col:768 + col + HEAD].astype(BF16)
            vn = qkv[:, 1536 + col:1536 + col + HEAD].astype(BF16)
            kc = cache[:, h * HEAD:(h + 1) * HEAD]
            vc = cache[:, A_OUT + h * HEAD:A_OUT + (h + 1) * HEAD]
            s_c = _dg(q, kc, NT) + bias_c[g][h]
            s_n = _dg(q, kn, NT) + bias_n[g * 4 + h]
            m = jnp.maximum(jnp.max(s_c, axis=-1, keepdims=True), jnp.max(s_n, axis=-1, keepdims=True))
            p_c = jnp.exp(s_c - m)
            p_n = jnp.exp(s_n - m)
            den = jnp.sum(p_c, axis=-1, keepdims=True) + jnp.sum(p_n, axis=-1, keepdims=True)
            o = _dg(p_c.astype(BF16), vc, NN) + _dg(p_n.astype(BF16), vn, NN)
            outs.append(o / den)
            lses.append(m + jnp.log(den))
    merged = []
    for h in range(4):
        l0, l1, l2 = lses[h], lses[4 + h], lses[8 + h]
        m = jnp.maximum(jnp.maximum(l0, l1), l2)
        w0, w1, w2 = jnp.exp(l0 - m), jnp.exp(l1 - m), jnp.exp(l2 - m)
        merged.append((w0 * outs[h] + w1 * outs[4 + h] + w2 * outs[8 + h]) / (w0 + w1 + w2))
    out_ref[...] = jnp.concatenate(merged, axis=-1)


def _sample_bucket_maps(window, dil, tp, t_real):
    t = np.arange(tp)[:, None]
    dist_c = window + t - np.arange(window)[None, :]
    ok_c = (dist_c % dil == 0) & (dist_c // dil <= A_KEYS) & (t < t_real)
    dist_n = t - np.arange(tp)[None, :]
    ok_n = (dist_n >= 0) & (dist_n % dil == 0) & (dist_n // dil <= A_KEYS)
    mc = np.where(ok_c, _t5_bucket_np(np.maximum(dist_c, 0)), -1).astype(np.int32)
    mn = np.where(ok_n, _t5_bucket_np(np.maximum(dist_n, 0)), -1).astype(np.int32)
    return mc, mn


def _attn_sample(qkv, caches, rel_bias, t_real):
    bsz, tp, _ = qkv.shape
    maps = [_sample_bucket_maps(w, d, tp, t_real) for w, d in A_GROUPS]
    mcs = [jnp.asarray(m[0]) for m in maps]
    mns = [jnp.asarray(m[1]) for m in maps]
    return pl.pallas_call(
        _attn_sample_kernel,
        grid=(bsz,),
        in_specs=[pl.BlockSpec(memory_space=pltpu.SMEM)]
        + [_const_spec(m.shape) for m in mcs] + [_const_spec(m.shape) for m in mns]
        + [pl.BlockSpec((None, tp, A_QKV), lambda b: (b, 0, 0))]
        + [pl.BlockSpec((None, w, 2 * A_OUT), lambda b: (b, 0, 0)) for w, _ in A_GROUPS],
        out_specs=pl.BlockSpec((None, tp, A_OUT), lambda b: (b, 0, 0)),
        out_shape=jax.ShapeDtypeStruct((bsz, tp, A_OUT), F32),
        scratch_shapes=[pltpu.VMEM((4, tp, w), F32) for w, _ in A_GROUPS] + [pltpu.VMEM((12, tp, tp), F32)],
        compiler_params=_params(1),
        name="attn_sample",
    )(rel_bias, *mcs, *mns, qkv, *caches)


def _neumann_offset(n, steps):
    x = n
    p = n
    for _ in range(steps):
        p = _mm(p, p)
        x = x + p + _mm(x, p)
    return x


def _rwkv_kernel(rkv_ref, lora_ref, sh_rkv_ref, sh_lora_ref, s0_ref, mu_rkv_ref, mu_lora_ref, w0_ref, w2_ref,
                 a0_ref, a2_ref, g2_ref, kk_ref, ka_ref, rk_ref, lnw_ref, lnb_ref, ones_ref, tri_ref,
                 out_ref, s_out_ref, s_scr, prev_rkv, prev_lora, *, chunk, t_real):
    c = pl.program_id(1)
    nc = pl.num_programs(1)

    @pl.when(c == 0)
    def _():
        s_scr[...] = s0_ref[...]
        prev_rkv[...] = sh_rkv_ref[...]
        prev_lora[...] = sh_lora_ref[...]

    rows = lax.broadcasted_iota(jnp.int32, (chunk, 1), 0)
    first_row = rows == 0
    pb = rkv_ref[...]
    lr = lora_ref[...]
    pb_prev = jnp.where(first_row, prev_rkv[...], pltpu.roll(pb, 1, 0))
    lr_prev = jnp.where(first_row, prev_lora[...], pltpu.roll(lr, 1, 0))
    prev_rkv[...] = pb[chunk - 1:chunk, :]
    prev_lora[...] = lr[chunk - 1:chunk, :]
    xs = pb + (pb_prev - pb) * mu_rkv_ref[...]
    xl = lr + (lr_prev - lr) * mu_lora_ref[...]
    r = xs[:, :B_WIDTH]
    k = xs[:, B_WIDTH:2 * B_WIDTH]
    v = xs[:, 2 * B_WIDTH:]
    z = w0_ref[...] + _mm(jnp.tanh(xl[:, :128]), w2_ref[...])
    lam = -DECAY_SCALE * _sigmoid(z)
    a = _sigmoid(a0_ref[...] + _mm(xl[:, 128:256], a2_ref[...]))
    gate = _mm(_sigmoid(xl[:, 256:]), g2_ref[...])
    kk = k * kk_ref[...]
    kmod = k * (1.0 + (a - 1.0) * ka_ref[...])
    ones = ones_ref[...]
    if t_real < chunk:
        live = rows < t_real
        lam = jnp.where(live, lam, 0.0)
        kk = jnp.where(live, kk, 0.0)
        kmod = jnp.where(live, kmod, 0.0)
        v = jnp.where(live, v, 0.0)

    cum = _mm_exact_lhs(tri_ref[...], lam)
    cum_end = cum[chunk - 1:chunk, :]
    e_in = jnp.exp(cum)
    e_prev = jnp.exp(cum - lam)
    e_neg = jnp.exp(-cum)
    e_end = jnp.exp(cum_end - cum)
    g_end = jnp.exp(cum_end)

    ri = lax.broadcasted_iota(jnp.int32, (chunk, chunk), 0)
    ci = lax.broadcasted_iota(jnp.int32, (chunk, chunk), 1)
    strict = ci < ri
    lower = ci <= ri
    lane = lax.broadcasted_iota(jnp.int32, (1, 128), 1)
    h1 = lane < HEAD
    bi = lax.broadcasted_iota(jnp.int32, (128, 128), 0) // HEAD
    bj = lax.broadcasted_iota(jnp.int32, (128, 128), 1) // HEAD
    block_diag = bi == bj
    steps = int(math.log2(chunk)) - 1

    for p in range(B_PAIRS):
        sl = slice(p * 128, (p + 1) * 128)
        kkp = kk[:, sl]
        ss = _mm_exact_rhs(kkp * kkp, ones)
        kappa = kkp / jnp.maximum(jnp.sqrt(ss), 1e-12)
        bb = kappa * a[:, sl]
        kp = kmod[:, sl]
        vp = v[:, sl]
        rp = r[:, sl]
        a_t = -kappa * e_prev[:, sl]
        r_t = rp * e_in[:, sl]
        b_t = bb * e_neg[:, sl]
        k_t = kp * e_neg[:, sl]
        b_e = bb * e_end[:, sl]
        k_e = kp * e_end[:, sl]
        zero = jnp.zeros_like(a_t)
        l4 = jnp.concatenate([jnp.where(h1, a_t, zero), jnp.where(h1, zero, a_t),
                              jnp.where(h1, r_t, zero), jnp.where(h1, zero, r_t)], axis=0)
        mb = _mm(l4, b_t, NT)
        mk = _mm(l4, k_t, NT)
        n1 = jnp.where(strict, mb[0:chunk], 0.0)
        n2 = jnp.where(strict, mb[chunk:2 * chunk], 0.0)
        mak1 = jnp.where(strict, mk[0:chunk], 0.0)
        mak2 = jnp.where(strict, mk[chunk:2 * chunk], 0.0)
        mrb1 = jnp.where(lower, mb[2 * chunk:3 * chunk], 0.0)
        mrb2 = jnp.where(lower, mb[3 * chunk:], 0.0)
        mrk1 = jnp.where(lower, mk[2 * chunk:3 * chunk], 0.0)
        mrk2 = jnp.where(lower, mk[3 * chunk:], 0.0)
        x1 = _neumann_offset(n1, steps)
        x2 = _neumann_offset(n2, steps)
        wv = jnp.where(h1, _mm(mak1, vp), _mm(mak2, vp))
        pv = wv + jnp.where(h1, _mm(x1, wv), _mm(x2, wv))
        ta = a_t + jnp.where(h1, _mm(x1, a_t), _mm(x2, a_t))
        s = s_scr[p]
        u = _mm(ta, s, NT) + pv
        y = _mm(r_t, s, NT) + jnp.where(h1, _mm(mrb1, u) + _mm(mrk1, vp), _mm(mrb2, u) + _mm(mrk2, vp))
        s_new = s * g_end[:, sl] + _mm(_stack_rows_128([u, vp]).T, _stack_rows_128([b_e, k_e]))
        s_scr[p] = jnp.where(block_diag, s_new, 0.0)

        mean = _mm_exact_rhs(y, ones) * (1.0 / HEAD)
        dlt = y - mean
        var = _mm_exact_rhs(dlt * dlt, ones) * (1.0 / HEAD)
        yn = dlt * lax.rsqrt(var + LNX_EPS) * lnw_ref[:, sl] + lnb_ref[:, sl]
        bonus = _mm_exact_rhs(rp * kp * rk_ref[:, sl], ones) * vp
        out_ref[:, sl] = (yn + bonus) * gate[:, sl]

    @pl.when(c == nc - 1)
    def _():
        s_out_ref[...] = s_scr[...]


def _pair_states(s):
    bsz = s.shape[0]
    s = s.reshape(bsz, B_PAIRS, 2, HEAD, HEAD)
    z = jnp.zeros_like(s[:, :, 0])
    top = jnp.concatenate([s[:, :, 0], z], axis=-1)
    bot = jnp.concatenate([z, s[:, :, 1]], axis=-1)
    return jnp.concatenate([top, bot], axis=-2)


def _unpair_states(s):
    bsz = s.shape[0]
    return jnp.stack([s[:, :, :HEAD, :HEAD], s[:, :, HEAD:, HEAD:]], axis=2).reshape(bsz, 2 * B_PAIRS, HEAD, HEAD)


def _rwkv(rkv, lora, sh_rkv, sh_lora, s0, prm, chunk, t_real):
    bsz, t, _ = rkv.shape
    nc = t // chunk
    ones = np.kron(np.eye(2), np.ones((HEAD, HEAD))).astype(np.float32)
    tri = np.tril(np.ones((chunk, chunk), np.float32))
    vec = lambda n: _const_spec((1, n))
    row = lambda x: x.reshape(1, -1)
    out, s_out = pl.pallas_call(
        functools.partial(_rwkv_kernel, chunk=chunk, t_real=t_real),
        grid=(bsz, nc),
        in_specs=[pl.BlockSpec((None, chunk, 3 * B_WIDTH), lambda b, c: (b, c, 0)),
                  pl.BlockSpec((None, chunk, LORA_PAD), lambda b, c: (b, c, 0)),
                  pl.BlockSpec((None, 1, 3 * B_WIDTH), lambda b, c: (b, 0, 0)),
                  pl.BlockSpec((None, 1, LORA_PAD), lambda b, c: (b, 0, 0)),
                  pl.BlockSpec((None, B_PAIRS, 128, 128), lambda b, c: (b, 0, 0, 0)),
                  vec(3 * B_WIDTH), vec(LORA_PAD), vec(B_WIDTH), _const_spec((128, B_WIDTH)),
                  vec(B_WIDTH), _const_spec((128, B_WIDTH)), _const_spec((256, B_WIDTH)),
                  vec(B_WIDTH), vec(B_WIDTH), vec(B_WIDTH), vec(B_WIDTH), vec(B_WIDTH),
                  _const_spec((128, 128)), _const_spec((chunk, chunk))],
        out_specs=[pl.BlockSpec((None, chunk, B_WIDTH), lambda b, c: (b, c, 0)),
                   pl.BlockSpec((None, B_PAIRS, 128, 128), lambda b, c: (b, 0, 0, 0))],
        out_shape=[jax.ShapeDtypeStruct((bsz, t, B_WIDTH), F32),
                   jax.ShapeDtypeStruct((bsz, B_PAIRS, 128, 128), F32)],
        scratch_shapes=[pltpu.VMEM((B_PAIRS, 128, 128), F32), pltpu.VMEM((1, 3 * B_WIDTH), F32),
                        pltpu.VMEM((1, LORA_PAD), F32)],
        compiler_params=_params(2),
        name="rwkv7",
    )(rkv, lora, sh_rkv[:, None], sh_lora[:, None], _pair_states(s0),
      row(prm["mu_rkv"]), row(prm["mu_lora"]), row(prm["w0"]), prm["w2"], row(prm["a0"]), prm["a2"], prm["g2"],
      row(prm["k_k"]), row(prm["k_a"]), row(prm["r_k"]), row(prm["lnx_w"]), row(prm["lnx_b"]),
      jnp.asarray(ones, BF16), jnp.asarray(tri, BF16))
    return out, _unpair_states(s_out)


def _gla_kernel(pc_ref, s0_ref, lb_ref, gn_ref, tri_ref, sel_ref, out_ref, s_out_ref, s_scr, *, chunk, t_real):
    c = pl.program_id(1)
    nc = pl.num_programs(1)

    @pl.when(c == 0)
    def _():
        for h in range(C_HEADS):
            s_scr[h] = s0_ref[h].T

    pc = pc_ref[...]
    lb = lb_ref[...]
    xq = pc[:, :C_FDIM]
    q = xq * _sigmoid(xq)
    fg = lb + (1.0 - lb) * _sigmoid(pc[:, C_FDIM:2 * C_FDIM])
    k = 1.0 - fg
    logf = jnp.log(fg)
    v = pc[:, 2 * C_FDIM:2 * C_FDIM + D_MODEL]
    xg = pc[:, 2 * C_FDIM + D_MODEL:]
    rows = lax.broadcasted_iota(jnp.int32, (chunk, 1), 0)
    if t_real < chunk:
        live = rows < t_real
        logf = jnp.where(live, logf, 0.0)
        k = jnp.where(live, k, 0.0)
        v = jnp.where(live, v, 0.0)

    cum = _mm_exact_lhs(tri_ref[...], logf)
    base = _mm_exact_lhs(sel_ref[...], logf)
    cum_end = cum[chunk - 1:chunk, :]
    q_in = q * jnp.exp(cum)
    q_loc = q * jnp.exp(cum - base)
    k_loc = k * jnp.exp(jnp.minimum(base - cum, EXP_CLAMP))
    k_end = k * jnp.exp(cum_end - cum)
    g_end = jnp.exp(cum_end)
    nsub = chunk // GLA_SUB
    k_sub = []
    for i in range(1, nsub):
        ref_i = cum[i * GLA_SUB - 1:i * GLA_SUB, :]
        k_sub.append(k * jnp.exp(jnp.minimum(ref_i - cum, 0.0)))

    ri = lax.broadcasted_iota(jnp.int32, (chunk, chunk), 0)
    ci = lax.broadcasted_iota(jnp.int32, (chunk, chunk), 1)
    diag_mask = (ci <= ri) & (ci // GLA_SUB == ri // GLA_SUB)
    sub_masks = [(ri // GLA_SUB == i) & (ci < i * GLA_SUB) for i in range(1, nsub)]

    outs = []
    for h in range(C_HEADS):
        sl = slice(h * 128, (h + 1) * 128)
        att = jnp.where(diag_mask, _mm(q_loc[:, sl], k_loc[:, sl], NT), 0.0)
        for msk, ks in zip(sub_masks, k_sub):
            att = att + jnp.where(msk, _mm(q_loc[:, sl], ks[:, sl], NT), 0.0)
        s = s_scr[h]
        outs.append(_mm(q_in[:, sl], s, NT) + _mm(att, v[:, sl]))
        s_scr[h] = s * g_end[:, sl] + _mm(_stack_rows_128([v[:, sl]]).T, _stack_rows_128([k_end[:, sl]]))
    o = jnp.concatenate(outs, axis=-1)
    out_ref[...] = _rms(o, gn_ref[...]) * (xg * _sigmoid(xg))

    @pl.when(c == nc - 1)
    def _():
        for h in range(C_HEADS):
            s_out_ref[h] = s_scr[h].T


def _gla(pc, s0, lb, gn, chunk, t_real):
    bsz, t, _ = pc.shape
    nc = t // chunk
    tri = np.tril(np.ones((chunk, chunk), np.float32))
    idx = np.arange(chunk)
    sel = (idx[None, :] < (idx[:, None] // GLA_SUB) * GLA_SUB).astype(np.float32)
    return pl.pallas_call(
        functools.partial(_gla_kernel, chunk=chunk, t_real=t_real),
        grid=(bsz, nc),
        in_specs=[pl.BlockSpec((None, chunk, 4 * D_MODEL), lambda b, c: (b, c, 0)),
                  pl.BlockSpec((None, C_HEADS, 128, 128), lambda b, c: (b, 0, 0, 0)),
                  _const_spec((1, C_FDIM)), _const_spec((1, D_MODEL)),
                  _const_spec((chunk, chunk)), _const_spec((chunk, chunk))],
        out_specs=[pl.BlockSpec((None, chunk, D_MODEL), lambda b, c: (b, c, 0)),
                   pl.BlockSpec((None, C_HEADS, 128, 128), lambda b, c: (b, 0, 0, 0))],
        out_shape=[jax.ShapeDtypeStruct((bsz, t, D_MODEL), F32),
                   jax.ShapeDtypeStruct((bsz, C_HEADS, 128, 128), F32)],
        scratch_shapes=[pltpu.VMEM((C_HEADS, 128, 128), F32)],
        compiler_params=_params(2),
        name="hgrn2",
    )(pc, s0, lb.reshape(1, -1), gn.reshape(1, -1), jnp.asarray(tri, BF16), jnp.asarray(sel, BF16))


def _post_kernel(*refs, n_mix, final):
    x_ref = refs[0]
    mix_refs = refs[1:1 + n_mix]
    w_refs = refs[1 + n_mix:1 + 2 * n_mix]
    gm_ref, up_ref, down_ref = refs[1 + 2 * n_mix:4 + 2 * n_mix]
    gf_ref = refs[4 + 2 * n_mix] if final else None
    o_ref = refs[-1]
    x = x_ref[...]
    for m_ref, w_ref in zip(mix_refs, w_refs):
        x = x + jnp.dot(m_ref[...].astype(BF16), w_ref[...], preferred_element_type=F32)
    h = _rms(x, gm_ref[...]).astype(BF16)
    u = jnp.dot(h, up_ref[...], preferred_element_type=F32)
    u = jnp.square(jnp.maximum(u, 0.0)).astype(BF16)
    x = x + jnp.dot(u, down_ref[...], preferred_element_type=F32)
    if final:
        x = _rms(x, gf_ref[...])
    o_ref[...] = x


def _post(x, mixes, ws, gm, up, down, gf=None, tm=256):
    m = x.shape[0]
    final = gf is not None
    row_spec = lambda n: pl.BlockSpec((tm, n), lambda i: (i, 0))
    args = [x, *mixes, *ws, gm.reshape(1, -1), up, down]
    specs = ([row_spec(D_MODEL)] + [row_spec(a.shape[1]) for a in mixes] + [_const_spec(w.shape) for w in ws]
             + [_const_spec((1, D_MODEL)), _const_spec(up.shape), _const_spec(down.shape)])
    if final:
        args.append(gf.reshape(1, -1))
        specs.append(_const_spec((1, D_MODEL)))
    return pl.pallas_call(
        functools.partial(_post_kernel, n_mix=len(mixes), final=final),
        grid=(m // tm,), in_specs=specs, out_specs=row_spec(D_MODEL),
        out_shape=jax.ShapeDtypeStruct((m, D_MODEL), F32), compiler_params=_params(1), name="post_mlp",
    )(*args)


def _lora_pad_cols(x):
    pad = lambda a, n: jnp.pad(a, [(0, 0)] * (a.ndim - 1) + [(0, n - a.shape[-1])])
    return jnp.concatenate([pad(x[..., :64], 128), pad(x[..., 64:128], 128), pad(x[..., 128:], 256)], axis=-1)


def _lora_unpad_cols(x):
    return jnp.concatenate([x[..., :64], x[..., 128:192], x[..., 256:416]], axis=-1)


def _prepare(rel_bias, norm_mix, norm_mlp, norm_final, e_w_in, e_mu, e_w0, e_w2, e_a0, e_a2, e_g2, e_k_k, e_k_a,
             e_r_k, e_lnx_w, e_lnx_b, e_w_out, c_w_in, c_lb_raw, c_norm, c_w_out, mlp_up, mlp_down):
    pad_rows = lambda a, n: jnp.pad(a, [(0, n - a.shape[0]), (0, 0)])
    w_in = e_w_in[0]
    sm = jax.nn.softmax(c_lb_raw.astype(F32), axis=0)
    lb = (jnp.cumsum(sm, axis=0) - sm[0])[1]
    return dict(
        rel_bias=rel_bias,
        w_qkv=w_in[:, :A_QKV].astype(BF16),
        w_rkv=w_in[:, A_QKV:A_QKV + 3 * B_WIDTH].astype(BF16),
        w_lora=_lora_pad_cols(w_in[:, A_QKV + 3 * B_WIDTH:]).astype(BF16),
        rwkv=dict(mu_rkv=e_mu[0, :3 * B_WIDTH], mu_lora=_lora_pad_cols(e_mu[0, 3 * B_WIDTH:]),
                  w0=e_w0[0], w2=pad_rows(e_w2[0], 128), a0=e_a0[0], a2=pad_rows(e_a2[0], 128),
                  g2=pad_rows(e_g2[0], 256), k_k=e_k_k[0], k_a=e_k_a[0], r_k=e_r_k[0].reshape(-1),
                  lnx_w=e_lnx_w[0], lnx_b=e_lnx_b[0]),
        w_out_a=e_w_out[0, :A_OUT].astype(BF16), w_out_b=e_w_out[0, A_OUT:].astype(BF16),
        c_w_in=c_w_in[0].astype(BF16), c_w_out=c_w_out[0].astype(BF16), lb=lb, c_norm=c_norm[0],
        norm_mix=norm_mix, norm_mlp=norm_mlp, norm_final=norm_final,
        up=mlp_up.astype(BF16), down=mlp_down.astype(BF16))


def _trunk(x, prm, attend, sh_rkv, sh_lora, wkv0, c0, chunk, t_real):
    bsz, t, _ = x.shape
    xf = x.reshape(bsz * t, D_MODEL)
    qkv, rkv, lora = _norm_proj(xf, prm["norm_mix"][0], [prm["w_qkv"], prm["w_rkv"], prm["w_lora"]])
    qkv = qkv.reshape(bsz, t, A_QKV)
    rkv = rkv.reshape(bsz, t, 3 * B_WIDTH)
    lora = lora.reshape(bsz, t, LORA_PAD)
    a_out = attend(qkv)
    b_out, wkv = _rwkv(rkv, lora, sh_rkv, sh_lora, wkv0, prm["rwkv"], chunk, t_real)
    x1 = _post(xf, [a_out.reshape(bsz * t, A_OUT), b_out.reshape(bsz * t, B_WIDTH)],
               [prm["w_out_a"], prm["w_out_b"]], prm["norm_mlp"][0], prm["up"][0], prm["down"][0])
    (pc,) = _norm_proj(x1, prm["norm_mix"][1], [prm["c_w_in"]])
    c_out, c_state = _gla(pc.reshape(bsz, t, 4 * D_MODEL), c0, prm["lb"], prm["c_norm"], chunk, t_real)
    y = _post(x1, [c_out.reshape(bsz * t, D_MODEL)], [prm["c_w_out"]], prm["norm_mlp"][1], prm["up"][1],
              prm["down"][1], gf=prm["norm_final"])
    kv = qkv[:, :t_real].reshape(bsz, t_real, 3, 12, HEAD)[:, :, 1:]
    kv_rows = [kv[:, -min(w, t_real):, :, 4 * g:4 * g + 4][None] for g, (w, _) in enumerate(A_GROUPS)]
    shift = jnp.concatenate([rkv[:, t_real - 1], _lora_unpad_cols(lora[:, t_real - 1])], axis=-1)[None]
    return y.reshape(bsz, t, D_MODEL)[:, :t_real], kv_rows, shift, wkv[None], c_state[None]


def kernel(x_prompt, x_sample, cache_a0, cache_a1, cache_a2, state_b_shift, state_b_wkv, state_c, rel_bias, norm_mix, norm_mlp, norm_final, e_w_in, e_mu, e_w0, e_w2, e_a0, e_a2, e_g2, e_k_k, e_k_a, e_r_k, e_lnx_w, e_lnx_b, e_w_out, c_w_in, c_lb_raw, c_norm, c_w_out, mlp_up, mlp_down):
    prm = _prepare(rel_bias, norm_mix, norm_mlp, norm_final, e_w_in, e_mu, e_w0, e_w2, e_a0, e_a2, e_g2, e_k_k,
                   e_k_a, e_r_k, e_lnx_w, e_lnx_b, e_w_out, c_w_in, c_lb_raw, c_norm, c_w_out, mlp_up, mlp_down)

    bp, tp, _ = x_prompt.shape

    def attend_prompt(qkv):
        parts = [_attn_prompt_group(qkv, prm["rel_bias"], g) for g in range(len(A_GROUPS))]
        flat = lambda a: a.reshape(bp * tp, A_OUT)
        return _merge_groups([flat(o) for o, _ in parts], [flat(l) for _, l in parts])

    y_p, p_kv, p_shift, p_wkv, p_c = _trunk(
        x_prompt, prm, attend_prompt,
        jnp.zeros((bp, 3 * B_WIDTH), F32), jnp.zeros((bp, LORA_PAD), F32),
        jnp.zeros((bp, 12, HEAD, HEAD), F32), jnp.zeros((bp, C_HEADS, 128, 128), F32), chunk=64, t_real=tp)

    bs, ts, _ = x_sample.shape
    t_pad = 16
    x_s = jnp.pad(x_sample, ((0, 0), (0, t_pad - ts), (0, 0)))
    caches = [c[0].reshape(bs, c.shape[2], 2 * A_OUT) for c in (cache_a0, cache_a1, cache_a2)]

    def attend_sample(qkv):
        return _attn_sample(qkv, caches, prm["rel_bias"], ts)

    sh = state_b_shift[0]
    y_s, s_kv, s_shift, s_wkv, s_c = _trunk(
        x_s, prm, attend_sample, sh[:, :3 * B_WIDTH], _lora_pad_cols(sh[:, 3 * B_WIDTH:]),
        state_b_wkv[0], state_c[0], chunk=t_pad, t_real=ts)

    return (y_p, y_s, p_kv[0], p_kv[1], p_kv[2], p_shift, p_wkv, p_c,
            s_kv[0], s_kv[1], s_kv[2], s_shift, s_wkv, s_c)
```

```python
import functools
import math

import jax
import jax.numpy as jnp
import numpy as np
from jax import lax
from jax.experimental import pallas as pl
from jax.experimental.pallas import tpu as pltpu

F32 = jnp.float32
BF16 = jnp.bfloat16

D_MODEL = 1024
D_FF = 4 * D_MODEL
EPS = 1e-6
LNX_EPS = 64e-5
HEAD = 64
A_GROUPS = ((128, 1), (512, 4), (2048, 16))
A_KEYS = 128
A_QKV = 2304
A_OUT = 256
N_BUCKETS = 32
BUCKET_MAX_DIST = 2048
B_WIDTH = 768
B_PAIRS = B_WIDTH // 128
LORA_PAD = 512
C_FDIM = 1024
C_HEADS = 8
NEG = -1e30
DECAY_SCALE = math.exp(-0.5)
GLA_SUB = 16
EXP_CLAMP = 80.0

VMEM_LIMIT = 56 * 1024 * 1024

NN = (((1,), (0,)), ((), ()))
NT = (((1,), (1,)), ((), ()))


def _dg(a, b, dims):
    return lax.dot_general(a, b, dims, preferred_element_type=F32)


def _split2(x):
    hi = x.astype(BF16)
    lo = (x - hi.astype(F32)).astype(BF16)
    return hi, lo


def _split3(x):
    hi = x.astype(BF16)
    r1 = x - hi.astype(F32)
    mid = r1.astype(BF16)
    lo = (r1 - mid.astype(F32)).astype(BF16)
    return hi, mid, lo


def _mm(a, b, dims=NN, passes=1):
    if passes == 1:
        return _dg(a.astype(BF16), b.astype(BF16), dims)
    ah, al = _split2(a)
    bh, bl = _split2(b)
    return _dg(ah, bh, dims) + (_dg(ah, bl, dims) + _dg(al, bh, dims))


def _mm_exact_lhs(a_bf16, b, dims=NN):
    bh, bm, bl = _split3(b)
    return _dg(a_bf16, bh, dims) + (_dg(a_bf16, bm, dims) + _dg(a_bf16, bl, dims))


def _mm_exact_rhs(a, b_bf16, dims=NN):
    ah, am, al = _split3(a)
    return _dg(ah, b_bf16, dims) + (_dg(am, b_bf16, dims) + _dg(al, b_bf16, dims))


def _stack_rows_128(parts):
    n = sum(p.shape[0] for p in parts)
    if n < 128:
        parts = list(parts) + [jnp.zeros((128 - n, 128), F32)]
    return jnp.concatenate(parts, axis=0) if len(parts) > 1 else parts[0]


def _stack_halves(top, bottom):
    n = top.shape[0]
    if n == 64:
        return jnp.concatenate([top, bottom], axis=0)
    pad = jnp.zeros((64 - n, 128), F32)
    return jnp.concatenate([top, pad, bottom, pad], axis=0)


def _sigmoid(x):
    return 1.0 / (1.0 + jnp.exp(-x))


def _rms(x, g):
    ms = jnp.mean(x * x, axis=-1, keepdims=True)
    return x * lax.rsqrt(ms + EPS) * g


def _const_spec(shape):
    nd = len(shape)
    return pl.BlockSpec(shape, lambda *_: (0,) * nd)


def _params(n_grid):
    return pltpu.CompilerParams(dimension_semantics=("arbitrary",) * n_grid,
                                vmem_limit_bytes=VMEM_LIMIT)


def _norm_proj_kernel(x_ref, g_ref, *refs):
    n = len(refs) // 2
    h = _rms(x_ref[...], g_ref[...]).astype(BF16)
    for w_ref, o_ref in zip(refs[:n], refs[n:]):
        o_ref[...] = jnp.dot(h, w_ref[...], preferred_element_type=F32)


def _norm_proj(x, g, ws, tm=256):
    m = x.shape[0]
    return pl.pallas_call(
        _norm_proj_kernel,
        grid=(m // tm,),
        in_specs=[pl.BlockSpec((tm, D_MODEL), lambda i: (i, 0)), _const_spec((1, D_MODEL))]
        + [_const_spec(w.shape) for w in ws],
        out_specs=[pl.BlockSpec((tm, w.shape[1]), lambda i: (i, 0)) for w in ws],
        out_shape=[jax.ShapeDtypeStruct((m, w.shape[1]), F32) for w in ws],
        compiler_params=_params(1),
        name="norm_proj",
    )(x, g.reshape(1, D_MODEL), *ws)


def _t5_bucket_np(dist):
    max_exact = N_BUCKETS // 2
    d = np.maximum(dist, 1).astype(np.float32)
    large = max_exact + (np.log(d / np.float32(max_exact)) / np.float32(math.log(BUCKET_MAX_DIST / max_exact))
                         * np.float32(N_BUCKETS - max_exact)).astype(np.int32)
    large = np.minimum(large, N_BUCKETS - 1)
    return np.where(dist < max_exact, dist, large).astype(np.int32)


def _bias_from_buckets(idx, rb_ref, head):
    acc = jnp.full(idx.shape, NEG, F32)
    for b in range(N_BUCKETS):
        acc = jnp.where(idx == b, rb_ref[b, head], acc)
    return acc


def _attn_prompt_kernel(rb_ref, bkt_ref, q_ref, kp_ref, kc_ref, vp_ref, vc_ref, o_ref, l_ref, bias_scr, *, group):
    first = (pl.program_id(0) == 0) & (pl.program_id(1) == 0) & (pl.program_id(2) == 0)

    @pl.when(first)
    def _():
        idx = bkt_ref[...]
        for h in range(4):
            bias_scr[h] = _bias_from_buckets(idx, rb_ref, group * 4 + h)

    prev_pen = jnp.where(pl.program_id(2) == 0, NEG, 0.0).astype(F32)
    q = (q_ref[...] * (HEAD ** -0.5)).astype(BF16)
    kp = kp_ref[...].astype(BF16)
    kc = kc_ref[...].astype(BF16)
    vp = vp_ref[...].astype(BF16)
    vc = vc_ref[...].astype(BF16)
    outs, lses = [], []
    for h in range(4):
        sl = slice(h * HEAD, (h + 1) * HEAD)
        bias = bias_scr[h]
        s_p = _dg(q[:, sl], kp[:, sl], NT) + (bias[:, :A_KEYS] + prev_pen)
        s_c = _dg(q[:, sl], kc[:, sl], NT) + bias[:, A_KEYS:]
        m = jnp.maximum(jnp.max(s_p, axis=-1, keepdims=True), jnp.max(s_c, axis=-1, keepdims=True))
        p_p = jnp.exp(s_p - m)
        p_c = jnp.exp(s_c - m)
        den = jnp.sum(p_p, axis=-1, keepdims=True) + jnp.sum(p_c, axis=-1, keepdims=True)
        o = _dg(p_p.astype(BF16), vp[:, sl], NN) + _dg(p_c.astype(BF16), vc[:, sl], NN)
        outs.append(o / den)
        lses.append(jnp.broadcast_to(m + jnp.log(den), o.shape))
    o_ref[...] = jnp.concatenate(outs, axis=-1)
    l_ref[...] = jnp.concatenate(lses, axis=-1)


def _prompt_bucket_map(dil):
    qi = np.arange(A_KEYS)[:, None]
    ki = np.arange(2 * A_KEYS)[None, :]
    j = qi + A_KEYS - ki
    return np.where((j >= 0) & (j <= A_KEYS), _t5_bucket_np(np.clip(j, 0, A_KEYS) * dil), -1).astype(np.int32)


def _attn_prompt_group(qkv, rel_bias, group):
    bsz, t, _ = qkv.shape
    dil = A_GROUPS[group][1]
    n = t // dil
    nblk = n // A_KEYS
    view = qkv.reshape(bsz, n, dil * A_QKV)
    cols = A_QKV // A_OUT

    def spec(part, prev):
        def idx(b, r, i):
            return (b, jnp.maximum(i - 1, 0) if prev else i, r * cols + part * 3 + group)
        return pl.BlockSpec((None, A_KEYS, A_OUT), idx)

    out_spec = pl.BlockSpec((None, A_KEYS, A_OUT), lambda b, r, i: (b, i, r))
    o, l = pl.pallas_call(
        functools.partial(_attn_prompt_kernel, group=group),
        grid=(bsz, dil, nblk),
        in_specs=[pl.BlockSpec(memory_space=pltpu.SMEM), _const_spec((A_KEYS, 2 * A_KEYS)),
                  spec(0, False), spec(1, True), spec(1, False), spec(2, True), spec(2, False)],
        out_specs=[out_spec, out_spec],
        out_shape=[jax.ShapeDtypeStruct((bsz, n, dil * A_OUT), F32)] * 2,
        scratch_shapes=[pltpu.VMEM((4, A_KEYS, 2 * A_KEYS), F32)],
        compiler_params=_params(3),
        name=f"attn_prompt_g{group}",
    )(rel_bias, jnp.asarray(_prompt_bucket_map(dil)), view, view, view, view, view)
    return o.reshape(bsz, t, A_OUT), l.reshape(bsz, t, A_OUT)


def _merge_kernel(o0, o1, o2, l0, l1, l2, out_ref):
    m = jnp.maximum(jnp.maximum(l0[...], l1[...]), l2[...])
    w0 = jnp.exp(l0[...] - m)
    w1 = jnp.exp(l1[...] - m)
    w2 = jnp.exp(l2[...] - m)
    out_ref[...] = (w0 * o0[...] + w1 * o1[...] + w2 * o2[...]) / (w0 + w1 + w2)


def _merge_groups(os, ls, tm=512):
    m = os[0].shape[0]
    spec = pl.BlockSpec((tm, A_OUT), lambda i: (i, 0))
    return pl.pallas_call(
        _merge_kernel, grid=(m // tm,), in_specs=[spec] * 6, out_specs=spec,
        out_shape=jax.ShapeDtypeStruct((m, A_OUT), F32), compiler_params=_params(1), name="attn_merge",
    )(*os, *ls)


def _attn_sample_kernel(rb_ref, bc0, bc1, bc2, bn0, bn1, bn2, qkv_ref, c0_ref, c1_ref, c2_ref, out_ref,
                        bias_c0, bias_c1, bias_c2, bias_n):
    bias_c = (bias_c0, bias_c1, bias_c2)

    @pl.when(pl.program_id(0) == 0)
    def _():
        for g, (bc, bn) in enumerate(((bc0, bn0), (bc1, bn1), (bc2, bn2))):
            for h in range(4):
                bias_c[g][h] = _bias_from_buckets(bc[...], rb_ref, g * 4 + h)
                bias_n[g * 4 + h] = _bias_from_buckets(bn[...], rb_ref, g * 4 + h)

    qkv = qkv_ref[...]
    outs, lses = [], []
    for g, c_ref in enumerate((c0_ref, c1_ref, c2_ref)):
        cache = c_ref[...].astype(BF16)
        for h in range(4):
            col = (g * 4 + h) * HEAD
            q = (qkv[:, col:col + HEAD] * (HEAD ** -0.5)).astype(BF16)
            kn = qkv[:, 768 + col:768 + col + HEAD].astype(BF16)
            vn = qkv[:, 1536 + col:1536 + col + HEAD].astype(BF16)
            kc = cache[:, h * HEAD:(h + 1) * HEAD]
            vc = cache[:, A_OUT + h * HEAD:A_OUT + (h + 1) * HEAD]
            s_c = _dg(q, kc, NT) + bias_c[g][h]
            s_n = _dg(q, kn, NT) + bias_n[g * 4 + h]
            m = jnp.maximum(jnp.max(s_c, axis=-1, keepdims=True), jnp.max(s_n, axis=-1, keepdims=True))
            p_c = jnp.exp(s_c - m)
            p_n = jnp.exp(s_n - m)
            den = jnp.sum(p_c, axis=-1, keepdims=True) + jnp.sum(p_n, axis=-1, keepdims=True)
            o = _dg(p_c.astype(BF16), vc, NN) + _dg(p_n.astype(BF16), vn, NN)
            outs.append(o / den)
            lses.append(m + jnp.log(den))
    merged = []
    for h in range(4):
        l0, l1, l2 = lses[h], lses[4 + h], lses[8 + h]
        m = jnp.maximum(jnp.maximum(l0, l1), l2)
        w0, w1, w2 = jnp.exp(l0 - m), jnp.exp(l1 - m), jnp.exp(l2 - m)
        merged.append((w0 * outs[h] + w1 * outs[4 + h] + w2 * outs[8 + h]) / (w0 + w1 + w2))
    out_ref[...] = jnp.concatenate(merged, axis=-1)


def _sample_bucket_maps(window, dil, tp, t_real):
    t = np.arange(tp)[:, None]
    dist_c = window + t - np.arange(window)[None, :]
    ok_c = (dist_c % dil == 0) & (dist_c // dil <= A_KEYS) & (t < t_real)
    dist_n = t - np.arange(tp)[None, :]
    ok_n = (dist_n >= 0) & (dist_n % dil == 0) & (dist_n // dil <= A_KEYS)
    mc = np.where(ok_c, _t5_bucket_np(np.maximum(dist_c, 0)), -1).astype(np.int32)
    mn = np.where(ok_n, _t5_bucket_np(np.maximum(dist_n, 0)), -1).astype(np.int32)
    return mc, mn


def _attn_sample(qkv, caches, rel_bias, t_real):
    bsz, tp, _ = qkv.shape
    maps = [_sample_bucket_maps(w, d, tp, t_real) for w, d in A_GROUPS]
    mcs = [jnp.asarray(m[0]) for m in maps]
    mns = [jnp.asarray(m[1]) for m in maps]
    return pl.pallas_call(
        _attn_sample_kernel,
        grid=(bsz,),
        in_specs=[pl.BlockSpec(memory_space=pltpu.SMEM)]
        + [_const_spec(m.shape) for m in mcs] + [_const_spec(m.shape) for m in mns]
        + [pl.BlockSpec((None, tp, A_QKV), lambda b: (b, 0, 0))]
        + [pl.BlockSpec((None, w, 2 * A_OUT), lambda b: (b, 0, 0)) for w, _ in A_GROUPS],
        out_specs=pl.BlockSpec((None, tp, A_OUT), lambda b: (b, 0, 0)),
        out_shape=jax.ShapeDtypeStruct((bsz, tp, A_OUT), F32),
        scratch_shapes=[pltpu.VMEM((4, tp, w), F32) for w, _ in A_GROUPS] + [pltpu.VMEM((12, tp, tp), F32)],
        compiler_params=_params(1),
        name="attn_sample",
    )(rel_bias, *mcs, *mns, qkv, *caches)


def _rwkv_kernel(rkv_ref, lora_ref, sh_rkv_ref, sh_lora_ref, s0_ref, mu_rkv_ref, mu_lora_ref, w0_ref, w2_ref,
                 a0_ref, a2_ref, g2_ref, kk_ref, ka_ref, rk_ref, lnw_ref, lnb_ref, ones_ref, tri_ref,
                 out_ref, s_out_ref, s_scr, prev_rkv, prev_lora, *, chunk, t_real):
    c = pl.program_id(1)
    nc = pl.num_programs(1)

    @pl.when(c == 0)
    def _():
        s_scr[...] = s0_ref[...]
        prev_rkv[...] = sh_rkv_ref[...]
        prev_lora[...] = sh_lora_ref[...]

    rows = lax.broadcasted_iota(jnp.int32, (chunk, 1), 0)
    first_row = rows == 0
    pb = rkv_ref[...]
    lr = lora_ref[...]
    pb_prev = jnp.where(first_row, prev_rkv[...], pltpu.roll(pb, 1, 0))
    lr_prev = jnp.where(first_row, prev_lora[...], pltpu.roll(lr, 1, 0))
    prev_rkv[...] = pb[chunk - 1:chunk, :]
    prev_lora[...] = lr[chunk - 1:chunk, :]
    xs = pb + (pb_prev - pb) * mu_rkv_ref[...]
    xl = lr + (lr_prev - lr) * mu_lora_ref[...]
    r = xs[:, :B_WIDTH]
    k = xs[:, B_WIDTH:2 * B_WIDTH]
    v = xs[:, 2 * B_WIDTH:]
    z = w0_ref[...] + _mm(jnp.tanh(xl[:, :128]), w2_ref[...], passes=3)
    lam = -DECAY_SCALE * _sigmoid(z)
    a = _sigmoid(a0_ref[...] + _mm(xl[:, 128:256], a2_ref[...]))
    gate = _mm(_sigmoid(xl[:, 256:]), g2_ref[...])
    kk = k * kk_ref[...]
    kmod = k * (1.0 + (a - 1.0) * ka_ref[...])
    ones = ones_ref[...]
    if t_real < chunk:
        live = rows < t_real
        lam = jnp.where(live, lam, 0.0)
        kk = jnp.where(live, kk, 0.0)
        kmod = jnp.where(live, kmod, 0.0)
        v = jnp.where(live, v, 0.0)

    cum = _mm_exact_lhs(tri_ref[...], lam)
    cum_end = cum[chunk - 1:chunk, :]
    e_in = jnp.exp(cum)
    e_prev = jnp.exp(cum - lam)
    e_neg = jnp.exp(-cum)
    e_end = jnp.exp(cum_end - cum)
    g_end = jnp.exp(cum_end)

    ri = lax.broadcasted_iota(jnp.int32, (chunk, 128), 0)
    ci = lax.broadcasted_iota(jnp.int32, (chunk, 128), 1) % HEAD
    strict = ci < ri
    lower = ci <= ri
    lane = lax.broadcasted_iota(jnp.int32, (1, 128), 1)
    h1 = lane < HEAD
    bi = lax.broadcasted_iota(jnp.int32, (128, 128), 0) // HEAD
    bj = lax.broadcasted_iota(jnp.int32, (128, 128), 1) // HEAD
    block_diag = bi == bj
    levels = int(math.log2(chunk))
    pairs = range(B_PAIRS)
    sls = [slice(p * 128, (p + 1) * 128) for p in pairs]
    zero = jnp.zeros((chunk, 128), F32)

    kkp = [kk[:, sl] for sl in sls]
    ss = [_mm_exact_rhs(x * x, ones) for x in kkp]
    kappa = [x / jnp.maximum(jnp.sqrt(q), 1e-12) for x, q in zip(kkp, ss)]
    bb = [kappa[p] * a[:, sls[p]] for p in pairs]
    a_t = [-kappa[p] * e_prev[:, sls[p]] for p in pairs]
    r_t = [r[:, sl] * e_in[:, sl] for sl in sls]
    vps = [v[:, sl] for sl in sls]
    m4 = []
    for p in pairs:
        sl = sls[p]
        l4 = jnp.concatenate([jnp.where(h1, a_t[p], zero), jnp.where(h1, zero, a_t[p]),
                              jnp.where(h1, r_t[p], zero), jnp.where(h1, zero, r_t[p])], axis=0)
        bk = _stack_halves(bb[p] * e_neg[:, sl], kmod[:, sl] * e_neg[:, sl])
        m4.append(_mm(l4, bk, NT))
    na = [[jnp.where(strict, m4[p][j * chunk:(j + 1) * chunk], 0.0) for j in range(2)] for p in pairs]
    nr = [[jnp.where(lower, m4[p][(2 + j) * chunk:(3 + j) * chunk], 0.0) for j in range(2)] for p in pairs]
    zv = [_stack_halves(zero, vps[p]) for p in pairs]
    zs = [[jnp.concatenate([a_t[p], _mm(na[p][j], zv[p])], axis=-1) for j in range(2)] for p in pairs]
    ps = [[na[p][j][:, :chunk] for j in range(2)] for p in pairs]
    for lvl in range(levels):
        last = lvl == levels - 1
        for p in pairs:
            for j in range(2):
                rhs = zs[p][j] if last else jnp.concatenate([zs[p][j], ps[p][j]], axis=-1)
                upd = _mm(ps[p][j], rhs)
                zs[p][j] = zs[p][j] + upd[:, :256]
                if not last:
                    ps[p][j] = upd[:, 256:]
    ta = [jnp.where(h1, zs[p][0][:, :128], zs[p][1][:, :128]) for p in pairs]
    pv = [jnp.where(h1, zs[p][0][:, 128:], zs[p][1][:, 128:]) for p in pairs]
    s_old = [s_scr[p] for p in pairs]
    u = [_mm(ta[p], s_old[p], NT) + pv[p] for p in pairs]
    uv = [_stack_halves(u[p], vps[p]) for p in pairs]
    y = [_mm(r_t[p], s_old[p], NT) + jnp.where(h1, _mm(nr[p][0], uv[p]), _mm(nr[p][1], uv[p])) for p in pairs]
    for p in pairs:
        sl = sls[p]
        bk_e = _stack_halves(bb[p] * e_end[:, sl], kmod[:, sl] * e_end[:, sl])
        s_new = s_old[p] * g_end[:, sl] + _mm(uv[p].T, bk_e)
        s_scr[p] = jnp.where(block_diag, s_new, 0.0)

    mean = [_mm_exact_rhs(y[p], ones) * (1.0 / HEAD) for p in pairs]
    dlt = [y[p] - mean[p] for p in pairs]
    var = [_mm_exact_rhs(d * d, ones) * (1.0 / HEAD) for d in dlt]
    bonus = [_mm_exact_rhs(r[:, sl] * kmod[:, sl] * rk_ref[:, sl], ones) for sl in sls]
    for p in pairs:
        sl = sls[p]
        yn = dlt[p] * lax.rsqrt(var[p] + LNX_EPS) * lnw_ref[:, sl] + lnb_ref[:, sl]
        out_ref[:, sl] = (yn + bonus[p] * vps[p]) * gate[:, sl]

    @pl.when(c == nc - 1)
    def _():
        s_out_ref[...] = s_scr[...]


def _pair_states(s):
    bsz = s.shape[0]
    s = s.reshape(bsz, B_PAIRS, 2, HEAD, HEAD)
    z = jnp.zeros_like(s[:, :, 0])
    top = jnp.concatenate([s[:, :, 0], z], axis=-1)
    bot = jnp.concatenate([z, s[:, :, 1]], axis=-1)
    return jnp.concatenate([top, bot], axis=-2)


def _unpair_states(s):
    bsz = s.shape[0]
    return jnp.stack([s[:, :, :HEAD, :HEAD], s[:, :, HEAD:, HEAD:]], axis=2).reshape(bsz, 2 * B_PAIRS, HEAD, HEAD)


def _rwkv(rkv, lora, sh_rkv, sh_lora, s0, prm, chunk, t_real):
    bsz, t, _ = rkv.shape
    nc = t // chunk
    ones = np.kron(np.eye(2), np.ones((HEAD, HEAD))).astype(np.float32)
    tri = np.tril(np.ones((chunk, chunk), np.float32))
    vec = lambda n: _const_spec((1, n))
    row = lambda x: x.reshape(1, -1)
    out, s_out = pl.pallas_call(
        functools.partial(_rwkv_kernel, chunk=chunk, t_real=t_real),
        grid=(bsz, nc),
        in_specs=[pl.BlockSpec((None, chunk, 3 * B_WIDTH), lambda b, c: (b, c, 0)),
                  pl.BlockSpec((None, chunk, LORA_PAD), lambda b, c: (b, c, 0)),
                  pl.BlockSpec((None, 1, 3 * B_WIDTH), lambda b, c: (b, 0, 0)),
                  pl.BlockSpec((None, 1, LORA_PAD), lambda b, c: (b, 0, 0)),
                  pl.BlockSpec((None, B_PAIRS, 128, 128), lambda b, c: (b, 0, 0, 0)),
                  vec(3 * B_WIDTH), vec(LORA_PAD), vec(B_WIDTH), _const_spec((128, B_WIDTH)),
                  vec(B_WIDTH), _const_spec((128, B_WIDTH)), _const_spec((256, B_WIDTH)),
                  vec(B_WIDTH), vec(B_WIDTH), vec(B_WIDTH), vec(B_WIDTH), vec(B_WIDTH),
                  _const_spec((128, 128)), _const_spec((chunk, chunk))],
        out_specs=[pl.BlockSpec((None, chunk, B_WIDTH), lambda b, c: (b, c, 0)),
                   pl.BlockSpec((None, B_PAIRS, 128, 128), lambda b, c: (b, 0, 0, 0))],
        out_shape=[jax.ShapeDtypeStruct((bsz, t, B_WIDTH), F32),
                   jax.ShapeDtypeStruct((bsz, B_PAIRS, 128, 128), F32)],
        scratch_shapes=[pltpu.VMEM((B_PAIRS, 128, 128), F32), pltpu.VMEM((1, 3 * B_WIDTH), F32),
                        pltpu.VMEM((1, LORA_PAD), F32)],
        compiler_params=_params(2),
        name="rwkv7",
    )(rkv, lora, sh_rkv[:, None], sh_lora[:, None], _pair_states(s0),
      row(prm["mu_rkv"]), row(prm["mu_lora"]), row(prm["w0"]), prm["w2"], row(prm["a0"]), prm["a2"], prm["g2"],
      row(prm["k_k"]), row(prm["k_a"]), row(prm["r_k"]), row(prm["lnx_w"]), row(prm["lnx_b"]),
      jnp.asarray(ones, BF16), jnp.asarray(tri, BF16))
    return out, _unpair_states(s_out)


def _gla_kernel(pc_ref, s0_ref, lb_ref, gn_ref, tri_ref, sel_ref, out_ref, s_out_ref, s_scr, *, chunk, t_real):
    c = pl.program_id(1)
    nc = pl.num_programs(1)

    @pl.when(c == 0)
    def _():
        for h in range(C_HEADS):
            s_scr[h] = s0_ref[h].T

    pc = pc_ref[...]
    lb = lb_ref[...]
    xq = pc[:, :C_FDIM]
    q = xq * _sigmoid(xq)
    fg = lb + (1.0 - lb) * _sigmoid(pc[:, C_FDIM:2 * C_FDIM])
    k = 1.0 - fg
    logf = jnp.log(fg)
    v = pc[:, 2 * C_FDIM:2 * C_FDIM + D_MODEL]
    xg = pc[:, 2 * C_FDIM + D_MODEL:]
    rows = lax.broadcasted_iota(jnp.int32, (chunk, 1), 0)
    if t_real < chunk:
        live = rows < t_real
        logf = jnp.where(live, logf, 0.0)
        k = jnp.where(live, k, 0.0)
        v = jnp.where(live, v, 0.0)

    cum = _mm_exact_lhs(tri_ref[...], logf)
    base = _mm_exact_lhs(sel_ref[...], logf)
    cum_end = cum[chunk - 1:chunk, :]
    q_in = q * jnp.exp(cum)
    q_loc = q * jnp.exp(cum - base)
    k_loc = k * jnp.exp(jnp.minimum(base - cum, EXP_CLAMP))
    k_end = k * jnp.exp(cum_end - cum)
    g_end = jnp.exp(cum_end)
    nsub = chunk // GLA_SUB
    k_var = [k_loc]
    for i in range(1, nsub):
        ref_i = cum[i * GLA_SUB - 1:i * GLA_SUB, :]
        k_var.append(k * jnp.exp(jnp.minimum(ref_i - cum, 0.0)))
    k_stack = jnp.concatenate(k_var, axis=0) if nsub > 1 else k_loc
    v_stack = jnp.concatenate([v] * nsub, axis=0) if nsub > 1 else v

    ri = lax.broadcasted_iota(jnp.int32, (chunk, nsub * chunk), 0)
    cc = lax.broadcasted_iota(jnp.int32, (chunk, nsub * chunk), 1)
    var = cc // chunk
    ci = cc % chunk
    same_sub = ci // GLA_SUB == ri // GLA_SUB
    att_mask = ((var == 0) & same_sub & (ci <= ri)) | ((ri // GLA_SUB == var) & (ci < var * GLA_SUB))

    heads = range(C_HEADS)
    sls = [slice(h * 128, (h + 1) * 128) for h in heads]
    att = [jnp.where(att_mask, _mm(q_loc[:, sl], k_stack[:, sl], NT), 0.0) for sl in sls]
    s_old = [s_scr[h] for h in heads]
    outs = [_mm(q_in[:, sls[h]], s_old[h], NT) + _mm(att[h], v_stack[:, sls[h]]) for h in heads]
    for h in heads:
        sl = sls[h]
        s_scr[h] = s_old[h] * g_end[:, sl] + _mm(_stack_rows_128([v[:, sl]]).T, _stack_rows_128([k_end[:, sl]]))
    o = jnp.concatenate(outs, axis=-1)
    out_ref[...] = _rms(o, gn_ref[...]) * (xg * _sigmoid(xg))

    @pl.when(c == nc - 1)
    def _():
        for h in range(C_HEADS):
            s_out_ref[h] = s_scr[h].T


def _gla(pc, s0, lb, gn, chunk, t_real):
    bsz, t, _ = pc.shape
    nc = t // chunk
    tri = np.tril(np.ones((chunk, chunk), np.float32))
    idx = np.arange(chunk)
    sel = (idx[None, :] < (idx[:, None] // GLA_SUB) * GLA_SUB).astype(np.float32)
    return pl.pallas_call(
        functools.partial(_gla_kernel, chunk=chunk, t_real=t_real),
        grid=(bsz, nc),
        in_specs=[pl.BlockSpec((None, chunk, 4 * D_MODEL), lambda b, c: (b, c, 0)),
                  pl.BlockSpec((None, C_HEADS, 128, 128), lambda b, c: (b, 0, 0, 0)),
                  _const_spec((1, C_FDIM)), _const_spec((1, D_MODEL)),
                  _const_spec((chunk, chunk)), _const_spec((chunk, chunk))],
        out_specs=[pl.BlockSpec((None, chunk, D_MODEL), lambda b, c: (b, c, 0)),
                   pl.BlockSpec((None, C_HEADS, 128, 128), lambda b, c: (b, 0, 0, 0))],
        out_shape=[jax.ShapeDtypeStruct((bsz, t, D_MODEL), F32),
                   jax.ShapeDtypeStruct((bsz, C_HEADS, 128, 128), F32)],
        scratch_shapes=[pltpu.VMEM((C_HEADS, 128, 128), F32)],
        compiler_params=_params(2),
        name="hgrn2",
    )(pc, s0, lb.reshape(1, -1), gn.reshape(1, -1), jnp.asarray(tri, BF16), jnp.asarray(sel, BF16))


def _post_kernel(*refs, n_mix, final):
    x_ref = refs[0]
    mix_refs = refs[1:1 + n_mix]
    w_refs = refs[1 + n_mix:1 + 2 * n_mix]
    gm_ref, up_ref, down_ref = refs[1 + 2 * n_mix:4 + 2 * n_mix]
    gf_ref = refs[4 + 2 * n_mix] if final else None
    o_ref = refs[-1]
    x = x_ref[...]
    for m_ref, w_ref in zip(mix_refs, w_refs):
        x = x + jnp.dot(m_ref[...].astype(BF16), w_ref[...], preferred_element_type=F32)
    h = _rms(x, gm_ref[...]).astype(BF16)
    u = jnp.dot(h, up_ref[...], preferred_element_type=F32)
    u = jnp.square(jnp.maximum(u, 0.0)).astype(BF16)
    x = x + jnp.dot(u, down_ref[...], preferred_element_type=F32)
    if final:
        x = _rms(x, gf_ref[...])
    o_ref[...] = x


def _post(x, mixes, ws, gm, up, down, gf=None, tm=256):
    m = x.shape[0]
    final = gf is not None
    row_spec = lambda n: pl.BlockSpec((tm, n), lambda i: (i, 0))
    args = [x, *mixes, *ws, gm.reshape(1, -1), up, down]
    specs = ([row_spec(D_MODEL)] + [row_spec(a.shape[1]) for a in mixes] + [_const_spec(w.shape) for w in ws]
             + [_const_spec((1, D_MODEL)), _const_spec(up.shape), _const_spec(down.shape)])
    if final:
        args.append(gf.reshape(1, -1))
        specs.append(_const_spec((1, D_MODEL)))
    return pl.pallas_call(
        functools.partial(_post_kernel, n_mix=len(mixes), final=final),
        grid=(m // tm,), in_specs=specs, out_specs=row_spec(D_MODEL),
        out_shape=jax.ShapeDtypeStruct((m, D_MODEL), F32), compiler_params=_params(1), name="post_mlp",
    )(*args)


def _lora_pad_cols(x):
    pad = lambda a, n: jnp.pad(a, [(0, 0)] * (a.ndim - 1) + [(0, n - a.shape[-1])])
    return jnp.concatenate([pad(x[..., :64], 128), pad(x[..., 64:128], 128), pad(x[..., 128:], 256)], axis=-1)


def _lora_unpad_cols(x):
    return jnp.concatenate([x[..., :64], x[..., 128:192], x[..., 256:416]], axis=-1)


def _prepare(rel_bias, norm_mix, norm_mlp, norm_final, e_w_in, e_mu, e_w0, e_w2, e_a0, e_a2, e_g2, e_k_k, e_k_a,
             e_r_k, e_lnx_w, e_lnx_b, e_w_out, c_w_in, c_lb_raw, c_norm, c_w_out, mlp_up, mlp_down):
    pad_rows = lambda a, n: jnp.pad(a, [(0, n - a.shape[0]), (0, 0)])
    w_in = e_w_in[0]
    sm = jax.nn.softmax(c_lb_raw.astype(F32), axis=0)
    lb = (jnp.cumsum(sm, axis=0) - sm[0])[1]
    return dict(
        rel_bias=rel_bias,
        w_qkv=w_in[:, :A_QKV].astype(BF16),
        w_rkv=w_in[:, A_QKV:A_QKV + 3 * B_WIDTH].astype(BF16),
        w_lora=_lora_pad_cols(w_in[:, A_QKV + 3 * B_WIDTH:]).astype(BF16),
        rwkv=dict(mu_rkv=e_mu[0, :3 * B_WIDTH], mu_lora=_lora_pad_cols(e_mu[0, 3 * B_WIDTH:]),
                  w0=e_w0[0], w2=pad_rows(e_w2[0], 128), a0=e_a0[0], a2=pad_rows(e_a2[0], 128),
                  g2=pad_rows(e_g2[0], 256), k_k=e_k_k[0], k_a=e_k_a[0], r_k=e_r_k[0].reshape(-1),
                  lnx_w=e_lnx_w[0], lnx_b=e_lnx_b[0]),
        w_out_a=e_w_out[0, :A_OUT].astype(BF16), w_out_b=e_w_out[0, A_OUT:].astype(BF16),
        c_w_in=c_w_in[0].astype(BF16), c_w_out=c_w_out[0].astype(BF16), lb=lb, c_norm=c_norm[0],
        norm_mix=norm_mix, norm_mlp=norm_mlp, norm_final=norm_final,
        up=mlp_up.astype(BF16), down=mlp_down.astype(BF16))


def _trunk(x, prm, attend, sh_rkv, sh_lora, wkv0, c0, chunk, t_real):
    bsz, t, _ = x.shape
    xf = x.reshape(bsz * t, D_MODEL)
    qkv, rkv, lora = _norm_proj(xf, prm["norm_mix"][0], [prm["w_qkv"], prm["w_rkv"], prm["w_lora"]])
    qkv = qkv.reshape(bsz, t, A_QKV)
    rkv = rkv.reshape(bsz, t, 3 * B_WIDTH)
    lora = lora.reshape(bsz, t, LORA_PAD)
    a_out = attend(qkv)
    b_out, wkv = _rwkv(rkv, lora, sh_rkv, sh_lora, wkv0, prm["rwkv"], chunk, t_real)
    x1 = _post(xf, [a_out.reshape(bsz * t, A_OUT), b_out.reshape(bsz * t, B_WIDTH)],
               [prm["w_out_a"], prm["w_out_b"]], prm["norm_mlp"][0], prm["up"][0], prm["down"][0])
    (pc,) = _norm_proj(x1, prm["norm_mix"][1], [prm["c_w_in"]])
    c_out, c_state = _gla(pc.reshape(bsz, t, 4 * D_MODEL), c0, prm["lb"], prm["c_norm"], chunk, t_real)
    y = _post(x1, [c_out.reshape(bsz * t, D_MODEL)], [prm["c_w_out"]], prm["norm_mlp"][1], prm["up"][1],
              prm["down"][1], gf=prm["norm_final"])
    kv = qkv[:, :t_real].reshape(bsz, t_real, 3, 12, HEAD)[:, :, 1:]
    kv_rows = [kv[:, -min(w, t_real):, :, 4 * g:4 * g + 4][None] for g, (w, _) in enumerate(A_GROUPS)]
    shift = jnp.concatenate([rkv[:, t_real - 1], _lora_unpad_cols(lora[:, t_real - 1])], axis=-1)[None]
    return y.reshape(bsz, t, D_MODEL)[:, :t_real], kv_rows, shift, wkv[None], c_state[None]


def kernel(x_prompt, x_sample, cache_a0, cache_a1, cache_a2, state_b_shift, state_b_wkv, state_c, rel_bias, norm_mix, norm_mlp, norm_final, e_w_in, e_mu, e_w0, e_w2, e_a0, e_a2, e_g2, e_k_k, e_k_a, e_r_k, e_lnx_w, e_lnx_b, e_w_out, c_w_in, c_lb_raw, c_norm, c_w_out, mlp_up, mlp_down):
    prm = _prepare(rel_bias, norm_mix, norm_mlp, norm_final, e_w_in, e_mu, e_w0, e_w2, e_a0, e_a2, e_g2, e_k_k,
                   e_k_a, e_r_k, e_lnx_w, e_lnx_b, e_w_out, c_w_in, c_lb_raw, c_norm, c_w_out, mlp_up, mlp_down)

    bp, tp, _ = x_prompt.shape

    def attend_prompt(qkv):
        parts = [_attn_prompt_group(qkv, prm["rel_bias"], g) for g in range(len(A_GROUPS))]
        flat = lambda a: a.reshape(bp * tp, A_OUT)
        return _merge_groups([flat(o) for o, _ in parts], [flat(l) for _, l in parts])

    y_p, p_kv, p_shift, p_wkv, p_c = _trunk(
        x_prompt, prm, attend_prompt,
        jnp.zeros((bp, 3 * B_WIDTH), F32), jnp.zeros((bp, LORA_PAD), F32),
        jnp.zeros((bp, 12, HEAD, HEAD), F32), jnp.zeros((bp, C_HEADS, 128, 128), F32), chunk=64, t_real=tp)

    bs, ts, _ = x_sample.shape
    t_pad = 16
    x_s = jnp.pad(x_sample, ((0, 0), (0, t_pad - ts), (0, 0)))
    caches = [c[0].reshape(bs, c.shape[2], 2 * A_OUT) for c in (cache_a0, cache_a1, cache_a2)]

    def attend_sample(qkv):
        return _attn_sample(qkv, caches, prm["rel_bias"], ts)

    sh = state_b_shift[0]
    y_s, s_kv, s_shift, s_wkv, s_c = _trunk(
        x_s, prm, attend_sample, sh[:, :3 * B_WIDTH], _lora_pad_cols(sh[:, 3 * B_WIDTH:]),
        state_b_wkv[0], state_c[0], chunk=t_pad, t_real=ts)

    return (y_p, y_s, p_kv[0], p_kv[1], p_kv[2], p_shift, p_wkv, p_c,
            s_kv[0], s_kv[1], s_kv[2], s_shift, s_wkv, s_c)
```

```python
import functools
import math

import jax
import jax.numpy as jnp
import numpy as np
from jax import lax
from jax.experimental import pallas as pl
from jax.experimental.pallas import tpu as pltpu

F32 = jnp.float32
BF16 = jnp.bfloat16

D_MODEL = 1024
D_FF = 4 * D_MODEL
EPS = 1e-6
LNX_EPS = 64e-5
HEAD = 64
A_GROUPS = ((128, 1), (512, 4), (2048, 16))
A_KEYS = 128
A_QKV = 2304
A_OUT = 256
N_BUCKETS = 32
BUCKET_MAX_DIST = 2048
B_WIDTH = 768
B_PAIRS = B_WIDTH // 128
LORA_PAD = 512
C_FDIM = 1024
C_HEADS = 8
NEG = -1e30
DECAY_SCALE = math.exp(-0.5)
GLA_SUB = 16
EXP_CLAMP = 80.0
MAX_SEQS_PER_STEP = 4

VMEM_LIMIT = 56 * 1024 * 1024

NN = (((1,), (0,)), ((), ()))
NT = (((1,), (1,)), ((), ()))


def _dg(a, b, dims):
    return lax.dot_general(a, b, dims, preferred_element_type=F32)


def _split2(x):
    hi = x.astype(BF16)
    lo = (x - hi.astype(F32)).astype(BF16)
    return hi, lo


def _split3(x):
    hi = x.astype(BF16)
    r1 = x - hi.astype(F32)
    mid = r1.astype(BF16)
    lo = (r1 - mid.astype(F32)).astype(BF16)
    return hi, mid, lo


def _mm(a, b, dims=NN, passes=1):
    if passes == 1:
        return _dg(a.astype(BF16), b.astype(BF16), dims)
    ah, al = _split2(a)
    bh, bl = _split2(b)
    return _dg(ah, bh, dims) + (_dg(ah, bl, dims) + _dg(al, bh, dims))


def _mm_exact_lhs(a_bf16, b, terms=3):
    if terms == 2:
        bh, bl = _split2(b)
        return _dg(a_bf16, bh, NN) + _dg(a_bf16, bl, NN)
    bh, bm, bl = _split3(b)
    return _dg(a_bf16, bh, NN) + (_dg(a_bf16, bm, NN) + _dg(a_bf16, bl, NN))


def _stack_rows_128(parts):
    n = sum(p.shape[0] for p in parts)
    if n < 128:
        parts = list(parts) + [jnp.zeros((128 - n, 128), F32)]
    return jnp.concatenate(parts, axis=0) if len(parts) > 1 else parts[0]


def _stack_halves(top, bottom):
    n = top.shape[0]
    if n == 64:
        return jnp.concatenate([top, bottom], axis=0)
    pad = jnp.zeros((64 - n, 128), F32)
    return jnp.concatenate([top, pad, bottom, pad], axis=0)


def _sigmoid(x):
    return 1.0 / (1.0 + jnp.exp(-x))


def _rms(x, g):
    ms = jnp.mean(x * x, axis=-1, keepdims=True)
    return x * lax.rsqrt(ms + EPS) * g


def _const_spec(shape):
    nd = len(shape)
    return pl.BlockSpec(shape, lambda *_: (0,) * nd)


def _params(n_grid):
    return pltpu.CompilerParams(dimension_semantics=("arbitrary",) * n_grid,
                                vmem_limit_bytes=VMEM_LIMIT)


def _norm_proj_kernel(x_ref, g_ref, *refs):
    n = len(refs) // 2
    h = _rms(x_ref[...], g_ref[...]).astype(BF16)
    for w_ref, o_ref in zip(refs[:n], refs[n:]):
        o_ref[...] = jnp.dot(h, w_ref[...], preferred_element_type=F32)


def _norm_proj(x, g, ws, tm=256):
    m = x.shape[0]
    return pl.pallas_call(
        _norm_proj_kernel,
        grid=(m // tm,),
        in_specs=[pl.BlockSpec((tm, D_MODEL), lambda i: (i, 0)), _const_spec((1, D_MODEL))]
        + [_const_spec(w.shape) for w in ws],
        out_specs=[pl.BlockSpec((tm, w.shape[1]), lambda i: (i, 0)) for w in ws],
        out_shape=[jax.ShapeDtypeStruct((m, w.shape[1]), F32) for w in ws],
        compiler_params=_params(1),
        name="norm_proj",
    )(x, g.reshape(1, D_MODEL), *ws)


def _t5_bucket_np(dist):
    max_exact = N_BUCKETS // 2
    d = np.maximum(dist, 1).astype(np.float32)
    large = max_exact + (np.log(d / np.float32(max_exact)) / np.float32(math.log(BUCKET_MAX_DIST / max_exact))
                         * np.float32(N_BUCKETS - max_exact)).astype(np.int32)
    large = np.minimum(large, N_BUCKETS - 1)
    return np.where(dist < max_exact, dist, large).astype(np.int32)


def _bias_from_buckets(idx, rb_ref, head):
    acc = jnp.full(idx.shape, NEG, F32)
    for b in range(N_BUCKETS):
        acc = jnp.where(idx == b, rb_ref[b, head], acc)
    return acc


def _attn_prompt_kernel(rb_ref, bkt_ref, q_ref, kp_ref, kc_ref, vp_ref, vc_ref, o_ref, l_ref, bias_scr, *, group, dil):
    pair = pl.program_id(2)
    first = (pl.program_id(0) == 0) & (pl.program_id(1) == 0) & (pair == 0)

    @pl.when(first)
    def _():
        idx = bkt_ref[...]
        for h in range(4):
            bias_scr[h] = _bias_from_buckets(idx, rb_ref, group * 4 + h)

    col = lax.broadcasted_iota(jnp.int32, (1, 2 * A_KEYS), 1)
    prev_pen = jnp.where((col < A_KEYS) & (pl.program_id(1) == 0), NEG, 0.0).astype(F32)
    h1 = lax.broadcasted_iota(jnp.int32, (1, 128), 1) < HEAD
    bias = [bias_scr[2 * pair + j] + prev_pen for j in range(2)]

    def residue(rows):
        q = q_ref[rows, :] * (HEAD ** -0.5)
        k = jnp.concatenate([kp_ref[rows, :], kc_ref[rows, :]], axis=0).astype(BF16)
        v = jnp.concatenate([vp_ref[rows, :], vc_ref[rows, :]], axis=0).astype(BF16)
        o_h, l_h = [], []
        for j in range(2):
            mine = h1 if j == 0 else jnp.logical_not(h1)
            qm = jnp.where(mine, q, 0.0).astype(BF16)
            s = _dg(qm, k, NT) + bias[j]
            m = jnp.max(s, axis=-1, keepdims=True)
            p = jnp.exp(s - m)
            den = jnp.sum(p, axis=-1, keepdims=True)
            o_h.append(_dg(p.astype(BF16), v, NN) / den)
            l_h.append(m + jnp.log(den))
        o_ref[rows, :] = jnp.where(h1, o_h[0], o_h[1])
        l_ref[rows, :] = jnp.where(h1, l_h[0], l_h[1])

    if dil == 1:
        residue(slice(None))
    else:
        def body(r, carry):
            residue(pl.ds(r, A_KEYS, stride=dil))
            return carry
        lax.fori_loop(0, dil, body, 0)


def _prompt_bucket_map(dil):
    qi = np.arange(A_KEYS)[:, None]
    ki = np.arange(2 * A_KEYS)[None, :]
    j = qi + A_KEYS - ki
    return np.where((j >= 0) & (j <= A_KEYS), _t5_bucket_np(np.clip(j, 0, A_KEYS) * dil), -1).astype(np.int32)


def _attn_prompt_group(qkv, rel_bias, group):
    bsz, t, _ = qkv.shape
    dil = A_GROUPS[group][1]
    blk = A_KEYS * dil

    def spec(part, prev):
        def idx(b, i, pair):
            return (b, jnp.maximum(i - 1, 0) if prev else i, (part * 3 + group) * 2 + pair)
        return pl.BlockSpec((None, blk, 128), idx)

    out_spec = pl.BlockSpec((None, blk, 128), lambda b, i, pair: (b, i, pair))
    return pl.pallas_call(
        functools.partial(_attn_prompt_kernel, group=group, dil=dil),
        grid=(bsz, t // blk, 2),
        in_specs=[pl.BlockSpec(memory_space=pltpu.SMEM), _const_spec((A_KEYS, 2 * A_KEYS)),
                  spec(0, False), spec(1, True), spec(1, False), spec(2, True), spec(2, False)],
        out_specs=[out_spec, out_spec],
        out_shape=[jax.ShapeDtypeStruct((bsz, t, A_OUT), F32)] * 2,
        scratch_shapes=[pltpu.VMEM((4, A_KEYS, 2 * A_KEYS), F32)],
        compiler_params=_params(3),
        name=f"attn_prompt_g{group}",
    )(rel_bias, jnp.asarray(_prompt_bucket_map(dil)), qkv, qkv, qkv, qkv, qkv)


def _merge_groups(outs, lses):
    m = functools.reduce(jnp.maximum, lses)
    ws = [jnp.exp(l - m) for l in lses]
    num = functools.reduce(lambda a, b: a + b, [w * o for w, o in zip(ws, outs)])
    return num / functools.reduce(lambda a, b: a + b, ws)


def _attn_sample_kernel(rb_ref, bc0, bc1, bc2, bn0, bn1, bn2, qkv_ref, c0_ref, c1_ref, c2_ref, out_ref,
                        bias_c0, bias_c1, bias_c2, bias_n):
    bias_c = (bias_c0, bias_c1, bias_c2)

    @pl.when(pl.program_id(0) == 0)
    def _():
        for g, (bc, bn) in enumerate(((bc0, bn0), (bc1, bn1), (bc2, bn2))):
            for h in range(4):
                bias_c[g][h] = _bias_from_buckets(bc[...], rb_ref, g * 4 + h)
                bias_n[g * 4 + h] = _bias_from_buckets(bn[...], rb_ref, g * 4 + h)

    qkv = qkv_ref[...]
    outs, lses = [], []
    for g, c_ref in enumerate((c0_ref, c1_ref, c2_ref)):
        cache = c_ref[...].astype(BF16)
        for h in range(4):
            col = (g * 4 + h) * HEAD
            q = (qkv[:, col:col + HEAD] * (HEAD ** -0.5)).astype(BF16)
            kn = qkv[:, 768 + col:768 + col + HEAD].astype(BF16)
            vn = qkv[:, 1536 + col:1536 + col + HEAD].astype(BF16)
            kc = cache[:, h * HEAD:(h + 1) * HEAD]
            vc = cache[:, A_OUT + h * HEAD:A_OUT + (h + 1) * HEAD]
            s_c = _dg(q, kc, NT) + bias_c[g][h]
            s_n = _dg(q, kn, NT) + bias_n[g * 4 + h]
            m = jnp.maximum(jnp.max(s_c, axis=-1, keepdims=True), jnp.max(s_n, axis=-1, keepdims=True))
            p_c = jnp.exp(s_c - m)
            p_n = jnp.exp(s_n - m)
            den = jnp.sum(p_c, axis=-1, keepdims=True) + jnp.sum(p_n, axis=-1, keepdims=True)
            o = _dg(p_c.astype(BF16), vc, NN) + _dg(p_n.astype(BF16), vn, NN)
            outs.append(o / den)
            lses.append(m + jnp.log(den))
    merged = [_merge_groups(outs[h::4], lses[h::4]) for h in range(4)]
    out_ref[...] = jnp.concatenate(merged, axis=-1)


def _sample_bucket_maps(window, dil, tp, t_real):
    t = np.arange(tp)[:, None]
    dist_c = window + t - np.arange(window)[None, :]
    ok_c = (dist_c % dil == 0) & (dist_c // dil <= A_KEYS) & (t < t_real)
    dist_n = t - np.arange(tp)[None, :]
    ok_n = (dist_n >= 0) & (dist_n % dil == 0) & (dist_n // dil <= A_KEYS)
    mc = np.where(ok_c, _t5_bucket_np(np.maximum(dist_c, 0)), -1).astype(np.int32)
    mn = np.where(ok_n, _t5_bucket_np(np.maximum(dist_n, 0)), -1).astype(np.int32)
    return mc, mn


def _attn_sample(qkv, caches, rel_bias, t_real):
    bsz, tp, _ = qkv.shape
    maps = [_sample_bucket_maps(w, d, tp, t_real) for w, d in A_GROUPS]
    mcs = [jnp.asarray(m[0]) for m in maps]
    mns = [jnp.asarray(m[1]) for m in maps]
    return pl.pallas_call(
        _attn_sample_kernel,
        grid=(bsz,),
        in_specs=[pl.BlockSpec(memory_space=pltpu.SMEM)]
        + [_const_spec(m.shape) for m in mcs] + [_const_spec(m.shape) for m in mns]
        + [pl.BlockSpec((None, tp, A_QKV), lambda b: (b, 0, 0))]
        + [pl.BlockSpec((None, w, 2 * A_OUT), lambda b: (b, 0, 0)) for w, _ in A_GROUPS],
        out_specs=pl.BlockSpec((None, tp, A_OUT), lambda b: (b, 0, 0)),
        out_shape=jax.ShapeDtypeStruct((bsz, tp, A_OUT), F32),
        scratch_shapes=[pltpu.VMEM((4, tp, w), F32) for w, _ in A_GROUPS] + [pltpu.VMEM((12, tp, tp), F32)],
        compiler_params=_params(1),
        name="attn_sample",
    )(rel_bias, *mcs, *mns, qkv, *caches)


def _head_sums(x, h1):
    total = jnp.sum(x, axis=-1, keepdims=True)
    first = jnp.sum(jnp.where(h1, x, 0.0), axis=-1, keepdims=True)
    return jnp.where(h1, first, total - first)


def _rwkv_kernel(rkv_ref, lora_ref, sh_rkv_ref, sh_lora_ref, s0_ref, mu_rkv_ref, mu_lora_ref, w0_ref, w2_ref,
                 a0_ref, a2_ref, g2_ref, kk_ref, ka_ref, rk_ref, lnw_ref, lnb_ref, tri_ref,
                 out_ref, s_out_ref, s_scr, prev_rkv, prev_lora, *, nb, chunk, t_real):
    c = pl.program_id(1)
    nc = pl.num_programs(1)
    seqs = range(nb)

    @pl.when(c == 0)
    def _():
        zero_blk = jnp.zeros((HEAD, HEAD), F32)
        for s in seqs:
            for p in range(B_PAIRS):
                s_scr[s, p] = jnp.concatenate([jnp.concatenate([s0_ref[s, 2 * p], zero_blk], axis=1),
                                               jnp.concatenate([zero_blk, s0_ref[s, 2 * p + 1]], axis=1)], axis=0)
        prev_rkv[...] = sh_rkv_ref[...]
        prev_lora[...] = sh_lora_ref[...]

    n_rows = nb * chunk
    rsl = [slice(s * chunk, (s + 1) * chunk) for s in seqs]
    rows = lax.broadcasted_iota(jnp.int32, (n_rows, 1), 0)
    pb = rkv_ref[...].reshape(n_rows, 3 * B_WIDTH)
    lr = lora_ref[...].reshape(n_rows, LORA_PAD)
    pb_prev = pltpu.roll(pb, 1, 0)
    lr_prev = pltpu.roll(lr, 1, 0)
    for s in seqs:
        at = rows == s * chunk
        pb_prev = jnp.where(at, prev_rkv[s], pb_prev)
        lr_prev = jnp.where(at, prev_lora[s], lr_prev)
        prev_rkv[s] = pb[(s + 1) * chunk - 1:(s + 1) * chunk, :]
        prev_lora[s] = lr[(s + 1) * chunk - 1:(s + 1) * chunk, :]
    xs = pb + (pb_prev - pb) * mu_rkv_ref[...]
    xl = lr + (lr_prev - lr) * mu_lora_ref[...]
    r = xs[:, :B_WIDTH]
    k = xs[:, B_WIDTH:2 * B_WIDTH]
    v = xs[:, 2 * B_WIDTH:]
    z = w0_ref[...] + _mm(jnp.tanh(xl[:, :128]), w2_ref[...], passes=3)
    lam = -DECAY_SCALE * _sigmoid(z)
    a = _sigmoid(a0_ref[...] + _mm(xl[:, 128:256], a2_ref[...]))
    gate = _mm(_sigmoid(xl[:, 256:]), g2_ref[...])
    kk = k * kk_ref[...]
    kmod = k * (1.0 + (a - 1.0) * ka_ref[...])
    if t_real < chunk:
        live = rows % chunk < t_real
        lam = jnp.where(live, lam, 0.0)
        kk = jnp.where(live, kk, 0.0)
        kmod = jnp.where(live, kmod, 0.0)
        v = jnp.where(live, v, 0.0)

    cum = _mm_exact_lhs(tri_ref[...], lam, terms=2)
    cum_end = [cum[(s + 1) * chunk - 1:(s + 1) * chunk, :] for s in seqs]
    e_in = jnp.exp(cum)
    e_prev = jnp.exp(cum - lam)
    e_neg = jnp.exp(-cum)
    e_end = [jnp.exp(cum_end[s] - cum[rsl[s]]) for s in seqs]
    g_end = [jnp.exp(cum_end[s]) for s in seqs]

    ri = lax.broadcasted_iota(jnp.int32, (chunk, 128), 0)
    ci = lax.broadcasted_iota(jnp.int32, (chunk, 128), 1) % HEAD
    strict = ci < ri
    lower = ci <= ri
    lane = lax.broadcasted_iota(jnp.int32, (1, 128), 1)
    h1 = lane < HEAD
    bi = lax.broadcasted_iota(jnp.int32, (128, 128), 0) // HEAD
    bj = lax.broadcasted_iota(jnp.int32, (128, 128), 1) // HEAD
    block_diag = bi == bj
    levels = int(math.log2(chunk))
    zero = jnp.zeros((chunk, 128), F32)
    units = [(s, p) for s in seqs for p in range(B_PAIRS)]
    idx = range(len(units))
    csl = [slice(p * 128, (p + 1) * 128) for _, p in units]
    take = lambda arr: [arr[rsl[s], csl[i]] for i, (s, _) in enumerate(units)]

    kappa = [x / jnp.maximum(jnp.sqrt(_head_sums(x * x, h1)), 1e-12) for x in take(kk)]
    a_u, r_u, k_u, v_u = take(a), take(r), take(kmod), take(v)
    e_neg_u = take(e_neg)
    bb = [kappa[i] * a_u[i] for i in idx]
    a_t = [-x * e for x, e in zip(kappa, take(e_prev))]
    r_t = [x * e for x, e in zip(r_u, take(e_in))]
    m4 = []
    for i in idx:
        l4 = jnp.concatenate([jnp.where(h1, a_t[i], zero), jnp.where(h1, zero, a_t[i]),
                              jnp.where(h1, r_t[i], zero), jnp.where(h1, zero, r_t[i])], axis=0)
        m4.append(_mm(l4, _stack_halves(bb[i] * e_neg_u[i], k_u[i] * e_neg_u[i]), NT))
    na = [[jnp.where(strict, m4[i][j * chunk:(j + 1) * chunk], 0.0) for j in range(2)] for i in idx]
    nr = [[jnp.where(lower, m4[i][(2 + j) * chunk:(3 + j) * chunk], 0.0) for j in range(2)] for i in idx]
    zv = [_stack_halves(zero, pltpu.roll(v_u[i], HEAD, 1)) for i in idx]
    zs = [[jnp.where(h1, a_t[i], _mm(na[i][0], zv[i])), jnp.where(h1, _mm(na[i][1], zv[i]), a_t[i])] for i in idx]
    ps = [[na[i][j][:, :chunk] for j in range(2)] for i in idx]
    for lvl in range(levels):
        last = lvl == levels - 1
        for i in idx:
            for j in range(2):
                rhs = zs[i][j] if last else jnp.concatenate([zs[i][j], ps[i][j]], axis=-1)
                upd = _mm(ps[i][j], rhs)
                zs[i][j] = zs[i][j] + upd[:, :128]
                if not last:
                    ps[i][j] = upd[:, 128:]
    ta = [jnp.where(h1, zs[i][0], zs[i][1]) for i in idx]
    pv = [pltpu.roll(jnp.where(h1, zs[i][1], zs[i][0]), HEAD, 1) for i in idx]
    s_old = [s_scr[s, p] for s, p in units]
    u = [_mm(ta[i], s_old[i], NT) + pv[i] for i in idx]
    uv = [_stack_halves(u[i], v_u[i]) for i in idx]
    y = [_mm(r_t[i], s_old[i], NT) + jnp.where(h1, _mm(nr[i][0], uv[i]), _mm(nr[i][1], uv[i])) for i in idx]
    for i, (s, p) in enumerate(units):
        e = e_end[s][:, csl[i]]
        s_new = s_old[i] * g_end[s][:, csl[i]] + _mm(uv[i].T, _stack_halves(bb[i] * e, k_u[i] * e))
        s_scr[s, p] = jnp.where(block_diag, s_new, 0.0)

    mean = [_head_sums(y[i], h1) * (1.0 / HEAD) for i in idx]
    dlt = [y[i] - mean[i] for i in idx]
    var = [_head_sums(d * d, h1) * (1.0 / HEAD) for d in dlt]
    gate_u = take(gate)
    for i, (s, p) in enumerate(units):
        sl = csl[i]
        bonus = _head_sums(r_u[i] * k_u[i] * rk_ref[:, sl], h1)
        yn = dlt[i] * lax.rsqrt(var[i] + LNX_EPS) * lnw_ref[:, sl] + lnb_ref[:, sl]
        out_ref[s, :, sl] = (yn + bonus * v_u[i]) * gate_u[i]

    @pl.when(c == nc - 1)
    def _():
        for s in seqs:
            for p in range(B_PAIRS):
                s_pair = s_scr[s, p]
                s_out_ref[s, 2 * p] = s_pair[:HEAD, :HEAD]
                s_out_ref[s, 2 * p + 1] = s_pair[HEAD:, HEAD:]


def _rwkv(rkv, lora, sh_rkv, sh_lora, s0, prm, chunk, t_real):
    bsz, t, _ = rkv.shape
    nc = t // chunk
    nb = min(bsz, 128 // chunk, MAX_SEQS_PER_STEP)
    tri = np.kron(np.eye(nb), np.tril(np.ones((chunk, chunk)))).astype(np.float32)
    vec = lambda n: _const_spec((1, n))
    row = lambda x: x.reshape(1, -1)
    out, s_out = pl.pallas_call(
        functools.partial(_rwkv_kernel, nb=nb, chunk=chunk, t_real=t_real),
        grid=(bsz // nb, nc),
        in_specs=[pl.BlockSpec((nb, chunk, 3 * B_WIDTH), lambda b, c: (b, c, 0)),
                  pl.BlockSpec((nb, chunk, LORA_PAD), lambda b, c: (b, c, 0)),
                  pl.BlockSpec((nb, 1, 3 * B_WIDTH), lambda b, c: (b, 0, 0)),
                  pl.BlockSpec((nb, 1, LORA_PAD), lambda b, c: (b, 0, 0)),
                  pl.BlockSpec((nb, 2 * B_PAIRS, HEAD, HEAD), lambda b, c: (b, 0, 0, 0)),
                  vec(3 * B_WIDTH), vec(LORA_PAD), vec(B_WIDTH), _const_spec((128, B_WIDTH)),
                  vec(B_WIDTH), _const_spec((128, B_WIDTH)), _const_spec((256, B_WIDTH)),
                  vec(B_WIDTH), vec(B_WIDTH), vec(B_WIDTH), vec(B_WIDTH), vec(B_WIDTH),
                  _const_spec(tri.shape)],
        out_specs=[pl.BlockSpec((nb, chunk, B_WIDTH), lambda b, c: (b, c, 0)),
                   pl.BlockSpec((nb, 2 * B_PAIRS, HEAD, HEAD), lambda b, c: (b, 0, 0, 0))],
        out_shape=[jax.ShapeDtypeStruct((bsz, t, B_WIDTH), F32),
                   jax.ShapeDtypeStruct((bsz, 2 * B_PAIRS, HEAD, HEAD), F32)],
        scratch_shapes=[pltpu.VMEM((nb, B_PAIRS, 128, 128), F32), pltpu.VMEM((nb, 1, 3 * B_WIDTH), F32),
                        pltpu.VMEM((nb, 1, LORA_PAD), F32)],
        compiler_params=_params(2),
        name="rwkv7",
    )(rkv, lora, sh_rkv[:, None], sh_lora[:, None], s0,
      row(prm["mu_rkv"]), row(prm["mu_lora"]), row(prm["w0"]), prm["w2"], row(prm["a0"]), prm["a2"], prm["g2"],
      row(prm["k_k"]), row(prm["k_a"]), row(prm["r_k"]), row(prm["lnx_w"]), row(prm["lnx_b"]),
      jnp.asarray(tri, BF16))
    return out, s_out


def _gla_kernel(pc_ref, s0_ref, lb_ref, gn_ref, tri_ref, sel_ref, out_ref, s_out_ref, s_scr, *, nb, chunk, t_real):
    c = pl.program_id(1)
    nc = pl.num_programs(1)
    seqs = range(nb)

    @pl.when(c == 0)
    def _():
        for s in seqs:
            for h in range(C_HEADS):
                s_scr[s, h] = s0_ref[s, h].T

    n_rows = nb * chunk
    rsl = [slice(s * chunk, (s + 1) * chunk) for s in seqs]
    pc = pc_ref[...].reshape(n_rows, 4 * D_MODEL)
    lb = lb_ref[...]
    xq = pc[:, :C_FDIM]
    q = xq * _sigmoid(xq)
    fg = lb + (1.0 - lb) * _sigmoid(pc[:, C_FDIM:2 * C_FDIM])
    k = 1.0 - fg
    logf = jnp.log(fg)
    v = pc[:, 2 * C_FDIM:2 * C_FDIM + D_MODEL]
    xg = pc[:, 2 * C_FDIM + D_MODEL:]
    if t_real < chunk:
        live = lax.broadcasted_iota(jnp.int32, (n_rows, 1), 0) % chunk < t_real
        logf = jnp.where(live, logf, 0.0)
        k = jnp.where(live, k, 0.0)
        v = jnp.where(live, v, 0.0)

    cum = _mm_exact_lhs(tri_ref[...], logf)
    base = _mm_exact_lhs(sel_ref[...], logf)
    q_in = q * jnp.exp(cum)
    q_loc = q * jnp.exp(cum - base)
    k_loc = k * jnp.exp(jnp.minimum(base - cum, EXP_CLAMP))
    nsub = chunk // GLA_SUB
    k_stack, v_stack, k_end, g_end = [], [], [], []
    for s in seqs:
        cum_s, k_s = cum[rsl[s]], k[rsl[s]]
        cum_end = cum_s[chunk - 1:chunk, :]
        k_end.append(k_s * jnp.exp(cum_end - cum_s))
        g_end.append(jnp.exp(cum_end))
        k_var = [k_loc[rsl[s]]]
        for i in range(1, nsub):
            ref_i = cum_s[i * GLA_SUB - 1:i * GLA_SUB, :]
            k_var.append(k_s * jnp.exp(jnp.minimum(ref_i - cum_s, 0.0)))
        k_stack.append(jnp.concatenate(k_var, axis=0) if nsub > 1 else k_var[0])
        v_stack.append(jnp.concatenate([v[rsl[s]]] * nsub, axis=0) if nsub > 1 else v[rsl[s]])

    ri = lax.broadcasted_iota(jnp.int32, (chunk, nsub * chunk), 0)
    cc = lax.broadcasted_iota(jnp.int32, (chunk, nsub * chunk), 1)
    var = cc // chunk
    ci = cc % chunk
    same_sub = ci // GLA_SUB == ri // GLA_SUB
    att_mask = ((var == 0) & same_sub & (ci <= ri)) | ((ri // GLA_SUB == var) & (ci < var * GLA_SUB))

    units = [(s, h) for s in seqs for h in range(C_HEADS)]
    idx = range(len(units))
    csl = [slice(h * 128, (h + 1) * 128) for _, h in units]
    att = [jnp.where(att_mask, _mm(q_loc[rsl[s], csl[i]], k_stack[s][:, csl[i]], NT), 0.0)
           for i, (s, _) in enumerate(units)]
    s_old = [s_scr[s, h] for s, h in units]
    outs = [_mm(q_in[rsl[s], csl[i]], s_old[i], NT) + _mm(att[i], v_stack[s][:, csl[i]])
            for i, (s, _) in enumerate(units)]
    for i, (s, h) in enumerate(units):
        sl = csl[i]
        s_scr[s, h] = s_old[i] * g_end[s][:, sl] + _mm(_stack_rows_128([v[rsl[s], sl]]).T,
                                                       _stack_rows_128([k_end[s][:, sl]]))
    o = jnp.concatenate([jnp.concatenate(outs[s * C_HEADS:(s + 1) * C_HEADS], axis=-1) for s in seqs], axis=0)
    out_ref[...] = (_rms(o, gn_ref[...]) * (xg * _sigmoid(xg))).reshape(nb, chunk, D_MODEL)

    @pl.when(c == nc - 1)
    def _():
        for s in seqs:
            for h in range(C_HEADS):
                s_out_ref[s, h] = s_scr[s, h].T


def _gla(pc, s0, lb, gn, chunk, t_real):
    bsz, t, _ = pc.shape
    nc = t // chunk
    nb = min(bsz, 128 // chunk, MAX_SEQS_PER_STEP)
    idx = np.arange(chunk)
    eye = np.eye(nb)
    tri = np.kron(eye, np.tril(np.ones((chunk, chunk)))).astype(np.float32)
    sel = np.kron(eye, idx[None, :] < (idx[:, None] // GLA_SUB) * GLA_SUB).astype(np.float32)
    return pl.pallas_call(
        functools.partial(_gla_kernel, nb=nb, chunk=chunk, t_real=t_real),
        grid=(bsz // nb, nc),
        in_specs=[pl.BlockSpec((nb, chunk, 4 * D_MODEL), lambda b, c: (b, c, 0)),
                  pl.BlockSpec((nb, C_HEADS, 128, 128), lambda b, c: (b, 0, 0, 0)),
                  _const_spec((1, C_FDIM)), _const_spec((1, D_MODEL)),
                  _const_spec(tri.shape), _const_spec(sel.shape)],
        out_specs=[pl.BlockSpec((nb, chunk, D_MODEL), lambda b, c: (b, c, 0)),
                   pl.BlockSpec((nb, C_HEADS, 128, 128), lambda b, c: (b, 0, 0, 0))],
        out_shape=[jax.ShapeDtypeStruct((bsz, t, D_MODEL), F32),
                   jax.ShapeDtypeStruct((bsz, C_HEADS, 128, 128), F32)],
        scratch_shapes=[pltpu.VMEM((nb, C_HEADS, 128, 128), F32)],
        compiler_params=_params(2),
        name="hgrn2",
    )(pc, s0, lb.reshape(1, -1), gn.reshape(1, -1), jnp.asarray(tri, BF16), jnp.asarray(sel, BF16))


def _post_kernel(*refs, n_groups, n_mix, final):
    it = iter(refs[:-1])
    take = lambda n: [next(it) for _ in range(n)]
    (x_ref,) = take(1)
    o_refs, l_refs, mix_refs = take(n_groups), take(n_groups), take(n_mix)
    w_refs = take(n_mix + (1 if n_groups else 0))
    gm_ref, up_ref, down_ref = take(3)
    gf_ref = next(it) if final else None
    o_ref = refs[-1]
    mixes = [m[...] for m in mix_refs]
    if n_groups:
        mixes.insert(0, _merge_groups([o[...] for o in o_refs], [l[...] for l in l_refs]))
    x = x_ref[...]
    for mix, w_ref in zip(mixes, w_refs):
        x = x + jnp.dot(mix.astype(BF16), w_ref[...], preferred_element_type=F32)
    h = _rms(x, gm_ref[...]).astype(BF16)
    u = jnp.dot(h, up_ref[...], preferred_element_type=F32)
    u = jnp.square(jnp.maximum(u, 0.0)).astype(BF16)
    x = x + jnp.dot(u, down_ref[...], preferred_element_type=F32)
    if final:
        x = _rms(x, gf_ref[...])
    o_ref[...] = x


def _post(x, groups, mixes, ws, gm, up, down, gf=None, tm=256):
    m = x.shape[0]
    final = gf is not None
    row_spec = lambda n: pl.BlockSpec((tm, n), lambda i: (i, 0))
    rows = [x] + [o for o, _ in groups] + [l for _, l in groups] + list(mixes)
    args = rows + [*ws, gm.reshape(1, -1), up, down]
    specs = ([row_spec(a.shape[1]) for a in rows] + [_const_spec(w.shape) for w in ws]
             + [_const_spec((1, D_MODEL)), _const_spec(up.shape), _const_spec(down.shape)])
    if final:
        args.append(gf.reshape(1, -1))
        specs.append(_const_spec((1, D_MODEL)))
    return pl.pallas_call(
        functools.partial(_post_kernel, n_groups=len(groups), n_mix=len(mixes), final=final),
        grid=(m // tm,), in_specs=specs, out_specs=row_spec(D_MODEL),
        out_shape=jax.ShapeDtypeStruct((m, D_MODEL), F32), compiler_params=_params(1), name="post_mlp",
    )(*args)


def _lora_pad_cols(x):
    pad = lambda a, n: jnp.pad(a, [(0, 0)] * (a.ndim - 1) + [(0, n - a.shape[-1])])
    return jnp.concatenate([pad(x[..., :64], 128), pad(x[..., 64:128], 128), pad(x[..., 128:], 256)], axis=-1)


def _lora_unpad_cols(x):
    return jnp.concatenate([x[..., :64], x[..., 128:192], x[..., 256:416]], axis=-1)


def _prepare(rel_bias, norm_mix, norm_mlp, norm_final, e_w_in, e_mu, e_w0, e_w2, e_a0, e_a2, e_g2, e_k_k, e_k_a,
             e_r_k, e_lnx_w, e_lnx_b, e_w_out, c_w_in, c_lb_raw, c_norm, c_w_out, mlp_up, mlp_down):
    pad_rows = lambda a, n: jnp.pad(a, [(0, n - a.shape[0]), (0, 0)])
    w_in = e_w_in[0]
    sm = jax.nn.softmax(c_lb_raw.astype(F32), axis=0)
    lb = (jnp.cumsum(sm, axis=0) - sm[0])[1]
    return dict(
        rel_bias=rel_bias,
        w_qkv=w_in[:, :A_QKV].astype(BF16),
        w_rkv=w_in[:, A_QKV:A_QKV + 3 * B_WIDTH].astype(BF16),
        w_lora=_lora_pad_cols(w_in[:, A_QKV + 3 * B_WIDTH:]).astype(BF16),
        rwkv=dict(mu_rkv=e_mu[0, :3 * B_WIDTH], mu_lora=_lora_pad_cols(e_mu[0, 3 * B_WIDTH:]),
                  w0=e_w0[0], w2=pad_rows(e_w2[0], 128), a0=e_a0[0], a2=pad_rows(e_a2[0], 128),
                  g2=pad_rows(e_g2[0], 256), k_k=e_k_k[0], k_a=e_k_a[0], r_k=e_r_k[0].reshape(-1),
                  lnx_w=e_lnx_w[0], lnx_b=e_lnx_b[0]),
        w_out_a=e_w_out[0, :A_OUT].astype(BF16), w_out_b=e_w_out[0, A_OUT:].astype(BF16),
        c_w_in=c_w_in[0].astype(BF16), c_w_out=c_w_out[0].astype(BF16), lb=lb, c_norm=c_norm[0],
        norm_mix=norm_mix, norm_mlp=norm_mlp, norm_final=norm_final,
        up=mlp_up.astype(BF16), down=mlp_down.astype(BF16))


def _trunk(x, prm, attend, sh_rkv, sh_lora, wkv0, c0, chunk, t_real):
    bsz, t, _ = x.shape
    flat = lambda a: a.reshape(bsz * t, a.shape[-1])
    xf = flat(x)
    qkv, rkv, lora = _norm_proj(xf, prm["norm_mix"][0], [prm["w_qkv"], prm["w_rkv"], prm["w_lora"]])
    qkv = qkv.reshape(bsz, t, A_QKV)
    rkv = rkv.reshape(bsz, t, 3 * B_WIDTH)
    lora = lora.reshape(bsz, t, LORA_PAD)
    groups, mixes = attend(qkv)
    b_out, wkv = _rwkv(rkv, lora, sh_rkv, sh_lora, wkv0, prm["rwkv"], chunk, t_real)
    x1 = _post(xf, [(flat(o), flat(l)) for o, l in groups], [flat(a) for a in mixes] + [flat(b_out)],
               [prm["w_out_a"], prm["w_out_b"]], prm["norm_mlp"][0], prm["up"][0], prm["down"][0])
    (pc,) = _norm_proj(x1, prm["norm_mix"][1], [prm["c_w_in"]])
    c_out, c_state = _gla(pc.reshape(bsz, t, 4 * D_MODEL), c0, prm["lb"], prm["c_norm"], chunk, t_real)
    y = _post(x1, [], [flat(c_out)], [prm["c_w_out"]], prm["norm_mlp"][1], prm["up"][1], prm["down"][1],
              gf=prm["norm_final"])
    kv_rows = []
    for g, (w, _) in enumerate(A_GROUPS):
        n = min(w, t_real)
        rows = qkv[:, t_real - n:t_real]
        kv = jnp.stack([rows[..., (3 * part + g) * A_OUT:(3 * part + g + 1) * A_OUT] for part in (1, 2)], axis=2)
        kv_rows.append(kv.reshape(1, bsz, n, 2, 4, HEAD))
    shift = jnp.concatenate([rkv[:, t_real - 1], _lora_unpad_cols(lora[:, t_real - 1])], axis=-1)[None]
    return y.reshape(bsz, t, D_MODEL)[:, :t_real], kv_rows, shift, wkv[None], c_state[None]


def kernel(x_prompt, x_sample, cache_a0, cache_a1, cache_a2, state_b_shift, state_b_wkv, state_c, rel_bias, norm_mix, norm_mlp, norm_final, e_w_in, e_mu, e_w0, e_w2, e_a0, e_a2, e_g2, e_k_k, e_k_a, e_r_k, e_lnx_w, e_lnx_b, e_w_out, c_w_in, c_lb_raw, c_norm, c_w_out, mlp_up, mlp_down):
    prm = _prepare(rel_bias, norm_mix, norm_mlp, norm_final, e_w_in, e_mu, e_w0, e_w2, e_a0, e_a2, e_g2, e_k_k,
                   e_k_a, e_r_k, e_lnx_w, e_lnx_b, e_w_out, c_w_in, c_lb_raw, c_norm, c_w_out, mlp_up, mlp_down)

    bp, tp, _ = x_prompt.shape

    def attend_prompt(qkv):
        return [_attn_prompt_group(qkv, prm["rel_bias"], g) for g in range(len(A_GROUPS))], []

    y_p, p_kv, p_shift, p_wkv, p_c = _trunk(
        x_prompt, prm, attend_prompt,
        jnp.zeros((bp, 3 * B_WIDTH), F32), jnp.zeros((bp, LORA_PAD), F32),
        jnp.zeros((bp, 12, HEAD, HEAD), F32), jnp.zeros((bp, C_HEADS, 128, 128), F32), chunk=64, t_real=tp)

    bs, ts, _ = x_sample.shape
    t_pad = 16
    x_s = jnp.pad(x_sample, ((0, 0), (0, t_pad - ts), (0, 0)))
    caches = [c[0].reshape(bs, c.shape[2], 2 * A_OUT) for c in (cache_a0, cache_a1, cache_a2)]

    def attend_sample(qkv):
        return [], [_attn_sample(qkv, caches, prm["rel_bias"], ts)]

    sh = state_b_shift[0]
    y_s, s_kv, s_shift, s_wkv, s_c = _trunk(
        x_s, prm, attend_sample, sh[:, :3 * B_WIDTH], _lora_pad_cols(sh[:, 3 * B_WIDTH:]),
        state_b_wkv[0], state_c[0], chunk=t_pad, t_real=ts)

    return (y_p, y_s, p_kv[0], p_kv[1], p_kv[2], p_shift, p_wkv, p_c,
            s_kv[0], s_kv[1], s_kv[2], s_shift, s_wkv, s_c)
```

```python
import functools
import math

import jax
import jax.numpy as jnp
import numpy as np
from jax import lax
from jax.experimental import pallas as pl
from jax.experimental.pallas import tpu as pltpu

F32 = jnp.float32
BF16 = jnp.bfloat16

D_MODEL = 1024
D_FF = 4 * D_MODEL
EPS = 1e-6
LNX_EPS = 64e-5
HEAD = 64
A_GROUPS = ((128, 1), (512, 4), (2048, 16))
A_KEYS = 128
A_TILES = 4
A_QKV = 2304
A_OUT = 256
N_BUCKETS = 32
BUCKET_MAX_DIST = 2048
B_WIDTH = 768
B_PAIRS = B_WIDTH // 128
LORA_PAD = 512
C_FDIM = 1024
C_HEADS = 8
NEG = -1e30
DECAY_SCALE = math.exp(-0.5)
GLA_SUB = 16
EXP_CLAMP = 80.0
MAX_SEQS_PER_STEP = 4

VMEM_LIMIT = 56 * 1024 * 1024

NN = (((1,), (0,)), ((), ()))
NT = (((1,), (1,)), ((), ()))


def _dg(a, b, dims):
    return lax.dot_general(a, b, dims, preferred_element_type=F32)


def _split2(x):
    hi = x.astype(BF16)
    lo = (x - hi.astype(F32)).astype(BF16)
    return hi, lo


def _split3(x):
    hi = x.astype(BF16)
    r1 = x - hi.astype(F32)
    mid = r1.astype(BF16)
    lo = (r1 - mid.astype(F32)).astype(BF16)
    return hi, mid, lo


def _mm(a, b, dims=NN, passes=1):
    if passes == 1:
        return _dg(a.astype(BF16), b.astype(BF16), dims)
    ah, al = _split2(a)
    bh, bl = _split2(b)
    return _dg(ah, bh, dims) + (_dg(ah, bl, dims) + _dg(al, bh, dims))


def _mm_exact_lhs(a_bf16, b, terms=3):
    if terms == 2:
        bh, bl = _split2(b)
        return _dg(a_bf16, bh, NN) + _dg(a_bf16, bl, NN)
    bh, bm, bl = _split3(b)
    return _dg(a_bf16, bh, NN) + (_dg(a_bf16, bm, NN) + _dg(a_bf16, bl, NN))


def _stack_rows_128(parts):
    n = sum(p.shape[0] for p in parts)
    if n < 128:
        parts = list(parts) + [jnp.zeros((128 - n, 128), F32)]
    return jnp.concatenate(parts, axis=0) if len(parts) > 1 else parts[0]


def _stack_halves(top, bottom):
    n = top.shape[0]
    if n == 64:
        return jnp.concatenate([top, bottom], axis=0)
    pad = jnp.zeros((64 - n, 128), F32)
    return jnp.concatenate([top, pad, bottom, pad], axis=0)


def _sigmoid(x):
    return 1.0 / (1.0 + jnp.exp(-x))


def _rms(x, g):
    ms = jnp.mean(x * x, axis=-1, keepdims=True)
    return x * lax.rsqrt(ms + EPS) * g


def _const_spec(shape):
    nd = len(shape)
    return pl.BlockSpec(shape, lambda *_: (0,) * nd)


def _params(n_grid):
    return pltpu.CompilerParams(dimension_semantics=("arbitrary",) * n_grid,
                                vmem_limit_bytes=VMEM_LIMIT)


def _norm_proj_kernel(x_ref, g_ref, *refs):
    n = len(refs) // 2
    h = _rms(x_ref[...], g_ref[...]).astype(BF16)
    for w_ref, o_ref in zip(refs[:n], refs[n:]):
        o_ref[...] = jnp.dot(h, w_ref[...], preferred_element_type=F32)


def _norm_proj(x, g, ws, tm=256):
    m = x.shape[0]
    return pl.pallas_call(
        _norm_proj_kernel,
        grid=(m // tm,),
        in_specs=[pl.BlockSpec((tm, D_MODEL), lambda i: (i, 0)), _const_spec((1, D_MODEL))]
        + [_const_spec(w.shape) for w in ws],
        out_specs=[pl.BlockSpec((tm, w.shape[1]), lambda i: (i, 0)) for w in ws],
        out_shape=[jax.ShapeDtypeStruct((m, w.shape[1]), F32) for w in ws],
        compiler_params=_params(1),
        name="norm_proj",
    )(x, g.reshape(1, D_MODEL), *ws)


def _t5_bucket_np(dist):
    max_exact = N_BUCKETS // 2
    d = np.maximum(dist, 1).astype(np.float32)
    large = max_exact + (np.log(d / np.float32(max_exact)) / np.float32(math.log(BUCKET_MAX_DIST / max_exact))
                         * np.float32(N_BUCKETS - max_exact)).astype(np.int32)
    large = np.minimum(large, N_BUCKETS - 1)
    return np.where(dist < max_exact, dist, large).astype(np.int32)


def _bias_from_buckets(idx, rb_ref, head):
    acc = jnp.full(idx.shape, NEG, F32)
    for b in range(N_BUCKETS):
        acc = jnp.where(idx == b, rb_ref[b, head], acc)
    return acc


def _attn_prompt_kernel(rb_ref, bkt_ref, q_ref, kp_ref, kc_ref, vp_ref, vc_ref, o_ref, l_ref, bias_scr, *, group, dil):
    pair = pl.program_id(2)
    first = (pl.program_id(0) == 0) & (pl.program_id(1) == 0) & (pair == 0)

    @pl.when(first)
    def _():
        idx = bkt_ref[...]
        for h in range(4):
            bias_scr[h] = _bias_from_buckets(idx, rb_ref, group * 4 + h)

    col = lax.broadcasted_iota(jnp.int32, (1, 2 * A_KEYS), 1)
    prev_pen = jnp.where((col < A_KEYS) & (pl.program_id(1) == 0), NEG, 0.0).astype(F32)
    h1 = lax.broadcasted_iota(jnp.int32, (1, 128), 1) < HEAD
    h2 = jnp.logical_not(h1)
    bias_in = [bias_scr[2 * pair + j] for j in range(2)]
    bias_edge = [b + prev_pen for b in bias_in]

    def attend(tiles):
        n = range(len(tiles))
        q = [q_ref[rows, :] * (HEAD ** -0.5) for rows, _, _ in tiles]
        k = [jnp.concatenate([pref[0][prows, :], kc_ref[rows, :]], axis=0).astype(BF16)
             for rows, pref, prows in tiles]
        v = [jnp.concatenate([pref[1][prows, :], vc_ref[rows, :]], axis=0).astype(BF16)
             for rows, pref, prows in tiles]
        bias = [bias_edge if tiles[i][1][0] is kp_ref else bias_in for i in n]
        s = [[_dg(jnp.where(mine, q[i], 0.0).astype(BF16), k[i], NT) + bias[i][j]
              for j, mine in enumerate((h1, h2))] for i in n]
        m = [[jnp.max(x, axis=-1, keepdims=True) for x in s[i]] for i in n]
        p = [[jnp.exp(x - mx) for x, mx in zip(s[i], m[i])] for i in n]
        den = [[jnp.sum(x, axis=-1, keepdims=True) for x in p[i]] for i in n]
        o = [[_dg(x.astype(BF16), v[i], NN) for x in p[i]] for i in n]
        for i in n:
            rows = tiles[i][0]
            inv = jnp.where(h1, 1.0 / den[i][0], 1.0 / den[i][1])
            o_ref[rows, :] = jnp.where(h1, o[i][0], o[i][1]) * inv
            l_ref[rows, :] = jnp.where(h1, m[i][0] + jnp.log(den[i][0]), m[i][1] + jnp.log(den[i][1]))

    prev_refs = (kp_ref, vp_ref)
    cur_refs = (kc_ref, vc_ref)
    if dil == 1:
        tile = lambda j: slice(j * A_KEYS, (j + 1) * A_KEYS)
        attend([(tile(j), prev_refs if j == 0 else cur_refs, slice(None) if j == 0 else tile(j - 1))
                for j in range(A_TILES)])
    elif dil == A_TILES:
        attend([(pl.ds(r, A_KEYS, stride=dil), prev_refs, pl.ds(r, A_KEYS, stride=dil)) for r in range(dil)])
    else:
        def body(it, carry):
            rows = [pl.ds(it * A_TILES + u, A_KEYS, stride=dil) for u in range(A_TILES)]
            attend([(r, prev_refs, r) for r in rows])
            return carry
        lax.fori_loop(0, dil // A_TILES, body, 0)


def _prompt_bucket_map(dil):
    qi = np.arange(A_KEYS)[:, None]
    ki = np.arange(2 * A_KEYS)[None, :]
    j = qi + A_KEYS - ki
    return np.where((j >= 0) & (j <= A_KEYS), _t5_bucket_np(np.clip(j, 0, A_KEYS) * dil), -1).astype(np.int32)


def _attn_prompt_group(qkv, rel_bias, group):
    bsz, t, _ = qkv.shape
    dil = A_GROUPS[group][1]
    blk = A_KEYS * max(dil, A_TILES)
    prev_blk = A_KEYS if dil == 1 else blk
    per = blk // prev_blk

    def spec(part, prev):
        def idx(b, i, pair):
            return (b, jnp.maximum(i * per - 1, 0) if prev else i, (part * 3 + group) * 2 + pair)
        return pl.BlockSpec((None, prev_blk if prev else blk, 128), idx)

    out_spec = pl.BlockSpec((None, blk, 128), lambda b, i, pair: (b, i, pair))
    return pl.pallas_call(
        functools.partial(_attn_prompt_kernel, group=group, dil=dil),
        grid=(bsz, t // blk, 2),
        in_specs=[pl.BlockSpec(memory_space=pltpu.SMEM), _const_spec((A_KEYS, 2 * A_KEYS)),
                  spec(0, False), spec(1, True), spec(1, False), spec(2, True), spec(2, False)],
        out_specs=[out_spec, out_spec],
        out_shape=[jax.ShapeDtypeStruct((bsz, t, A_OUT), F32)] * 2,
        scratch_shapes=[pltpu.VMEM((4, A_KEYS, 2 * A_KEYS), F32)],
        compiler_params=_params(3),
        name=f"attn_prompt_g{group}",
    )(rel_bias, jnp.asarray(_prompt_bucket_map(dil)), qkv, qkv, qkv, qkv, qkv)


def _merge_groups(outs, lses):
    m = functools.reduce(jnp.maximum, lses)
    ws = [jnp.exp(l - m) for l in lses]
    num = functools.reduce(lambda a, b: a + b, [w * o for w, o in zip(ws, outs)])
    return num / functools.reduce(lambda a, b: a + b, ws)


def _sample_table(group, t, t_real):
    return t if group == 0 else t_real - 1 + group


def _attn_sample_kernel(rb_ref, bkt_ref, new_ref, c0_ref, c1_ref, c2_ref, out_ref, bias_scr, *, t_real):
    hh = lax.broadcasted_iota(jnp.int32, (4, 1), 0)

    def head_vec(bucket, group):
        vals = [rb_ref[bucket, group * 4 + h] for h in range(4)]
        return jnp.where(hh == 0, vals[0], jnp.where(hh == 1, vals[1], jnp.where(hh == 2, vals[2], vals[3])))

    @pl.when(pl.program_id(0) == 0)
    def _():
        for group in range(len(A_GROUPS)):
            for t in range(t_real if group == 0 else 1):
                slot = _sample_table(group, t, t_real)
                idx = bkt_ref[slot]
                acc = jnp.full(idx.shape, NEG, F32)
                for b in range(N_BUCKETS):
                    acc = jnp.where(idx == b, head_vec(b, group)[None], acc)
                bias_scr[slot] = acc

    scale = HEAD ** -0.5
    for t in range(t_real):
        o_g, l_g = [], []
        for group, (_, dil) in enumerate(A_GROUPS):
            q = new_ref[t, 0, group] * scale
            if group == 0:
                kc, vc = c0_ref[:, 0], c0_ref[:, 1]
            else:
                c_ref = (c1_ref, c2_ref)[group - 1]
                kc, vc = c_ref[:, t, 0], c_ref[:, t, 1]
            s = jnp.sum(kc * q[None], axis=-1, keepdims=True) + bias_scr[_sample_table(group, t, t_real)]
            news = range(t + 1) if dil == 1 else (t,)
            s_n = [jnp.sum(new_ref[t2, 1, group] * q, axis=-1, keepdims=True)
                   + head_vec(int(_t5_bucket_np(np.array(t - t2))), group) for t2 in news]
            m = functools.reduce(jnp.maximum, s_n, jnp.max(s, axis=0))
            p = jnp.exp(s - m[None])
            p_n = [jnp.exp(x - m) for x in s_n]
            den = functools.reduce(lambda a, b: a + b, p_n, jnp.sum(p, axis=0))
            o = functools.reduce(lambda a, b: a + b, [w * new_ref[t2, 2, group] for w, t2 in zip(p_n, news)],
                                 jnp.sum(p * vc, axis=0))
            o_g.append(o / den)
            l_g.append(m + jnp.log(den))
        out_ref[t] = _merge_groups(o_g, l_g)


def _sample_bucket_maps(t_real):
    n = np.arange(A_KEYS)
    maps = np.zeros((t_real + len(A_GROUPS) - 1, A_KEYS), np.int32)
    for t in range(t_real):
        dist = A_KEYS + t - n
        maps[_sample_table(0, t, t_real)] = np.where(dist <= A_KEYS, _t5_bucket_np(dist), -1)
    for group in range(1, len(A_GROUPS)):
        maps[_sample_table(group, 0, t_real)] = _t5_bucket_np((A_KEYS - n) * A_GROUPS[group][1])
    return np.broadcast_to(maps[:, :, None, None], maps.shape + (4, 1)).copy()


def _attn_sample(new_rows, caches, rel_bias):
    bsz, t_real = new_rows.shape[:2]
    assert t_real <= A_GROUPS[1][1] and all(c.shape[1] == A_KEYS * d for c, (_, d) in zip(caches, A_GROUPS))
    views = [caches[0]] + [c.reshape(bsz, A_KEYS, d, 2, 4, HEAD) for c, (_, d) in zip(caches[1:], A_GROUPS[1:])]
    maps = jnp.asarray(_sample_bucket_maps(t_real))
    tail = (2, 4, HEAD)
    return pl.pallas_call(
        functools.partial(_attn_sample_kernel, t_real=t_real),
        grid=(bsz,),
        in_specs=[pl.BlockSpec(memory_space=pltpu.SMEM), _const_spec(maps.shape),
                  pl.BlockSpec((None,) + new_rows.shape[1:], lambda b: (b, 0, 0, 0, 0, 0)),
                  pl.BlockSpec((None, A_KEYS) + tail, lambda b: (b, 0, 0, 0, 0)),
                  pl.BlockSpec((None, A_KEYS, A_GROUPS[1][1]) + tail, lambda b: (b, 0, 0, 0, 0, 0)),
                  pl.BlockSpec((None, A_KEYS, t_real) + tail, lambda b: (b, 0, 0, 0, 0, 0))],
        out_specs=pl.BlockSpec((None, t_real, 4, HEAD), lambda b: (b, 0, 0, 0)),
        out_shape=jax.ShapeDtypeStruct((bsz, t_real, 4, HEAD), F32),
        scratch_shapes=[pltpu.VMEM(maps.shape, F32)],
        compiler_params=_params(1),
        name="attn_sample",
    )(rel_bias, maps, new_rows, *views)


def _head_sums(x, h1):
    total = jnp.sum(x, axis=-1, keepdims=True)
    first = jnp.sum(jnp.where(h1, x, 0.0), axis=-1, keepdims=True)
    return jnp.where(h1, first, total - first)


def _rwkv_kernel(rkv_ref, lora_ref, sh_rkv_ref, sh_lora_ref, s0_ref, mu_rkv_ref, mu_lora_ref, w0_ref, w2_ref,
                 a0_ref, a2_ref, g2_ref, kk_ref, ka_ref, rk_ref, lnw_ref, lnb_ref, tri_ref,
                 out_ref, s_out_ref, s_scr, prev_rkv, prev_lora, *, nb, chunk, t_real):
    c = pl.program_id(1)
    nc = pl.num_programs(1)
    seqs = range(nb)

    @pl.when(c == 0)
    def _():
        zero_blk = jnp.zeros((HEAD, HEAD), F32)
        for s in seqs:
            for p in range(B_PAIRS):
                s_scr[s, p] = jnp.concatenate([jnp.concatenate([s0_ref[s, 2 * p], zero_blk], axis=1),
                                               jnp.concatenate([zero_blk, s0_ref[s, 2 * p + 1]], axis=1)], axis=0)
        prev_rkv[...] = sh_rkv_ref[...]
        prev_lora[...] = sh_lora_ref[...]

    n_rows = nb * chunk
    rsl = [slice(s * chunk, (s + 1) * chunk) for s in seqs]
    rows = lax.broadcasted_iota(jnp.int32, (n_rows, 1), 0)
    pb = rkv_ref[...].reshape(n_rows, 3 * B_WIDTH)
    lr = lora_ref[...].reshape(n_rows, LORA_PAD)
    pb_prev = pltpu.roll(pb, 1, 0)
    lr_prev = pltpu.roll(lr, 1, 0)
    for s in seqs:
        at = rows == s * chunk
        pb_prev = jnp.where(at, prev_rkv[s], pb_prev)
        lr_prev = jnp.where(at, prev_lora[s], lr_prev)
        prev_rkv[s] = pb[(s + 1) * chunk - 1:(s + 1) * chunk, :]
        prev_lora[s] = lr[(s + 1) * chunk - 1:(s + 1) * chunk, :]
    xs = pb + (pb_prev - pb) * mu_rkv_ref[...]
    xl = lr + (lr_prev - lr) * mu_lora_ref[...]
    r = xs[:, :B_WIDTH]
    k = xs[:, B_WIDTH:2 * B_WIDTH]
    v = xs[:, 2 * B_WIDTH:]
    z = w0_ref[...] + _mm(jnp.tanh(xl[:, :128]), w2_ref[...], passes=3)
    lam = -DECAY_SCALE * _sigmoid(z)
    a = _sigmoid(a0_ref[...] + _mm(xl[:, 128:256], a2_ref[...]))
    gate = _mm(_sigmoid(xl[:, 256:]), g2_ref[...])
    kk = k * kk_ref[...]
    kmod = k * (1.0 + (a - 1.0) * ka_ref[...])
    if t_real < chunk:
        live = rows % chunk < t_real
        lam = jnp.where(live, lam, 0.0)
        kk = jnp.where(live, kk, 0.0)
        kmod = jnp.where(live, kmod, 0.0)
        v = jnp.where(live, v, 0.0)

    cum = _mm_exact_lhs(tri_ref[...], lam, terms=2)
    cum_end = [cum[(s + 1) * chunk - 1:(s + 1) * chunk, :] for s in seqs]
    e_in = jnp.exp(cum)
    e_prev = jnp.exp(cum - lam)
    e_neg = jnp.exp(-cum)
    e_end = [jnp.exp(cum_end[s] - cum[rsl[s]]) for s in seqs]
    g_end = [jnp.exp(cum_end[s]) for s in seqs]

    ri = lax.broadcasted_iota(jnp.int32, (chunk, 128), 0)
    ci = lax.broadcasted_iota(jnp.int32, (chunk, 128), 1) % HEAD
    strict = ci < ri
    lower = ci <= ri
    lane = lax.broadcasted_iota(jnp.int32, (1, 128), 1)
    h1 = lane < HEAD
    bi = lax.broadcasted_iota(jnp.int32, (128, 128), 0) // HEAD
    bj = lax.broadcasted_iota(jnp.int32, (128, 128), 1) // HEAD
    block_diag = bi == bj
    levels = int(math.log2(chunk))
    zero = jnp.zeros((chunk, 128), F32)
    units = [(s, p) for s in seqs for p in range(B_PAIRS)]
    idx = range(len(units))
    csl = [slice(p * 128, (p + 1) * 128) for _, p in units]
    take = lambda arr: [arr[rsl[s], csl[i]] for i, (s, _) in enumerate(units)]

    kappa = [x / jnp.maximum(jnp.sqrt(_head_sums(x * x, h1)), 1e-12) for x in take(kk)]
    a_u, r_u, k_u, v_u = take(a), take(r), take(kmod), take(v)
    e_neg_u = take(e_neg)
    bb = [kappa[i] * a_u[i] for i in idx]
    a_t = [-x * e for x, e in zip(kappa, take(e_prev))]
    r_t = [x * e for x, e in zip(r_u, take(e_in))]
    m4 = []
    for i in idx:
        l4 = jnp.concatenate([jnp.where(h1, a_t[i], zero), jnp.where(h1, zero, a_t[i]),
                              jnp.where(h1, r_t[i], zero), jnp.where(h1, zero, r_t[i])], axis=0)
        m4.append(_mm(l4, _stack_halves(bb[i] * e_neg_u[i], k_u[i] * e_neg_u[i]), NT))
    na = [[jnp.where(strict, m4[i][j * chunk:(j + 1) * chunk], 0.0) for j in range(2)] for i in idx]
    nr = [[jnp.where(lower, m4[i][(2 + j) * chunk:(3 + j) * chunk], 0.0) for j in range(2)] for i in idx]
    zv = [_stack_halves(zero, pltpu.roll(v_u[i], HEAD, 1)) for i in idx]
    zs = [[jnp.where(h1, a_t[i], _mm(na[i][0], zv[i])), jnp.where(h1, _mm(na[i][1], zv[i]), a_t[i])] for i in idx]
    ps = [[na[i][j][:, :chunk] for j in range(2)] for i in idx]
    for lvl in range(levels):
        last = lvl == levels - 1
        for i in idx:
            for j in range(2):
                rhs = zs[i][j] if last else jnp.concatenate([zs[i][j], ps[i][j]], axis=-1)
                upd = _mm(ps[i][j], rhs)
                zs[i][j] = zs[i][j] + upd[:, :128]
                if not last:
                    ps[i][j] = upd[:, 128:]
    ta = [jnp.where(h1, zs[i][0], zs[i][1]) for i in idx]
    pv = [pltpu.roll(jnp.where(h1, zs[i][1], zs[i][0]), HEAD, 1) for i in idx]
    s_old = [s_scr[s, p] for s, p in units]
    u = [_mm(ta[i], s_old[i], NT) + pv[i] for i in idx]
    uv = [_stack_halves(u[i], v_u[i]) for i in idx]
    y = [_mm(r_t[i], s_old[i], NT) + jnp.where(h1, _mm(nr[i][0], uv[i]), _mm(nr[i][1], uv[i])) for i in idx]
    for i, (s, p) in enumerate(units):
        e = e_end[s][:, csl[i]]
        s_new = s_old[i] * g_end[s][:, csl[i]] + _mm(uv[i].T, _stack_halves(bb[i] * e, k_u[i] * e))
        s_scr[s, p] = jnp.where(block_diag, s_new, 0.0)

    mean = [_head_sums(y[i], h1) * (1.0 / HEAD) for i in idx]
    dlt = [y[i] - mean[i] for i in idx]
    var = [_head_sums(d * d, h1) * (1.0 / HEAD) for d in dlt]
    gate_u = take(gate)
    for i, (s, p) in enumerate(units):
        sl = csl[i]
        bonus = _head_sums(r_u[i] * k_u[i] * rk_ref[:, sl], h1)
        yn = dlt[i] * lax.rsqrt(var[i] + LNX_EPS) * lnw_ref[:, sl] + lnb_ref[:, sl]
        out_ref[s, :, sl] = (yn + bonus * v_u[i]) * gate_u[i]

    @pl.when(c == nc - 1)
    def _():
        for s in seqs:
            for p in range(B_PAIRS):
                s_pair = s_scr[s, p]
                s_out_ref[s, 2 * p] = s_pair[:HEAD, :HEAD]
                s_out_ref[s, 2 * p + 1] = s_pair[HEAD:, HEAD:]


def _rwkv(rkv, lora, sh_rkv, sh_lora, s0, prm, chunk, t_real):
    bsz, t, _ = rkv.shape
    nc = t // chunk
    nb = min(bsz, 128 // chunk, MAX_SEQS_PER_STEP)
    tri = np.kron(np.eye(nb), np.tril(np.ones((chunk, chunk)))).astype(np.float32)
    vec = lambda n: _const_spec((1, n))
    row = lambda x: x.reshape(1, -1)
    out, s_out = pl.pallas_call(
        functools.partial(_rwkv_kernel, nb=nb, chunk=chunk, t_real=t_real),
        grid=(bsz // nb, nc),
        in_specs=[pl.BlockSpec((nb, chunk, 3 * B_WIDTH), lambda b, c: (b, c, 0)),
                  pl.BlockSpec((nb, chunk, LORA_PAD), lambda b, c: (b, c, 0)),
                  pl.BlockSpec((nb, 1, 3 * B_WIDTH), lambda b, c: (b, 0, 0)),
                  pl.BlockSpec((nb, 1, LORA_PAD), lambda b, c: (b, 0, 0)),
                  pl.BlockSpec((nb, 2 * B_PAIRS, HEAD, HEAD), lambda b, c: (b, 0, 0, 0)),
                  vec(3 * B_WIDTH), vec(LORA_PAD), vec(B_WIDTH), _const_spec((128, B_WIDTH)),
                  vec(B_WIDTH), _const_spec((128, B_WIDTH)), _const_spec((256, B_WIDTH)),
                  vec(B_WIDTH), vec(B_WIDTH), vec(B_WIDTH), vec(B_WIDTH), vec(B_WIDTH),
                  _const_spec(tri.shape)],
        out_specs=[pl.BlockSpec((nb, chunk, B_WIDTH), lambda b, c: (b, c, 0)),
                   pl.BlockSpec((nb, 2 * B_PAIRS, HEAD, HEAD), lambda b, c: (b, 0, 0, 0))],
        out_shape=[jax.ShapeDtypeStruct((bsz, t, B_WIDTH), F32),
                   jax.ShapeDtypeStruct((bsz, 2 * B_PAIRS, HEAD, HEAD), F32)],
        scratch_shapes=[pltpu.VMEM((nb, B_PAIRS, 128, 128), F32), pltpu.VMEM((nb, 1, 3 * B_WIDTH), F32),
                        pltpu.VMEM((nb, 1, LORA_PAD), F32)],
        compiler_params=_params(2),
        name="rwkv7",
    )(rkv, lora, sh_rkv[:, None], sh_lora[:, None], s0,
      row(prm["mu_rkv"]), row(prm["mu_lora"]), row(prm["w0"]), prm["w2"], row(prm["a0"]), prm["a2"], prm["g2"],
      row(prm["k_k"]), row(prm["k_a"]), row(prm["r_k"]), row(prm["lnx_w"]), row(prm["lnx_b"]),
      jnp.asarray(tri, BF16))
    return out, s_out


def _gla_kernel(pc_ref, s0_ref, lb_ref, gn_ref, tri_ref, sel_ref, out_ref, s_out_ref, s_scr, *, nb, chunk, t_real):
    c = pl.program_id(1)
    nc = pl.num_programs(1)
    seqs = range(nb)

    @pl.when(c == 0)
    def _():
        for s in seqs:
            for h in range(C_HEADS):
                s_scr[s, h] = s0_ref[s, h].T

    n_rows = nb * chunk
    rsl = [slice(s * chunk, (s + 1) * chunk) for s in seqs]
    pc = pc_ref[...].reshape(n_rows, 4 * D_MODEL)
    lb = lb_ref[...]
    xq = pc[:, :C_FDIM]
    q = xq * _sigmoid(xq)
    fg = lb + (1.0 - lb) * _sigmoid(pc[:, C_FDIM:2 * C_FDIM])
    k = 1.0 - fg
    logf = jnp.log(fg)
    v = pc[:, 2 * C_FDIM:2 * C_FDIM + D_MODEL]
    xg = pc[:, 2 * C_FDIM + D_MODEL:]
    if t_real < chunk:
        live = lax.broadcasted_iota(jnp.int32, (n_rows, 1), 0) % chunk < t_real
        logf = jnp.where(live, logf, 0.0)
        k = jnp.where(live, k, 0.0)
        v = jnp.where(live, v, 0.0)

    cum = _mm_exact_lhs(tri_ref[...], logf)
    base = _mm_exact_lhs(sel_ref[...], logf)
    q_in = q * jnp.exp(cum)
    q_loc = q * jnp.exp(cum - base)
    k_loc = k * jnp.exp(jnp.minimum(base - cum, EXP_CLAMP))
    nsub = chunk // GLA_SUB
    k_stack, v_stack, k_end, g_end = [], [], [], []
    for s in seqs:
        cum_s, k_s = cum[rsl[s]], k[rsl[s]]
        cum_end = cum_s[chunk - 1:chunk, :]
        k_end.append(k_s * jnp.exp(cum_end - cum_s))
        g_end.append(jnp.exp(cum_end))
        k_var = [k_loc[rsl[s]]]
        for i in range(1, nsub):
            ref_i = cum_s[i * GLA_SUB - 1:i * GLA_SUB, :]
            k_var.append(k_s * jnp.exp(jnp.minimum(ref_i - cum_s, 0.0)))
        k_stack.append(jnp.concatenate(k_var, axis=0) if nsub > 1 else k_var[0])
        v_stack.append(jnp.concatenate([v[rsl[s]]] * nsub, axis=0) if nsub > 1 else v[rsl[s]])

    ri = lax.broadcasted_iota(jnp.int32, (chunk, nsub * chunk), 0)
    cc = lax.broadcasted_iota(jnp.int32, (chunk, nsub * chunk), 1)
    var = cc // chunk
    ci = cc % chunk
    same_sub = ci // GLA_SUB == ri // GLA_SUB
    att_mask = ((var == 0) & same_sub & (ci <= ri)) | ((ri // GLA_SUB == var) & (ci < var * GLA_SUB))

    units = [(s, h) for s in seqs for h in range(C_HEADS)]
    idx = range(len(units))
    csl = [slice(h * 128, (h + 1) * 128) for _, h in units]
    att = [jnp.where(att_mask, _mm(q_loc[rsl[s], csl[i]], k_stack[s][:, csl[i]], NT), 0.0)
           for i, (s, _) in enumerate(units)]
    s_old = [s_scr[s, h] for s, h in units]
    outs = [_mm(q_in[rsl[s], csl[i]], s_old[i], NT) + _mm(att[i], v_stack[s][:, csl[i]])
            for i, (s, _) in enumerate(units)]
    for i, (s, h) in enumerate(units):
        sl = csl[i]
        s_scr[s, h] = s_old[i] * g_end[s][:, sl] + _mm(_stack_rows_128([v[rsl[s], sl]]).T,
                                                       _stack_rows_128([k_end[s][:, sl]]))
    o = jnp.concatenate([jnp.concatenate(outs[s * C_HEADS:(s + 1) * C_HEADS], axis=-1) for s in seqs], axis=0)
    out_ref[...] = (_rms(o, gn_ref[...]) * (xg * _sigmoid(xg))).reshape(nb, chunk, D_MODEL)

    @pl.when(c == nc - 1)
    def _():
        for s in seqs:
            for h in range(C_HEADS):
                s_out_ref[s, h] = s_scr[s, h].T


def _gla(pc, s0, lb, gn, chunk, t_real):
    bsz, t, _ = pc.shape
    nc = t // chunk
    nb = min(bsz, 128 // chunk, MAX_SEQS_PER_STEP)
    idx = np.arange(chunk)
    eye = np.eye(nb)
    tri = np.kron(eye, np.tril(np.ones((chunk, chunk)))).astype(np.float32)
    sel = np.kron(eye, idx[None, :] < (idx[:, None] // GLA_SUB) * GLA_SUB).astype(np.float32)
    return pl.pallas_call(
        functools.partial(_gla_kernel, nb=nb, chunk=chunk, t_real=t_real),
        grid=(bsz // nb, nc),
        in_specs=[pl.BlockSpec((nb, chunk, 4 * D_MODEL), lambda b, c: (b, c, 0)),
                  pl.BlockSpec((nb, C_HEADS, 128, 128), lambda b, c: (b, 0, 0, 0)),
                  _const_spec((1, C_FDIM)), _const_spec((1, D_MODEL)),
                  _const_spec(tri.shape), _const_spec(sel.shape)],
        out_specs=[pl.BlockSpec((nb, chunk, D_MODEL), lambda b, c: (b, c, 0)),
                   pl.BlockSpec((nb, C_HEADS, 128, 128), lambda b, c: (b, 0, 0, 0))],
        out_shape=[jax.ShapeDtypeStruct((bsz, t, D_MODEL), F32),
                   jax.ShapeDtypeStruct((bsz, C_HEADS, 128, 128), F32)],
        scratch_shapes=[pltpu.VMEM((nb, C_HEADS, 128, 128), F32)],
        compiler_params=_params(2),
        name="hgrn2",
    )(pc, s0, lb.reshape(1, -1), gn.reshape(1, -1), jnp.asarray(tri, BF16), jnp.asarray(sel, BF16))


def _post_kernel(*refs, n_groups, n_mix, final):
    it = iter(refs[:-1])
    take = lambda n: [next(it) for _ in range(n)]
    (x_ref,) = take(1)
    o_refs, l_refs, mix_refs = take(n_groups), take(n_groups), take(n_mix)
    w_refs = take(n_mix + (1 if n_groups else 0))
    gm_ref, up_ref, down_ref = take(3)
    gf_ref = next(it) if final else None
    o_ref = refs[-1]
    mixes = [m[...] for m in mix_refs]
    if n_groups:
        mixes.insert(0, _merge_groups([o[...] for o in o_refs], [l[...] for l in l_refs]))
    x = x_ref[...]
    for mix, w_ref in zip(mixes, w_refs):
        x = x + jnp.dot(mix.astype(BF16), w_ref[...], preferred_element_type=F32)
    h = _rms(x, gm_ref[...]).astype(BF16)
    u = jnp.dot(h, up_ref[...], preferred_element_type=F32)
    u = jnp.square(jnp.maximum(u, 0.0)).astype(BF16)
    x = x + jnp.dot(u, down_ref[...], preferred_element_type=F32)
    if final:
        x = _rms(x, gf_ref[...])
    o_ref[...] = x


def _post(x, groups, mixes, ws, gm, up, down, gf=None, tm=256):
    m = x.shape[0]
    final = gf is not None
    row_spec = lambda n: pl.BlockSpec((tm, n), lambda i: (i, 0))
    rows = [x] + [o for o, _ in groups] + [l for _, l in groups] + list(mixes)
    args = rows + [*ws, gm.reshape(1, -1), up, down]
    specs = ([row_spec(a.shape[1]) for a in rows] + [_const_spec(w.shape) for w in ws]
             + [_const_spec((1, D_MODEL)), _const_spec(up.shape), _const_spec(down.shape)])
    if final:
        args.append(gf.reshape(1, -1))
        specs.append(_const_spec((1, D_MODEL)))
    return pl.pallas_call(
        functools.partial(_post_kernel, n_groups=len(groups), n_mix=len(mixes), final=final),
        grid=(m // tm,), in_specs=specs, out_specs=row_spec(D_MODEL),
        out_shape=jax.ShapeDtypeStruct((m, D_MODEL), F32), compiler_params=_params(1), name="post_mlp",
    )(*args)


def _lora_pad_cols(x):
    pad = lambda a, n: jnp.pad(a, [(0, 0)] * (a.ndim - 1) + [(0, n - a.shape[-1])])
    return jnp.concatenate([pad(x[..., :64], 128), pad(x[..., 64:128], 128), pad(x[..., 128:], 256)], axis=-1)


def _lora_unpad_cols(x):
    return jnp.concatenate([x[..., :64], x[..., 128:192], x[..., 256:416]], axis=-1)


def _prepare(rel_bias, norm_mix, norm_mlp, norm_final, e_w_in, e_mu, e_w0, e_w2, e_a0, e_a2, e_g2, e_k_k, e_k_a,
             e_r_k, e_lnx_w, e_lnx_b, e_w_out, c_w_in, c_lb_raw, c_norm, c_w_out, mlp_up, mlp_down):
    pad_rows = lambda a, n: jnp.pad(a, [(0, n - a.shape[0]), (0, 0)])
    w_in = e_w_in[0]
    sm = jax.nn.softmax(c_lb_raw.astype(F32), axis=0)
    lb = (jnp.cumsum(sm, axis=0) - sm[0])[1]
    return dict(
        rel_bias=rel_bias,
        w_qkv=w_in[:, :A_QKV].astype(BF16),
        w_rkv=w_in[:, A_QKV:A_QKV + 3 * B_WIDTH].astype(BF16),
        w_lora=_lora_pad_cols(w_in[:, A_QKV + 3 * B_WIDTH:]).astype(BF16),
        rwkv=dict(mu_rkv=e_mu[0, :3 * B_WIDTH], mu_lora=_lora_pad_cols(e_mu[0, 3 * B_WIDTH:]),
                  w0=e_w0[0], w2=pad_rows(e_w2[0], 128), a0=e_a0[0], a2=pad_rows(e_a2[0], 128),
                  g2=pad_rows(e_g2[0], 256), k_k=e_k_k[0], k_a=e_k_a[0], r_k=e_r_k[0].reshape(-1),
                  lnx_w=e_lnx_w[0], lnx_b=e_lnx_b[0]),
        w_out_a=e_w_out[0, :A_OUT].astype(BF16), w_out_b=e_w_out[0, A_OUT:].astype(BF16),
        c_w_in=c_w_in[0].astype(BF16), c_w_out=c_w_out[0].astype(BF16), lb=lb, c_norm=c_norm[0],
        norm_mix=norm_mix, norm_mlp=norm_mlp, norm_final=norm_final,
        up=mlp_up.astype(BF16), down=mlp_down.astype(BF16))


def _trunk(x, prm, attend, sh_rkv, sh_lora, wkv0, c0, chunk, t_real):
    bsz, t, _ = x.shape
    flat = lambda a: a.reshape(bsz * t, a.shape[-1])
    xf = flat(x)
    qkv, rkv, lora = _norm_proj(xf, prm["norm_mix"][0], [prm["w_qkv"], prm["w_rkv"], prm["w_lora"]])
    qkv = qkv.reshape(bsz, t, A_QKV)
    rkv = rkv.reshape(bsz, t, 3 * B_WIDTH)
    lora = lora.reshape(bsz, t, LORA_PAD)
    groups, mixes = attend(qkv)
    b_out, wkv = _rwkv(rkv, lora, sh_rkv, sh_lora, wkv0, prm["rwkv"], chunk, t_real)
    x1 = _post(xf, [(flat(o), flat(l)) for o, l in groups], [flat(a) for a in mixes] + [flat(b_out)],
               [prm["w_out_a"], prm["w_out_b"]], prm["norm_mlp"][0], prm["up"][0], prm["down"][0])
    (pc,) = _norm_proj(x1, prm["norm_mix"][1], [prm["c_w_in"]])
    c_out, c_state = _gla(pc.reshape(bsz, t, 4 * D_MODEL), c0, prm["lb"], prm["c_norm"], chunk, t_real)
    y = _post(x1, [], [flat(c_out)], [prm["c_w_out"]], prm["norm_mlp"][1], prm["up"][1], prm["down"][1],
              gf=prm["norm_final"])
    kv_rows = []
    for g, (w, _) in enumerate(A_GROUPS):
        n = min(w, t_real)
        rows = qkv[:, t_real - n:t_real]
        kv = jnp.stack([rows[..., (3 * part + g) * A_OUT:(3 * part + g + 1) * A_OUT] for part in (1, 2)], axis=2)
        kv_rows.append(kv.reshape(1, bsz, n, 2, 4, HEAD))
    shift = jnp.concatenate([rkv[:, t_real - 1], _lora_unpad_cols(lora[:, t_real - 1])], axis=-1)[None]
    return y.reshape(bsz, t, D_MODEL)[:, :t_real], kv_rows, shift, wkv[None], c_state[None]


def kernel(x_prompt, x_sample, cache_a0, cache_a1, cache_a2, state_b_shift, state_b_wkv, state_c, rel_bias, norm_mix, norm_mlp, norm_final, e_w_in, e_mu, e_w0, e_w2, e_a0, e_a2, e_g2, e_k_k, e_k_a, e_r_k, e_lnx_w, e_lnx_b, e_w_out, c_w_in, c_lb_raw, c_norm, c_w_out, mlp_up, mlp_down):
    prm = _prepare(rel_bias, norm_mix, norm_mlp, norm_final, e_w_in, e_mu, e_w0, e_w2, e_a0, e_a2, e_g2, e_k_k,
                   e_k_a, e_r_k, e_lnx_w, e_lnx_b, e_w_out, c_w_in, c_lb_raw, c_norm, c_w_out, mlp_up, mlp_down)

    bp, tp, _ = x_prompt.shape

    def attend_prompt(qkv):
        return [_attn_prompt_group(qkv, prm["rel_bias"], g) for g in range(len(A_GROUPS))], []

    y_p, p_kv, p_shift, p_wkv, p_c = _trunk(
        x_prompt, prm, attend_prompt,
        jnp.zeros((bp, 3 * B_WIDTH), F32), jnp.zeros((bp, LORA_PAD), F32),
        jnp.zeros((bp, 12, HEAD, HEAD), F32), jnp.zeros((bp, C_HEADS, 128, 128), F32), chunk=64, t_real=tp)

    bs, ts, _ = x_sample.shape
    t_pad = 16
    x_s = jnp.pad(x_sample, ((0, 0), (0, t_pad - ts), (0, 0)))
    caches = [cache_a0[0], cache_a1[0], cache_a2[0]]

    def attend_sample(qkv):
        new_rows = qkv[:, :ts].reshape(bs, ts, 3, len(A_GROUPS), 4, HEAD)
        a_out = _attn_sample(new_rows, caches, prm["rel_bias"]).reshape(bs, ts, A_OUT)
        return [], [jnp.pad(a_out, ((0, 0), (0, t_pad - ts), (0, 0)))]

    sh = state_b_shift[0]
    y_s, s_kv, s_shift, s_wkv, s_c = _trunk(
        x_s, prm, attend_sample, sh[:, :3 * B_WIDTH], _lora_pad_cols(sh[:, 3 * B_WIDTH:]),
        state_b_wkv[0], state_c[0], chunk=t_pad, t_real=ts)

    return (y_p, y_s, p_kv[0], p_kv[1], p_kv[2], p_shift, p_wkv, p_c,
            s_kv[0], s_kv[1], s_kv[2], s_shift, s_wkv, s_c)
```

```python
import functools
import math

import jax
import jax.numpy as jnp
import numpy as np
from jax import lax
from jax.experimental import pallas as pl
from jax.experimental.pallas import tpu as pltpu

F32 = jnp.float32
BF16 = jnp.bfloat16

D_MODEL = 1024
D_FF = 4 * D_MODEL
EPS = 1e-6
LNX_EPS = 64e-5
HEAD = 64
A_GROUPS = ((128, 1), (512, 4), (2048, 16))
A_KEYS = 128
A_TILES = 4
A_QKV = 2304
A_OUT = 256
N_BUCKETS = 32
BUCKET_MAX_DIST = 2048
B_WIDTH = 768
B_PAIRS = B_WIDTH // 128
LORA_PAD = 512
C_FDIM = 1024
C_HEADS = 8
NEG = -1e30
DECAY_SCALE = math.exp(-0.5)
GLA_SUB = 16
EXP_CLAMP = 80.0
MAX_SEQS_PER_STEP = 4

VMEM_LIMIT = 56 * 1024 * 1024

NN = (((1,), (0,)), ((), ()))
NT = (((1,), (1,)), ((), ()))


def _dg(a, b, dims):
    return lax.dot_general(a, b, dims, preferred_element_type=F32)


def _split2(x):
    hi = x.astype(BF16)
    lo = (x - hi.astype(F32)).astype(BF16)
    return hi, lo


def _split3(x):
    hi = x.astype(BF16)
    r1 = x - hi.astype(F32)
    mid = r1.astype(BF16)
    lo = (r1 - mid.astype(F32)).astype(BF16)
    return hi, mid, lo


def _mm(a, b, dims=NN, passes=1):
    if passes == 1:
        return _dg(a.astype(BF16), b.astype(BF16), dims)
    ah, al = _split2(a)
    bh, bl = _split2(b)
    return _dg(ah, bh, dims) + (_dg(ah, bl, dims) + _dg(al, bh, dims))


def _mm_exact_lhs(a_bf16, b, terms=3):
    if terms == 2:
        bh, bl = _split2(b)
        return _dg(a_bf16, bh, NN) + _dg(a_bf16, bl, NN)
    bh, bm, bl = _split3(b)
    return _dg(a_bf16, bh, NN) + (_dg(a_bf16, bm, NN) + _dg(a_bf16, bl, NN))


def _stack_rows_128(parts):
    n = sum(p.shape[0] for p in parts)
    if n < 128:
        parts = list(parts) + [jnp.zeros((128 - n, 128), F32)]
    return jnp.concatenate(parts, axis=0) if len(parts) > 1 else parts[0]


def _stack_halves(top, bottom):
    n = top.shape[0]
    if n == 64:
        return jnp.concatenate([top, bottom], axis=0)
    pad = jnp.zeros((64 - n, 128), F32)
    return jnp.concatenate([top, pad, bottom, pad], axis=0)


def _sigmoid(x):
    return 1.0 / (1.0 + jnp.exp(-x))


def _rms(x, g):
    ms = jnp.mean(x * x, axis=-1, keepdims=True)
    return x * lax.rsqrt(ms + EPS) * g


def _const_spec(shape):
    nd = len(shape)
    return pl.BlockSpec(shape, lambda *_: (0,) * nd)


def _params(n_grid):
    return pltpu.CompilerParams(dimension_semantics=("arbitrary",) * n_grid,
                                vmem_limit_bytes=VMEM_LIMIT)


def _norm_proj_kernel(x_ref, g_ref, *refs):
    n = len(refs) // 2
    h = _rms(x_ref[...], g_ref[...]).astype(BF16)
    for w_ref, o_ref in zip(refs[:n], refs[n:]):
        o_ref[...] = jnp.dot(h, w_ref[...], preferred_element_type=F32)


def _norm_proj(x, g, ws, tm=256):
    m = x.shape[0]
    return pl.pallas_call(
        _norm_proj_kernel,
        grid=(m // tm,),
        in_specs=[pl.BlockSpec((tm, D_MODEL), lambda i: (i, 0)), _const_spec((1, D_MODEL))]
        + [_const_spec(w.shape) for w in ws],
        out_specs=[pl.BlockSpec((tm, w.shape[1]), lambda i: (i, 0)) for w in ws],
        out_shape=[jax.ShapeDtypeStruct((m, w.shape[1]), F32) for w in ws],
        compiler_params=_params(1),
        name="norm_proj",
    )(x, g.reshape(1, D_MODEL), *ws)


def _t5_bucket_np(dist):
    max_exact = N_BUCKETS // 2
    d = np.maximum(dist, 1).astype(np.float32)
    large = max_exact + (np.log(d / np.float32(max_exact)) / np.float32(math.log(BUCKET_MAX_DIST / max_exact))
                         * np.float32(N_BUCKETS - max_exact)).astype(np.int32)
    large = np.minimum(large, N_BUCKETS - 1)
    return np.where(dist < max_exact, dist, large).astype(np.int32)


def _bias_from_buckets(idx, rb_ref, head):
    acc = jnp.full(idx.shape, NEG, F32)
    for b in range(N_BUCKETS):
        acc = jnp.where(idx == b, rb_ref[b, head], acc)
    return acc


def _attn_prompt_kernel(rb_ref, bkt_ref, q_ref, kp_ref, kc_ref, vp_ref, vc_ref, o_ref, l_ref, bias_scr, *, group, dil):
    pair = pl.program_id(2)
    first = (pl.program_id(0) == 0) & (pl.program_id(1) == 0) & (pair == 0)

    @pl.when(first)
    def _():
        idx = bkt_ref[...]
        for h in range(4):
            bias_scr[h] = _bias_from_buckets(idx, rb_ref, group * 4 + h)

    col = lax.broadcasted_iota(jnp.int32, (1, 2 * A_KEYS), 1)
    prev_pen = jnp.where((col < A_KEYS) & (pl.program_id(1) == 0), NEG, 0.0).astype(F32)
    h1 = lax.broadcasted_iota(jnp.int32, (1, 128), 1) < HEAD
    h2 = jnp.logical_not(h1)
    bias_in = [bias_scr[2 * pair + j] for j in range(2)]
    bias_edge = [b + prev_pen for b in bias_in]

    def attend(tiles):
        n = range(len(tiles))
        q = [q_ref[rows, :] * (HEAD ** -0.5) for rows, _, _ in tiles]
        k = [jnp.concatenate([pref[0][prows, :], kc_ref[rows, :]], axis=0).astype(BF16)
             for rows, pref, prows in tiles]
        v = [jnp.concatenate([pref[1][prows, :], vc_ref[rows, :]], axis=0).astype(BF16)
             for rows, pref, prows in tiles]
        bias = [bias_edge if tiles[i][1][0] is kp_ref else bias_in for i in n]
        s = [[_dg(jnp.where(mine, q[i], 0.0).astype(BF16), k[i], NT) + bias[i][j]
              for j, mine in enumerate((h1, h2))] for i in n]
        m = [[jnp.max(x, axis=-1, keepdims=True) for x in s[i]] for i in n]
        p = [[jnp.exp(x - mx) for x, mx in zip(s[i], m[i])] for i in n]
        den = [[jnp.sum(x, axis=-1, keepdims=True) for x in p[i]] for i in n]
        o = [[_dg(x.astype(BF16), v[i], NN) for x in p[i]] for i in n]
        for i in n:
            rows = tiles[i][0]
            inv = jnp.where(h1, 1.0 / den[i][0], 1.0 / den[i][1])
            o_ref[rows, :] = jnp.where(h1, o[i][0], o[i][1]) * inv
            l_ref[rows, :] = jnp.where(h1, m[i][0] + jnp.log(den[i][0]), m[i][1] + jnp.log(den[i][1]))

    prev_refs = (kp_ref, vp_ref)
    cur_refs = (kc_ref, vc_ref)
    if dil == 1:
        tile = lambda j: slice(j * A_KEYS, (j + 1) * A_KEYS)
        attend([(tile(j), prev_refs if j == 0 else cur_refs, slice(None) if j == 0 else tile(j - 1))
                for j in range(A_TILES)])
    elif dil == A_TILES:
        attend([(pl.ds(r, A_KEYS, stride=dil), prev_refs, pl.ds(r, A_KEYS, stride=dil)) for r in range(dil)])
    else:
        def body(it, carry):
            rows = [pl.ds(it * A_TILES + u, A_KEYS, stride=dil) for u in range(A_TILES)]
            attend([(r, prev_refs, r) for r in rows])
            return carry
        lax.fori_loop(0, dil // A_TILES, body, 0)


def _prompt_bucket_map(dil):
    qi = np.arange(A_KEYS)[:, None]
    ki = np.arange(2 * A_KEYS)[None, :]
    j = qi + A_KEYS - ki
    return np.where((j >= 0) & (j <= A_KEYS), _t5_bucket_np(np.clip(j, 0, A_KEYS) * dil), -1).astype(np.int32)


def _attn_prompt_group(qkv, rel_bias, group):
    bsz, t, _ = qkv.shape
    dil = A_GROUPS[group][1]
    blk = A_KEYS * max(dil, A_TILES)
    prev_blk = A_KEYS if dil == 1 else blk
    per = blk // prev_blk

    def spec(part, prev):
        def idx(b, i, pair):
            return (b, jnp.maximum(i * per - 1, 0) if prev else i, (part * 3 + group) * 2 + pair)
        return pl.BlockSpec((None, prev_blk if prev else blk, 128), idx)

    out_spec = pl.BlockSpec((None, blk, 128), lambda b, i, pair: (b, i, pair))
    return pl.pallas_call(
        functools.partial(_attn_prompt_kernel, group=group, dil=dil),
        grid=(bsz, t // blk, 2),
        in_specs=[pl.BlockSpec(memory_space=pltpu.SMEM), _const_spec((A_KEYS, 2 * A_KEYS)),
                  spec(0, False), spec(1, True), spec(1, False), spec(2, True), spec(2, False)],
        out_specs=[out_spec, out_spec],
        out_shape=[jax.ShapeDtypeStruct((bsz, t, A_OUT), F32)] * 2,
        scratch_shapes=[pltpu.VMEM((4, A_KEYS, 2 * A_KEYS), F32)],
        compiler_params=_params(3),
        name=f"attn_prompt_g{group}",
    )(rel_bias, jnp.asarray(_prompt_bucket_map(dil)), qkv, qkv, qkv, qkv, qkv)


def _merge_groups(outs, lses):
    m = functools.reduce(jnp.maximum, lses)
    ws = [jnp.exp(l - m) for l in lses]
    num = functools.reduce(lambda a, b: a + b, [w * o for w, o in zip(ws, outs)])
    return num / functools.reduce(lambda a, b: a + b, ws)


def _attn_sample_kernel(rb_ref, bc0, bc1, bc2, bn0, bn1, bn2, qkv_ref, c0_ref, c1_ref, c2_ref, out_ref,
                        bias_c0, bias_c1, bias_c2, bias_n):
    bias_c = (bias_c0, bias_c1, bias_c2)

    @pl.when(pl.program_id(0) == 0)
    def _():
        for g, (bc, bn) in enumerate(((bc0, bn0), (bc1, bn1), (bc2, bn2))):
            for h in range(4):
                bias_c[g][h] = _bias_from_buckets(bc[...], rb_ref, g * 4 + h)
                bias_n[g * 4 + h] = _bias_from_buckets(bn[...], rb_ref, g * 4 + h)

    qkv = qkv_ref[...]
    c_refs = (c0_ref, c1_ref, c2_ref)
    units = [(g, h) for g in range(len(A_GROUPS)) for h in range(4)]
    cols = [(g * 4 + h) * HEAD for g, h in units]
    q = [(qkv[:, c:c + HEAD] * (HEAD ** -0.5)).astype(BF16) for c in cols]
    kn = [qkv[:, 3 * A_OUT + c:3 * A_OUT + c + HEAD].astype(BF16) for c in cols]
    vn = [qkv[:, 6 * A_OUT + c:6 * A_OUT + c + HEAD].astype(BF16) for c in cols]
    n = range(len(units))
    s_c = [_dg(q[i], c_refs[g][0, h].astype(BF16), NN) + bias_c[g][h] for i, (g, h) in enumerate(units)]
    s_n = [_dg(q[i], kn[i], NT) + bias_n[i] for i in n]
    m = [jnp.maximum(jnp.max(s_c[i], axis=-1, keepdims=True), jnp.max(s_n[i], axis=-1, keepdims=True)) for i in n]
    p_c = [jnp.exp(s_c[i] - m[i]) for i in n]
    p_n = [jnp.exp(s_n[i] - m[i]) for i in n]
    den = [jnp.sum(p_c[i], axis=-1, keepdims=True) + jnp.sum(p_n[i], axis=-1, keepdims=True) for i in n]
    o = [(_dg(p_c[i].astype(BF16), c_refs[g][1, h].astype(BF16), NT) + _dg(p_n[i].astype(BF16), vn[i], NN)) / den[i]
         for i, (g, h) in enumerate(units)]
    lse = [m[i] + jnp.log(den[i]) for i in n]
    out_ref[...] = jnp.concatenate([_merge_groups(o[h::4], lse[h::4]) for h in range(4)], axis=-1)


def _sample_bucket_maps(window, dil, tp, t_real):
    t = np.arange(tp)[:, None]
    dist_c = window + t - np.arange(window)[None, :]
    ok_c = (dist_c % dil == 0) & (dist_c // dil <= A_KEYS) & (t < t_real)
    dist_n = t - np.arange(tp)[None, :]
    ok_n = (dist_n >= 0) & (dist_n % dil == 0) & (dist_n // dil <= A_KEYS)
    mc = np.where(ok_c, _t5_bucket_np(np.maximum(dist_c, 0)), -1).astype(np.int32)
    mn = np.where(ok_n, _t5_bucket_np(np.maximum(dist_n, 0)), -1).astype(np.int32)
    return mc, mn


def _attn_sample(qkv, caches, rel_bias, t_real):
    bsz, tp, _ = qkv.shape
    maps = [_sample_bucket_maps(w, d, tp, t_real) for w, d in A_GROUPS]
    mcs = [jnp.asarray(m[0]) for m in maps]
    mns = [jnp.asarray(m[1]) for m in maps]
    return pl.pallas_call(
        _attn_sample_kernel,
        grid=(bsz,),
        in_specs=[pl.BlockSpec(memory_space=pltpu.SMEM)]
        + [_const_spec(m.shape) for m in mcs] + [_const_spec(m.shape) for m in mns]
        + [pl.BlockSpec((None, tp, A_QKV), lambda b: (b, 0, 0))]
        + [pl.BlockSpec((None, 2, 4, HEAD, w), lambda b: (b, 0, 0, 0, 0)) for w, _ in A_GROUPS],
        out_specs=pl.BlockSpec((None, tp, A_OUT), lambda b: (b, 0, 0)),
        out_shape=jax.ShapeDtypeStruct((bsz, tp, A_OUT), F32),
        scratch_shapes=[pltpu.VMEM((4, tp, w), F32) for w, _ in A_GROUPS] + [pltpu.VMEM((12, tp, tp), F32)],
        compiler_params=_params(1),
        name="attn_sample",
    )(rel_bias, *mcs, *mns, qkv, *caches)


def _head_sums(x, h1):
    total = jnp.sum(x, axis=-1, keepdims=True)
    first = jnp.sum(jnp.where(h1, x, 0.0), axis=-1, keepdims=True)
    return jnp.where(h1, first, total - first)


def _rwkv_kernel(rkv_ref, lora_ref, sh_rkv_ref, sh_lora_ref, s0_ref, mu_rkv_ref, mu_lora_ref, w0_ref, w2_ref,
                 a0_ref, a2_ref, g2_ref, kk_ref, ka_ref, rk_ref, lnw_ref, lnb_ref, tri_ref,
                 out_ref, s_out_ref, s_scr, prev_rkv, prev_lora, *, nb, chunk, t_real):
    c = pl.program_id(1)
    nc = pl.num_programs(1)
    seqs = range(nb)

    @pl.when(c == 0)
    def _():
        zero_blk = jnp.zeros((HEAD, HEAD), F32)
        for s in seqs:
            for p in range(B_PAIRS):
                s_scr[s, p] = jnp.concatenate([jnp.concatenate([s0_ref[s, 2 * p], zero_blk], axis=1),
                                               jnp.concatenate([zero_blk, s0_ref[s, 2 * p + 1]], axis=1)], axis=0)
        prev_rkv[...] = sh_rkv_ref[...]
        prev_lora[...] = sh_lora_ref[...]

    n_rows = nb * chunk
    rsl = [slice(s * chunk, (s + 1) * chunk) for s in seqs]
    rows = lax.broadcasted_iota(jnp.int32, (n_rows, 1), 0)
    pb = rkv_ref[...].reshape(n_rows, 3 * B_WIDTH)
    lr = lora_ref[...].reshape(n_rows, LORA_PAD)
    pb_prev = pltpu.roll(pb, 1, 0)
    lr_prev = pltpu.roll(lr, 1, 0)
    for s in seqs:
        at = rows == s * chunk
        pb_prev = jnp.where(at, prev_rkv[s], pb_prev)
        lr_prev = jnp.where(at, prev_lora[s], lr_prev)
        prev_rkv[s] = pb[(s + 1) * chunk - 1:(s + 1) * chunk, :]
        prev_lora[s] = lr[(s + 1) * chunk - 1:(s + 1) * chunk, :]
    xs = pb + (pb_prev - pb) * mu_rkv_ref[...]
    xl = lr + (lr_prev - lr) * mu_lora_ref[...]
    r = xs[:, :B_WIDTH]
    k = xs[:, B_WIDTH:2 * B_WIDTH]
    v = xs[:, 2 * B_WIDTH:]
    z = w0_ref[...] + _mm(jnp.tanh(xl[:, :128]), w2_ref[...], passes=3)
    lam = -DECAY_SCALE * _sigmoid(z)
    a = _sigmoid(a0_ref[...] + _mm(xl[:, 128:256], a2_ref[...]))
    gate = _mm(_sigmoid(xl[:, 256:]), g2_ref[...])
    kk = k * kk_ref[...]
    kmod = k * (1.0 + (a - 1.0) * ka_ref[...])
    if t_real < chunk:
        live = rows % chunk < t_real
        lam = jnp.where(live, lam, 0.0)
        kk = jnp.where(live, kk, 0.0)
        kmod = jnp.where(live, kmod, 0.0)
        v = jnp.where(live, v, 0.0)

    cum = _mm_exact_lhs(tri_ref[...], lam, terms=2)
    cum_end = [cum[(s + 1) * chunk - 1:(s + 1) * chunk, :] for s in seqs]
    e_in = jnp.exp(cum)
    e_prev = jnp.exp(cum - lam)
    e_neg = jnp.exp(-cum)
    e_end = [jnp.exp(cum_end[s] - cum[rsl[s]]) for s in seqs]
    g_end = [jnp.exp(cum_end[s]) for s in seqs]

    ri = lax.broadcasted_iota(jnp.int32, (chunk, 128), 0)
    ci = lax.broadcasted_iota(jnp.int32, (chunk, 128), 1) % HEAD
    strict = ci < ri
    lower = ci <= ri
    lane = lax.broadcasted_iota(jnp.int32, (1, 128), 1)
    h1 = lane < HEAD
    bi = lax.broadcasted_iota(jnp.int32, (128, 128), 0) // HEAD
    bj = lax.broadcasted_iota(jnp.int32, (128, 128), 1) // HEAD
    block_diag = bi == bj
    levels = int(math.log2(chunk))
    zero = jnp.zeros((chunk, 128), F32)
    units = [(s, p) for s in seqs for p in range(B_PAIRS)]
    idx = range(len(units))
    csl = [slice(p * 128, (p + 1) * 128) for _, p in units]
    take = lambda arr: [arr[rsl[s], csl[i]] for i, (s, _) in enumerate(units)]

    kappa = [x / jnp.maximum(jnp.sqrt(_head_sums(x * x, h1)), 1e-12) for x in take(kk)]
    a_u, r_u, k_u, v_u = take(a), take(r), take(kmod), take(v)
    e_neg_u = take(e_neg)
    bb = [kappa[i] * a_u[i] for i in idx]
    a_t = [-x * e for x, e in zip(kappa, take(e_prev))]
    r_t = [x * e for x, e in zip(r_u, take(e_in))]
    m4 = []
    for i in idx:
        l4 = jnp.concatenate([jnp.where(h1, a_t[i], zero), jnp.where(h1, zero, a_t[i]),
                              jnp.where(h1, r_t[i], zero), jnp.where(h1, zero, r_t[i])], axis=0)
        m4.append(_mm(l4, _stack_halves(bb[i] * e_neg_u[i], k_u[i] * e_neg_u[i]), NT))
    na = [[jnp.where(strict, m4[i][j * chunk:(j + 1) * chunk], 0.0) for j in range(2)] for i in idx]
    nr = [[jnp.where(lower, m4[i][(2 + j) * chunk:(3 + j) * chunk], 0.0) for j in range(2)] for i in idx]
    zv = [_stack_halves(zero, pltpu.roll(v_u[i], HEAD, 1)) for i in idx]
    zs = [[jnp.where(h1, a_t[i], _mm(na[i][0], zv[i])), jnp.where(h1, _mm(na[i][1], zv[i]), a_t[i])] for i in idx]
    ps = [[na[i][j][:, :chunk] for j in range(2)] for i in idx]
    for lvl in range(levels):
        last = lvl == levels - 1
        for i in idx:
            for j in range(2):
                rhs = zs[i][j] if last else jnp.concatenate([zs[i][j], ps[i][j]], axis=-1)
                upd = _mm(ps[i][j], rhs)
                zs[i][j] = zs[i][j] + upd[:, :128]
                if not last:
                    ps[i][j] = upd[:, 128:]
    ta = [jnp.where(h1, zs[i][0], zs[i][1]) for i in idx]
    pv = [pltpu.roll(jnp.where(h1, zs[i][1], zs[i][0]), HEAD, 1) for i in idx]
    s_old = [s_scr[s, p] for s, p in units]
    u = [_mm(ta[i], s_old[i], NT) + pv[i] for i in idx]
    uv = [_stack_halves(u[i], v_u[i]) for i in idx]
    y = [_mm(r_t[i], s_old[i], NT) + jnp.where(h1, _mm(nr[i][0], uv[i]), _mm(nr[i][1], uv[i])) for i in idx]
    for i, (s, p) in enumerate(units):
        e = e_end[s][:, csl[i]]
        s_new = s_old[i] * g_end[s][:, csl[i]] + _mm(uv[i].T, _stack_halves(bb[i] * e, k_u[i] * e))
        s_scr[s, p] = jnp.where(block_diag, s_new, 0.0)

    mean = [_head_sums(y[i], h1) * (1.0 / HEAD) for i in idx]
    dlt = [y[i] - mean[i] for i in idx]
    var = [_head_sums(d * d, h1) * (1.0 / HEAD) for d in dlt]
    gate_u = take(gate)
    for i, (s, p) in enumerate(units):
        sl = csl[i]
        bonus = _head_sums(r_u[i] * k_u[i] * rk_ref[:, sl], h1)
        yn = dlt[i] * lax.rsqrt(var[i] + LNX_EPS) * lnw_ref[:, sl] + lnb_ref[:, sl]
        out_ref[s, :, sl] = (yn + bonus * v_u[i]) * gate_u[i]

    @pl.when(c == nc - 1)
    def _():
        for s in seqs:
            for p in range(B_PAIRS):
                s_pair = s_scr[s, p]
                s_out_ref[s, 2 * p] = s_pair[:HEAD, :HEAD]
                s_out_ref[s, 2 * p + 1] = s_pair[HEAD:, HEAD:]


def _rwkv(rkv, lora, sh_rkv, sh_lora, s0, prm, chunk, t_real):
    bsz, t, _ = rkv.shape
    nc = t // chunk
    nb = min(bsz, 128 // chunk, MAX_SEQS_PER_STEP)
    tri = np.kron(np.eye(nb), np.tril(np.ones((chunk, chunk)))).astype(np.float32)
    vec = lambda n: _const_spec((1, n))
    row = lambda x: x.reshape(1, -1)
    out, s_out = pl.pallas_call(
        functools.partial(_rwkv_kernel, nb=nb, chunk=chunk, t_real=t_real),
        grid=(bsz // nb, nc),
        in_specs=[pl.BlockSpec((nb, chunk, 3 * B_WIDTH), lambda b, c: (b, c, 0)),
                  pl.BlockSpec((nb, chunk, LORA_PAD), lambda b, c: (b, c, 0)),
                  pl.BlockSpec((nb, 1, 3 * B_WIDTH), lambda b, c: (b, 0, 0)),
                  pl.BlockSpec((nb, 1, LORA_PAD), lambda b, c: (b, 0, 0)),
                  pl.BlockSpec((nb, 2 * B_PAIRS, HEAD, HEAD), lambda b, c: (b, 0, 0, 0)),
                  vec(3 * B_WIDTH), vec(LORA_PAD), vec(B_WIDTH), _const_spec((128, B_WIDTH)),
                  vec(B_WIDTH), _const_spec((128, B_WIDTH)), _const_spec((256, B_WIDTH)),
                  vec(B_WIDTH), vec(B_WIDTH), vec(B_WIDTH), vec(B_WIDTH), vec(B_WIDTH),
                  _const_spec(tri.shape)],
        out_specs=[pl.BlockSpec((nb, chunk, B_WIDTH), lambda b, c: (b, c, 0)),
                   pl.BlockSpec((nb, 2 * B_PAIRS, HEAD, HEAD), lambda b, c: (b, 0, 0, 0))],
        out_shape=[jax.ShapeDtypeStruct((bsz, t, B_WIDTH), F32),
                   jax.ShapeDtypeStruct((bsz, 2 * B_PAIRS, HEAD, HEAD), F32)],
        scratch_shapes=[pltpu.VMEM((nb, B_PAIRS, 128, 128), F32), pltpu.VMEM((nb, 1, 3 * B_WIDTH), F32),
                        pltpu.VMEM((nb, 1, LORA_PAD), F32)],
        compiler_params=_params(2),
        name="rwkv7",
    )(rkv, lora, sh_rkv[:, None], sh_lora[:, None], s0,
      row(prm["mu_rkv"]), row(prm["mu_lora"]), row(prm["w0"]), prm["w2"], row(prm["a0"]), prm["a2"], prm["g2"],
      row(prm["k_k"]), row(prm["k_a"]), row(prm["r_k"]), row(prm["lnx_w"]), row(prm["lnx_b"]),
      jnp.asarray(tri, BF16))
    return out, s_out


def _gla_kernel(pc_ref, s0_ref, lb_ref, gn_ref, tri_ref, sel_ref, out_ref, s_out_ref, s_scr, *, nb, chunk, t_real):
    c = pl.program_id(1)
    nc = pl.num_programs(1)
    seqs = range(nb)

    @pl.when(c == 0)
    def _():
        for s in seqs:
            for h in range(C_HEADS):
                s_scr[s, h] = s0_ref[s, h].T

    n_rows = nb * chunk
    rsl = [slice(s * chunk, (s + 1) * chunk) for s in seqs]
    pc = pc_ref[...].reshape(n_rows, 4 * D_MODEL)
    lb = lb_ref[...]
    xq = pc[:, :C_FDIM]
    q = xq * _sigmoid(xq)
    fg = lb + (1.0 - lb) * _sigmoid(pc[:, C_FDIM:2 * C_FDIM])
    k = 1.0 - fg
    logf = jnp.log(fg)
    v = pc[:, 2 * C_FDIM:2 * C_FDIM + D_MODEL]
    xg = pc[:, 2 * C_FDIM + D_MODEL:]
    if t_real < chunk:
        live = lax.broadcasted_iota(jnp.int32, (n_rows, 1), 0) % chunk < t_real
        logf = jnp.where(live, logf, 0.0)
        k = jnp.where(live, k, 0.0)
        v = jnp.where(live, v, 0.0)

    cum = _mm_exact_lhs(tri_ref[...], logf)
    base = _mm_exact_lhs(sel_ref[...], logf)
    q_in = q * jnp.exp(cum)
    q_loc = q * jnp.exp(cum - base)
    k_loc = k * jnp.exp(jnp.minimum(base - cum, EXP_CLAMP))
    nsub = chunk // GLA_SUB
    k_stack, v_stack, k_end, g_end = [], [], [], []
    for s in seqs:
        cum_s, k_s = cum[rsl[s]], k[rsl[s]]
        cum_end = cum_s[chunk - 1:chunk, :]
        k_end.append(k_s * jnp.exp(cum_end - cum_s))
        g_end.append(jnp.exp(cum_end))
        k_var = [k_loc[rsl[s]]]
        for i in range(1, nsub):
            ref_i = cum_s[i * GLA_SUB - 1:i * GLA_SUB, :]
            k_var.append(k_s * jnp.exp(jnp.minimum(ref_i - cum_s, 0.0)))
        k_stack.append(jnp.concatenate(k_var, axis=0) if nsub > 1 else k_var[0])
        v_stack.append(jnp.concatenate([v[rsl[s]]] * nsub, axis=0) if nsub > 1 else v[rsl[s]])

    ri = lax.broadcasted_iota(jnp.int32, (chunk, nsub * chunk), 0)
    cc = lax.broadcasted_iota(jnp.int32, (chunk, nsub * chunk), 1)
    var = cc // chunk
    ci = cc % chunk
    same_sub = ci // GLA_SUB == ri // GLA_SUB
    att_mask = ((var == 0) & same_sub & (ci <= ri)) | ((ri // GLA_SUB == var) & (ci < var * GLA_SUB))

    units = [(s, h) for s in seqs for h in range(C_HEADS)]
    idx = range(len(units))
    csl = [slice(h * 128, (h + 1) * 128) for _, h in units]
    att = [jnp.where(att_mask, _mm(q_loc[rsl[s], csl[i]], k_stack[s][:, csl[i]], NT), 0.0)
           for i, (s, _) in enumerate(units)]
    s_old = [s_scr[s, h] for s, h in units]
    outs = [_mm(q_in[rsl[s], csl[i]], s_old[i], NT) + _mm(att[i], v_stack[s][:, csl[i]])
            for i, (s, _) in enumerate(units)]
    for i, (s, h) in enumerate(units):
        sl = csl[i]
        s_scr[s, h] = s_old[i] * g_end[s][:, sl] + _mm(_stack_rows_128([v[rsl[s], sl]]).T,
                                                       _stack_rows_128([k_end[s][:, sl]]))
    o = jnp.concatenate([jnp.concatenate(outs[s * C_HEADS:(s + 1) * C_HEADS], axis=-1) for s in seqs], axis=0)
    out_ref[...] = (_rms(o, gn_ref[...]) * (xg * _sigmoid(xg))).reshape(nb, chunk, D_MODEL)

    @pl.when(c == nc - 1)
    def _():
        for s in seqs:
            for h in range(C_HEADS):
                s_out_ref[s, h] = s_scr[s, h].T


def _gla(pc, s0, lb, gn, chunk, t_real):
    bsz, t, _ = pc.shape
    nc = t // chunk
    nb = min(bsz, 128 // chunk, MAX_SEQS_PER_STEP)
    idx = np.arange(chunk)
    eye = np.eye(nb)
    tri = np.kron(eye, np.tril(np.ones((chunk, chunk)))).astype(np.float32)
    sel = np.kron(eye, idx[None, :] < (idx[:, None] // GLA_SUB) * GLA_SUB).astype(np.float32)
    return pl.pallas_call(
        functools.partial(_gla_kernel, nb=nb, chunk=chunk, t_real=t_real),
        grid=(bsz // nb, nc),
        in_specs=[pl.BlockSpec((nb, chunk, 4 * D_MODEL), lambda b, c: (b, c, 0)),
                  pl.BlockSpec((nb, C_HEADS, 128, 128), lambda b, c: (b, 0, 0, 0)),
                  _const_spec((1, C_FDIM)), _const_spec((1, D_MODEL)),
                  _const_spec(tri.shape), _const_spec(sel.shape)],
        out_specs=[pl.BlockSpec((nb, chunk, D_MODEL), lambda b, c: (b, c, 0)),
                   pl.BlockSpec((nb, C_HEADS, 128, 128), lambda b, c: (b, 0, 0, 0))],
        out_shape=[jax.ShapeDtypeStruct((bsz, t, D_MODEL), F32),
                   jax.ShapeDtypeStruct((bsz, C_HEADS, 128, 128), F32)],
        scratch_shapes=[pltpu.VMEM((nb, C_HEADS, 128, 128), F32)],
        compiler_params=_params(2),
        name="hgrn2",
    )(pc, s0, lb.reshape(1, -1), gn.reshape(1, -1), jnp.asarray(tri, BF16), jnp.asarray(sel, BF16))


def _post_kernel(*refs, n_groups, n_mix, final):
    it = iter(refs[:-1])
    take = lambda n: [next(it) for _ in range(n)]
    (x_ref,) = take(1)
    o_refs, l_refs, mix_refs = take(n_groups), take(n_groups), take(n_mix)
    w_refs = take(n_mix + (1 if n_groups else 0))
    gm_ref, up_ref, down_ref = take(3)
    gf_ref = next(it) if final else None
    o_ref = refs[-1]
    mixes = [m[...] for m in mix_refs]
    if n_groups:
        mixes.insert(0, _merge_groups([o[...] for o in o_refs], [l[...] for l in l_refs]))
    x = x_ref[...]
    for mix, w_ref in zip(mixes, w_refs):
        x = x + jnp.dot(mix.astype(BF16), w_ref[...], preferred_element_type=F32)
    h = _rms(x, gm_ref[...]).astype(BF16)
    u = jnp.dot(h, up_ref[...], preferred_element_type=F32)
    u = jnp.square(jnp.maximum(u, 0.0)).astype(BF16)
    x = x + jnp.dot(u, down_ref[...], preferred_element_type=F32)
    if final:
        x = _rms(x, gf_ref[...])
    o_ref[...] = x


def _post(x, groups, mixes, ws, gm, up, down, gf=None, tm=256):
    m = x.shape[0]
    final = gf is not None
    row_spec = lambda n: pl.BlockSpec((tm, n), lambda i: (i, 0))
    rows = [x] + [o for o, _ in groups] + [l for _, l in groups] + list(mixes)
    args = rows + [*ws, gm.reshape(1, -1), up, down]
    specs = ([row_spec(a.shape[1]) for a in rows] + [_const_spec(w.shape) for w in ws]
             + [_const_spec((1, D_MODEL)), _const_spec(up.shape), _const_spec(down.shape)])
    if final:
        args.append(gf.reshape(1, -1))
        specs.append(_const_spec((1, D_MODEL)))
    return pl.pallas_call(
        functools.partial(_post_kernel, n_groups=len(groups), n_mix=len(mixes), final=final),
        grid=(m // tm,), in_specs=specs, out_specs=row_spec(D_MODEL),
        out_shape=jax.ShapeDtypeStruct((m, D_MODEL), F32), compiler_params=_params(1), name="post_mlp",
    )(*args)


def _lora_pad_cols(x):
    pad = lambda a, n: jnp.pad(a, [(0, 0)] * (a.ndim - 1) + [(0, n - a.shape[-1])])
    return jnp.concatenate([pad(x[..., :64], 128), pad(x[..., 64:128], 128), pad(x[..., 128:], 256)], axis=-1)


def _lora_unpad_cols(x):
    return jnp.concatenate([x[..., :64], x[..., 128:192], x[..., 256:416]], axis=-1)


def _prepare(rel_bias, norm_mix, norm_mlp, norm_final, e_w_in, e_mu, e_w0, e_w2, e_a0, e_a2, e_g2, e_k_k, e_k_a,
             e_r_k, e_lnx_w, e_lnx_b, e_w_out, c_w_in, c_lb_raw, c_norm, c_w_out, mlp_up, mlp_down):
    pad_rows = lambda a, n: jnp.pad(a, [(0, n - a.shape[0]), (0, 0)])
    w_in = e_w_in[0]
    sm = jax.nn.softmax(c_lb_raw.astype(F32), axis=0)
    lb = (jnp.cumsum(sm, axis=0) - sm[0])[1]
    return dict(
        rel_bias=rel_bias,
        w_qkv=w_in[:, :A_QKV].astype(BF16),
        w_rkv=w_in[:, A_QKV:A_QKV + 3 * B_WIDTH].astype(BF16),
        w_lora=_lora_pad_cols(w_in[:, A_QKV + 3 * B_WIDTH:]).astype(BF16),
        rwkv=dict(mu_rkv=e_mu[0, :3 * B_WIDTH], mu_lora=_lora_pad_cols(e_mu[0, 3 * B_WIDTH:]),
                  w0=e_w0[0], w2=pad_rows(e_w2[0], 128), a0=e_a0[0], a2=pad_rows(e_a2[0], 128),
                  g2=pad_rows(e_g2[0], 256), k_k=e_k_k[0], k_a=e_k_a[0], r_k=e_r_k[0].reshape(-1),
                  lnx_w=e_lnx_w[0], lnx_b=e_lnx_b[0]),
        w_out_a=e_w_out[0, :A_OUT].astype(BF16), w_out_b=e_w_out[0, A_OUT:].astype(BF16),
        c_w_in=c_w_in[0].astype(BF16), c_w_out=c_w_out[0].astype(BF16), lb=lb, c_norm=c_norm[0],
        norm_mix=norm_mix, norm_mlp=norm_mlp, norm_final=norm_final,
        up=mlp_up.astype(BF16), down=mlp_down.astype(BF16))


def _trunk(x, prm, attend, sh_rkv, sh_lora, wkv0, c0, chunk, t_real):
    bsz, t, _ = x.shape
    flat = lambda a: a.reshape(bsz * t, a.shape[-1])
    xf = flat(x)
    qkv, rkv, lora = _norm_proj(xf, prm["norm_mix"][0], [prm["w_qkv"], prm["w_rkv"], prm["w_lora"]])
    qkv = qkv.reshape(bsz, t, A_QKV)
    rkv = rkv.reshape(bsz, t, 3 * B_WIDTH)
    lora = lora.reshape(bsz, t, LORA_PAD)
    groups, mixes = attend(qkv)
    b_out, wkv = _rwkv(rkv, lora, sh_rkv, sh_lora, wkv0, prm["rwkv"], chunk, t_real)
    x1 = _post(xf, [(flat(o), flat(l)) for o, l in groups], [flat(a) for a in mixes] + [flat(b_out)],
               [prm["w_out_a"], prm["w_out_b"]], prm["norm_mlp"][0], prm["up"][0], prm["down"][0])
    (pc,) = _norm_proj(x1, prm["norm_mix"][1], [prm["c_w_in"]])
    c_out, c_state = _gla(pc.reshape(bsz, t, 4 * D_MODEL), c0, prm["lb"], prm["c_norm"], chunk, t_real)
    y = _post(x1, [], [flat(c_out)], [prm["c_w_out"]], prm["norm_mlp"][1], prm["up"][1], prm["down"][1],
              gf=prm["norm_final"])
    kv_rows = []
    for g, (w, _) in enumerate(A_GROUPS):
        n = min(w, t_real)
        rows = qkv[:, t_real - n:t_real]
        kv = jnp.stack([rows[..., (3 * part + g) * A_OUT:(3 * part + g + 1) * A_OUT] for part in (1, 2)], axis=2)
        kv_rows.append(kv.reshape(1, bsz, n, 2, 4, HEAD))
    shift = jnp.concatenate([rkv[:, t_real - 1], _lora_unpad_cols(lora[:, t_real - 1])], axis=-1)[None]
    return y.reshape(bsz, t, D_MODEL)[:, :t_real], kv_rows, shift, wkv[None], c_state[None]


def kernel(x_prompt, x_sample, cache_a0, cache_a1, cache_a2, state_b_shift, state_b_wkv, state_c, rel_bias, norm_mix, norm_mlp, norm_final, e_w_in, e_mu, e_w0, e_w2, e_a0, e_a2, e_g2, e_k_k, e_k_a, e_r_k, e_lnx_w, e_lnx_b, e_w_out, c_w_in, c_lb_raw, c_norm, c_w_out, mlp_up, mlp_down):
    prm = _prepare(rel_bias, norm_mix, norm_mlp, norm_final, e_w_in, e_mu, e_w0, e_w2, e_a0, e_a2, e_g2, e_k_k,
                   e_k_a, e_r_k, e_lnx_w, e_lnx_b, e_w_out, c_w_in, c_lb_raw, c_norm, c_w_out, mlp_up, mlp_down)

    bp, tp, _ = x_prompt.shape

    def attend_prompt(qkv):
        return [_attn_prompt_group(qkv, prm["rel_bias"], g) for g in range(len(A_GROUPS))], []

    y_p, p_kv, p_shift, p_wkv, p_c = _trunk(
        x_prompt, prm, attend_prompt,
        jnp.zeros((bp, 3 * B_WIDTH), F32), jnp.zeros((bp, LORA_PAD), F32),
        jnp.zeros((bp, 12, HEAD, HEAD), F32), jnp.zeros((bp, C_HEADS, 128, 128), F32), chunk=64, t_real=tp)

    bs, ts, _ = x_sample.shape
    t_pad = 16
    x_s = jnp.pad(x_sample, ((0, 0), (0, t_pad - ts), (0, 0)))
    caches = [jnp.transpose(c[0], (0, 2, 3, 4, 1)) for c in (cache_a0, cache_a1, cache_a2)]

    def attend_sample(qkv):
        return [], [_attn_sample(qkv, caches, prm["rel_bias"], ts)]

    sh = state_b_shift[0]
    y_s, s_kv, s_shift, s_wkv, s_c = _trunk(
        x_s, prm, attend_sample, sh[:, :3 * B_WIDTH], _lora_pad_cols(sh[:, 3 * B_WIDTH:]),
        state_b_wkv[0], state_c[0], chunk=t_pad, t_real=ts)

    return (y_p, y_s, p_kv[0], p_kv[1], p_kv[2], p_shift, p_wkv, p_c,
            s_kv[0], s_kv[1], s_kv[2], s_shift, s_wkv, s_c)
```

```python
import functools
import math

import jax
import jax.numpy as jnp
import numpy as np
from jax import lax
from jax.experimental import pallas as pl
from jax.experimental.pallas import tpu as pltpu

F32 = jnp.float32
BF16 = jnp.bfloat16

D_MODEL = 1024
D_FF = 4 * D_MODEL
EPS = 1e-6
LNX_EPS = 64e-5
HEAD = 64
A_GROUPS = ((128, 1), (512, 4), (2048, 16))
A_KEYS = 128
A_TILES = 4
A_BLOCK = A_KEYS * A_GROUPS[-1][1]
A_QKV = 2304
A_OUT = 256
N_BUCKETS = 32
BUCKET_MAX_DIST = 2048
B_WIDTH = 768
B_PAIRS = B_WIDTH // 128
LORA_PAD = 512
C_FDIM = 1024
C_HEADS = 8
NEG = -1e30
LOG2E = math.log2(math.e)
DECAY_SCALE = math.exp(-0.5)
GLA_SUB = 16
EXP_CLAMP = 80.0
MAX_SEQS_PER_STEP = 4

VMEM_LIMIT = 56 * 1024 * 1024

NN = (((1,), (0,)), ((), ()))
NT = (((1,), (1,)), ((), ()))


def _dg(a, b, dims):
    return lax.dot_general(a, b, dims, preferred_element_type=F32)


def _split2(x):
    hi = x.astype(BF16)
    lo = (x - hi.astype(F32)).astype(BF16)
    return hi, lo


def _split3(x):
    hi = x.astype(BF16)
    r1 = x - hi.astype(F32)
    mid = r1.astype(BF16)
    lo = (r1 - mid.astype(F32)).astype(BF16)
    return hi, mid, lo


def _mm(a, b, dims=NN):
    return _dg(a.astype(BF16), b.astype(BF16), dims)


def _mm_exact_lhs(a_bf16, b, terms=3):
    if terms == 2:
        bh, bl = _split2(b)
        return _dg(a_bf16, bh, NN) + _dg(a_bf16, bl, NN)
    bh, bm, bl = _split3(b)
    return _dg(a_bf16, bh, NN) + (_dg(a_bf16, bm, NN) + _dg(a_bf16, bl, NN))


def _stack_rows_128(parts):
    n = sum(p.shape[0] for p in parts)
    if n < 128:
        parts = list(parts) + [jnp.zeros((128 - n, 128), F32)]
    return jnp.concatenate(parts, axis=0) if len(parts) > 1 else parts[0]


def _stack_halves(top, bottom):
    n = top.shape[0]
    if n == 64:
        return jnp.concatenate([top, bottom], axis=0)
    pad = jnp.zeros((64 - n, 128), F32)
    return jnp.concatenate([top, pad, bottom, pad], axis=0)


def _sigmoid(x):
    return 0.5 * jnp.tanh(0.5 * x) + 0.5


def _rms(x, g):
    ms = jnp.mean(x * x, axis=-1, keepdims=True)
    return x * lax.rsqrt(ms + EPS) * g


def _const_spec(shape):
    nd = len(shape)
    return pl.BlockSpec(shape, lambda *_: (0,) * nd)


def _params(n_grid):
    return pltpu.CompilerParams(dimension_semantics=("arbitrary",) * n_grid,
                                vmem_limit_bytes=VMEM_LIMIT)


def _norm_proj_kernel(x_ref, g_ref, *refs):
    n = len(refs) // 2
    h = _rms(x_ref[...], g_ref[...]).astype(BF16)
    for w_ref, o_ref in zip(refs[:n], refs[n:]):
        o_ref[...] = jnp.dot(h, w_ref[...], preferred_element_type=F32)


def _norm_proj(x, g, ws, tm=256):
    m = x.shape[0]
    return pl.pallas_call(
        _norm_proj_kernel,
        grid=(m // tm,),
        in_specs=[pl.BlockSpec((tm, D_MODEL), lambda i: (i, 0)), _const_spec((1, D_MODEL))]
        + [_const_spec(w.shape) for w in ws],
        out_specs=[pl.BlockSpec((tm, w.shape[1]), lambda i: (i, 0)) for w in ws],
        out_shape=[jax.ShapeDtypeStruct((m, w.shape[1]), F32) for w in ws],
        compiler_params=_params(1),
        name="norm_proj",
    )(x, g.reshape(1, D_MODEL), *ws)


def _t5_bucket_np(dist):
    max_exact = N_BUCKETS // 2
    d = np.maximum(dist, 1).astype(np.float32)
    large = max_exact + (np.log(d / np.float32(max_exact)) / np.float32(math.log(BUCKET_MAX_DIST / max_exact))
                         * np.float32(N_BUCKETS - max_exact)).astype(np.int32)
    large = np.minimum(large, N_BUCKETS - 1)
    return np.where(dist < max_exact, dist, large).astype(np.int32)


def _bias_from_buckets(idx, rb_ref, head):
    acc = jnp.full(idx.shape, NEG, F32)
    for b in range(N_BUCKETS):
        acc = jnp.where(idx == b, rb_ref[b, head], acc)
    return acc


def _tile_rows(start, dil):
    if dil > 1:
        return pl.ds(start, A_KEYS, stride=dil)
    return pl.ds(start if isinstance(start, int) else pl.multiple_of(start, A_KEYS), A_KEYS)


def _attn_prompt_kernel(rb_ref, bkt_ref, *refs):
    ng = len(A_GROUPS)
    q_refs, kc_refs, vc_refs, kp_refs, vp_refs = (refs[i * ng:(i + 1) * ng] for i in range(5))
    out_ref, bias_scr, o_scr, m_scr, d_scr = refs[5 * ng:]
    pair = pl.program_id(2)
    first = (pl.program_id(0) == 0) & (pl.program_id(1) == 0) & (pair == 0)

    @pl.when(first)
    def _():
        for g in range(ng):
            for h in range(4):
                bias_scr[g * 4 + h] = _bias_from_buckets(bkt_ref[g], rb_ref, g * 4 + h) * LOG2E

    col = lax.broadcasted_iota(jnp.int32, (1, 2 * A_KEYS), 1)
    prev_pen = jnp.where((col < A_KEYS) & (pl.program_id(1) == 0), NEG, 0.0).astype(F32)
    h1 = lax.broadcasted_iota(jnp.int32, (1, 128), 1) < HEAD
    h2 = jnp.logical_not(h1)

    def attend(g, tiles):
        dil = A_GROUPS[g][1]
        n = range(len(tiles))
        rows = [_tile_rows(start, dil) for start, _, _ in tiles]
        prev = [((kp_refs[g], vp_refs[g]) if edge else (kc_refs[g], vc_refs[g]), _tile_rows(pstart, dil))
                for _, edge, pstart in tiles]
        bias_in = [bias_scr[g * 4 + 2 * pair + j] for j in range(2)]
        bias_edge = [b + prev_pen for b in bias_in]
        q = [q_refs[g][r, :] * (HEAD ** -0.5 * LOG2E) for r in rows]
        k = [jnp.concatenate([prev[i][0][0][prev[i][1], :], kc_refs[g][rows[i], :]], axis=0).astype(BF16)
             for i in n]
        v = [jnp.concatenate([prev[i][0][1][prev[i][1], :], vc_refs[g][rows[i], :]], axis=0).astype(BF16)
             for i in n]
        s = [[_dg(jnp.where(mine, q[i], 0.0).astype(BF16), k[i], NT) + (bias_edge if tiles[i][1] else bias_in)[j]
              for j, mine in enumerate((h1, h2))] for i in n]
        m = [[jnp.max(x, axis=-1, keepdims=True) for x in s[i]] for i in n]
        p = [[jnp.exp2(x - mx) for x, mx in zip(s[i], m[i])] for i in n]
        den = [[jnp.sum(x, axis=-1, keepdims=True) for x in p[i]] for i in n]
        o = [[_dg(x.astype(BF16), v[i], NN) for x in p[i]] for i in n]
        for i in n:
            o_scr[g, rows[i], :] = jnp.where(h1, o[i][0], o[i][1])
            m_scr[g, rows[i], :] = jnp.where(h1, m[i][0], m[i][1])
            d_scr[g, rows[i], :] = jnp.where(h1, den[i][0], den[i][1])

    n_tiles = A_BLOCK // A_KEYS
    for g, (_, dil) in enumerate(A_GROUPS):
        span = A_KEYS * dil
        single_span = span == A_BLOCK

        def batch(it, static, g=g, dil=dil, span=span, single_span=single_span):
            tiles = []
            for u in range(A_TILES):
                t = it * A_TILES + u
                s_idx, r = (t // dil, t % dil) if static else _span_residue(it, u, dil)
                edge = single_span or (static and s_idx == 0)
                tiles.append((s_idx * span + r, edge, r if edge else (s_idx - 1) * span + r))
            attend(g, tiles)

        batch(0, True)

        def body(it, carry, batch=batch):
            batch(it, False)
            return carry
        lax.fori_loop(1, n_tiles // A_TILES, body, 0)

    def merge(i, carry):
        rows = pl.ds(pl.multiple_of(i * A_KEYS, A_KEYS), A_KEYS)
        ms = [m_scr[g, rows, :] for g in range(ng)]
        top = functools.reduce(jnp.maximum, ms)
        ws = [jnp.exp2(x - top) for x in ms]
        num = functools.reduce(lambda a, b: a + b, [w * o_scr[g, rows, :] for g, w in enumerate(ws)])
        den = functools.reduce(lambda a, b: a + b, [w * d_scr[g, rows, :] for g, w in enumerate(ws)])
        out_ref[rows, :] = num / den
        return carry
    lax.fori_loop(0, n_tiles, merge, 0)


def _span_residue(it, u, dil):
    if dil >= A_TILES:
        per = dil // A_TILES
        return it // per, (it % per) * A_TILES + u
    return it * (A_TILES // dil) + u // dil, u % dil


def _prompt_bucket_map(dil):
    qi = np.arange(A_KEYS)[:, None]
    ki = np.arange(2 * A_KEYS)[None, :]
    j = qi + A_KEYS - ki
    return np.where((j >= 0) & (j <= A_KEYS), _t5_bucket_np(np.clip(j, 0, A_KEYS) * dil), -1).astype(np.int32)


def _attn_prompt(qkv, rel_bias):
    bsz, t, _ = qkv.shape
    ng = len(A_GROUPS)

    def cur(part, g):
        return pl.BlockSpec((None, A_BLOCK, 128), lambda b, i, pair: (b, i, (part * ng + g) * 2 + pair))

    def before(part, g):
        span = A_KEYS * A_GROUPS[g][1]
        per = A_BLOCK // span
        return pl.BlockSpec((None, span, 128),
                            lambda b, i, pair: (b, jnp.maximum(i * per - 1, 0), (part * ng + g) * 2 + pair))

    specs = [cur(part, g) for part in range(3) for g in range(ng)] + \
            [before(part, g) for part in (1, 2) for g in range(ng)]
    maps = jnp.asarray(np.stack([_prompt_bucket_map(d) for _, d in A_GROUPS]))
    return pl.pallas_call(
        _attn_prompt_kernel,
        grid=(bsz, t // A_BLOCK, 2),
        in_specs=[pl.BlockSpec(memory_space=pltpu.SMEM), _const_spec(maps.shape)] + specs,
        out_specs=pl.BlockSpec((None, A_BLOCK, 128), lambda b, i, pair: (b, i, pair)),
        out_shape=jax.ShapeDtypeStruct((bsz, t, A_OUT), F32),
        scratch_shapes=[pltpu.VMEM((4 * ng, A_KEYS, 2 * A_KEYS), F32)] + [pltpu.VMEM((ng, A_BLOCK, 128), F32)] * 3,
        compiler_params=_params(3),
        name="attn_prompt",
    )(rel_bias, maps, *([qkv] * (5 * ng)))


def _merge_groups(outs, lses):
    m = functools.reduce(jnp.maximum, lses)
    ws = [jnp.exp(l - m) for l in lses]
    num = functools.reduce(lambda a, b: a + b, [w * o for w, o in zip(ws, outs)])
    return num / functools.reduce(lambda a, b: a + b, ws)


def _attn_sample_kernel(rb_ref, bc0, bc1, bc2, bn0, bn1, bn2, qkv_ref, c0_ref, c1_ref, c2_ref, out_ref,
                        bias_c0, bias_c1, bias_c2, bias_n):
    bias_c = (bias_c0, bias_c1, bias_c2)

    @pl.when(pl.program_id(0) == 0)
    def _():
        for g, (bc, bn) in enumerate(((bc0, bn0), (bc1, bn1), (bc2, bn2))):
            for h in range(4):
                bias_c[g][h] = _bias_from_buckets(bc[...], rb_ref, g * 4 + h)
                bias_n[g * 4 + h] = _bias_from_buckets(bn[...], rb_ref, g * 4 + h)

    qkv = qkv_ref[...]
    c_refs = (c0_ref, c1_ref, c2_ref)
    units = [(g, h) for g in range(len(A_GROUPS)) for h in range(4)]
    cols = [(g * 4 + h) * HEAD for g, h in units]
    q = [(qkv[:, c:c + HEAD] * (HEAD ** -0.5)).astype(BF16) for c in cols]
    kn = [qkv[:, 3 * A_OUT + c:3 * A_OUT + c + HEAD].astype(BF16) for c in cols]
    vn = [qkv[:, 6 * A_OUT + c:6 * A_OUT + c + HEAD].astype(BF16) for c in cols]
    n = range(len(units))
    s_c = [_dg(q[i], c_refs[g][0, h].astype(BF16), NN) + bias_c[g][h] for i, (g, h) in enumerate(units)]
    s_n = [_dg(q[i], kn[i], NT) + bias_n[i] for i in n]
    m = [jnp.maximum(jnp.max(s_c[i], axis=-1, keepdims=True), jnp.max(s_n[i], axis=-1, keepdims=True)) for i in n]
    p_c = [jnp.exp(s_c[i] - m[i]) for i in n]
    p_n = [jnp.exp(s_n[i] - m[i]) for i in n]
    den = [jnp.sum(p_c[i], axis=-1, keepdims=True) + jnp.sum(p_n[i], axis=-1, keepdims=True) for i in n]
    o = [(_dg(p_c[i].astype(BF16), c_refs[g][1, h].astype(BF16), NT) + _dg(p_n[i].astype(BF16), vn[i], NN)) / den[i]
         for i, (g, h) in enumerate(units)]
    lse = [m[i] + jnp.log(den[i]) for i in n]
    out_ref[...] = jnp.concatenate([_merge_groups(o[h::4], lse[h::4]) for h in range(4)], axis=-1)


def _sample_bucket_maps(window, dil, tp, t_real):
    t = np.arange(tp)[:, None]
    dist_c = window + t - np.arange(window)[None, :]
    ok_c = (dist_c % dil == 0) & (dist_c // dil <= A_KEYS) & (t < t_real)
    dist_n = t - np.arange(tp)[None, :]
    ok_n = (dist_n >= 0) & (dist_n % dil == 0) & (dist_n // dil <= A_KEYS)
    mc = np.where(ok_c, _t5_bucket_np(np.maximum(dist_c, 0)), -1).astype(np.int32)
    mn = np.where(ok_n, _t5_bucket_np(np.maximum(dist_n, 0)), -1).astype(np.int32)
    return mc, mn


def _attn_sample(qkv, caches, rel_bias, t_real):
    bsz, tp, _ = qkv.shape
    maps = [_sample_bucket_maps(w, d, tp, t_real) for w, d in A_GROUPS]
    mcs = [jnp.asarray(m[0]) for m in maps]
    mns = [jnp.asarray(m[1]) for m in maps]
    return pl.pallas_call(
        _attn_sample_kernel,
        grid=(bsz,),
        in_specs=[pl.BlockSpec(memory_space=pltpu.SMEM)]
        + [_const_spec(m.shape) for m in mcs] + [_const_spec(m.shape) for m in mns]
        + [pl.BlockSpec((None, tp, A_QKV), lambda b: (b, 0, 0))]
        + [pl.BlockSpec((None, 2, 4, HEAD, w), lambda b: (b, 0, 0, 0, 0)) for w, _ in A_GROUPS],
        out_specs=pl.BlockSpec((None, tp, A_OUT), lambda b: (b, 0, 0)),
        out_shape=jax.ShapeDtypeStruct((bsz, tp, A_OUT), F32),
        scratch_shapes=[pltpu.VMEM((4, tp, w), F32) for w, _ in A_GROUPS] + [pltpu.VMEM((12, tp, tp), F32)],
        compiler_params=_params(1),
        name="attn_sample",
    )(rel_bias, *mcs, *mns, qkv, *caches)


def _head_sums(x, h1, fn=lambda s: s):
    first = jnp.sum(jnp.where(h1, x, 0.0), axis=-1, keepdims=True)
    second = jnp.sum(jnp.where(h1, 0.0, x), axis=-1, keepdims=True)
    return jnp.where(h1, fn(first), fn(second))


def _rwkv_kernel(rkv_ref, lora_ref, sh_rkv_ref, sh_lora_ref, s0_ref, mu_rkv_ref, mu_lora_ref, w0_ref, w2_ref,
                 a0_ref, a2_ref, g2_ref, kk_ref, ka_ref, rk_ref, lnw_ref, lnb_ref, tri_ref,
                 out_ref, s_out_ref, s_scr, prev_rkv, prev_lora, *, nb, chunk, t_real):
    c = pl.program_id(1)
    nc = pl.num_programs(1)
    seqs = range(nb)

    @pl.when(c == 0)
    def _():
        zero_blk = jnp.zeros((HEAD, HEAD), F32)
        for s in seqs:
            for p in range(B_PAIRS):
                s_scr[s, p] = jnp.concatenate([jnp.concatenate([s0_ref[s, 2 * p], zero_blk], axis=1),
                                               jnp.concatenate([zero_blk, s0_ref[s, 2 * p + 1]], axis=1)], axis=0)
        prev_rkv[...] = sh_rkv_ref[...]
        prev_lora[...] = sh_lora_ref[...]

    n_rows = nb * chunk
    rsl = [slice(s * chunk, (s + 1) * chunk) for s in seqs]
    rows = lax.broadcasted_iota(jnp.int32, (n_rows, 1), 0)
    pb = rkv_ref[...].reshape(n_rows, 3 * B_WIDTH)
    lr = lora_ref[...].reshape(n_rows, LORA_PAD)
    pb_prev = pltpu.roll(pb, 1, 0)
    lr_prev = pltpu.roll(lr, 1, 0)
    for s in seqs:
        at = rows == s * chunk
        pb_prev = jnp.where(at, prev_rkv[s], pb_prev)
        lr_prev = jnp.where(at, prev_lora[s], lr_prev)
        prev_rkv[s] = pb[(s + 1) * chunk - 1:(s + 1) * chunk, :]
        prev_lora[s] = lr[(s + 1) * chunk - 1:(s + 1) * chunk, :]
    xs = pb + (pb_prev - pb) * mu_rkv_ref[...]
    xl = lr + (lr_prev - lr) * mu_lora_ref[...]
    r = xs[:, :B_WIDTH]
    k = xs[:, B_WIDTH:2 * B_WIDTH]
    v = xs[:, 2 * B_WIDTH:]
    th_hi, th_lo = _split2(jnp.tanh(xl[:, :128]))
    z = w0_ref[...] + (_dg(th_hi, w2_ref[0], NN) + (_dg(th_hi, w2_ref[1], NN) + _dg(th_lo, w2_ref[0], NN)))
    lam = -DECAY_SCALE * _sigmoid(z)
    a = _sigmoid(a0_ref[...] + _mm(xl[:, 128:256], a2_ref[...]))
    gate = _mm(_sigmoid(xl[:, 256:]), g2_ref[...])
    kk = k * kk_ref[...]
    kmod = k * (1.0 + (a - 1.0) * ka_ref[...])
    if t_real < chunk:
        live = rows % chunk < t_real
        lam = jnp.where(live, lam, 0.0)
        kk = jnp.where(live, kk, 0.0)
        kmod = jnp.where(live, kmod, 0.0)
        v = jnp.where(live, v, 0.0)

    cum = _mm_exact_lhs(tri_ref[...], lam, terms=2)
    cum_end = [cum[(s + 1) * chunk - 1:(s + 1) * chunk, :] for s in seqs]
    e_in = jnp.exp(cum)
    e_prev = jnp.exp(cum - lam)
    e_neg = jnp.exp(-cum)
    e_end = [jnp.exp(cum_end[s] - cum[rsl[s]]) for s in seqs]
    g_end = [jnp.exp(cum_end[s]) for s in seqs]

    ri = lax.broadcasted_iota(jnp.int32, (chunk, 128), 0)
    ci = lax.broadcasted_iota(jnp.int32, (chunk, 128), 1) % HEAD
    strict = ci < ri
    lower = ci <= ri
    lane = lax.broadcasted_iota(jnp.int32, (1, 128), 1)
    h1 = lane < HEAD
    bi = lax.broadcasted_iota(jnp.int32, (128, 128), 0) // HEAD
    bj = lax.broadcasted_iota(jnp.int32, (128, 128), 1) // HEAD
    block_diag = bi == bj
    levels = int(math.log2(chunk))
    zero = jnp.zeros((chunk, 128), F32)
    units = [(s, p) for s in seqs for p in range(B_PAIRS)]
    idx = range(len(units))
    csl = [slice(p * 128, (p + 1) * 128) for _, p in units]
    take = lambda arr: [arr[rsl[s], csl[i]] for i, (s, _) in enumerate(units)]

    kappa = [x * _head_sums(x * x, h1, lambda q: lax.rsqrt(jnp.maximum(q, 1e-24))) for x in take(kk)]
    a_u, r_u, k_u, v_u = take(a), take(r), take(kmod), take(v)
    e_neg_u = take(e_neg)
    bb = [kappa[i] * a_u[i] for i in idx]
    a_t = [-x * e for x, e in zip(kappa, take(e_prev))]
    r_t = [x * e for x, e in zip(r_u, take(e_in))]
    m4 = []
    for i in idx:
        l4 = jnp.concatenate([jnp.where(h1, a_t[i], zero), jnp.where(h1, zero, a_t[i]),
                              jnp.where(h1, r_t[i], zero), jnp.where(h1, zero, r_t[i])], axis=0)
        m4.append(_mm(l4, _stack_halves(bb[i] * e_neg_u[i], k_u[i] * e_neg_u[i]), NT))
    na = [[jnp.where(strict, m4[i][j * chunk:(j + 1) * chunk], 0.0) for j in range(2)] for i in idx]
    nr = [[jnp.where(lower, m4[i][(2 + j) * chunk:(3 + j) * chunk], 0.0) for j in range(2)] for i in idx]
    zv = [_stack_halves(zero, pltpu.roll(v_u[i], HEAD, 1)).astype(BF16) for i in idx]
    zs = [[jnp.where(h1, a_t[i], _mm(na[i][0], zv[i])), jnp.where(h1, _mm(na[i][1], zv[i]), a_t[i])] for i in idx]
    ps = [[na[i][j][:, :chunk].astype(BF16) for j in range(2)] for i in idx]
    for lvl in range(levels):
        last = lvl == levels - 1
        for i in idx:
            for j in range(2):
                z_b = zs[i][j].astype(BF16)
                upd = _dg(ps[i][j], z_b if last else jnp.concatenate([z_b, ps[i][j]], axis=-1), NN)
                zs[i][j] = zs[i][j] + upd[:, :128]
                if not last:
                    ps[i][j] = upd[:, 128:].astype(BF16)
    ta = [jnp.where(h1, zs[i][0], zs[i][1]) for i in idx]
    pv = [pltpu.roll(jnp.where(h1, zs[i][1], zs[i][0]), HEAD, 1) for i in idx]
    s_old = [s_scr[s, p] for s, p in units]
    s_b = [x.astype(BF16) for x in s_old]
    u = [_mm(ta[i], s_b[i], NT) + pv[i] for i in idx]
    uv = [_stack_halves(u[i], v_u[i]) for i in idx]
    uv_b = [x.astype(BF16) for x in uv]
    y = [_mm(r_t[i], s_b[i], NT) + jnp.where(h1, _mm(nr[i][0], uv_b[i]), _mm(nr[i][1], uv_b[i])) for i in idx]
    for i, (s, p) in enumerate(units):
        e = e_end[s][:, csl[i]]
        s_new = s_old[i] * g_end[s][:, csl[i]] + _mm(uv[i].T, _stack_halves(bb[i] * e, k_u[i] * e))
        s_scr[s, p] = jnp.where(block_diag, s_new, 0.0)

    mean = [_head_sums(y[i], h1) * (1.0 / HEAD) for i in idx]
    dlt = [y[i] - mean[i] for i in idx]
    var = [_head_sums(d * d, h1) * (1.0 / HEAD) for d in dlt]
    gate_u = take(gate)
    for i, (s, p) in enumerate(units):
        sl = csl[i]
        bonus = _head_sums(r_u[i] * k_u[i] * rk_ref[:, sl], h1)
        yn = dlt[i] * lax.rsqrt(var[i] + LNX_EPS) * lnw_ref[:, sl] + lnb_ref[:, sl]
        out_ref[s, :, sl] = (yn + bonus * v_u[i]) * gate_u[i]

    @pl.when(c == nc - 1)
    def _():
        for s in seqs:
            for p in range(B_PAIRS):
                s_pair = s_scr[s, p]
                s_out_ref[s, 2 * p] = s_pair[:HEAD, :HEAD]
                s_out_ref[s, 2 * p + 1] = s_pair[HEAD:, HEAD:]


def _rwkv(rkv, lora, sh_rkv, sh_lora, s0, prm, chunk, t_real):
    bsz, t, _ = rkv.shape
    nc = t // chunk
    nb = min(bsz, 128 // chunk, MAX_SEQS_PER_STEP)
    tri = np.kron(np.eye(nb), np.tril(np.ones((chunk, chunk)))).astype(np.float32)
    vec = lambda n: _const_spec((1, n))
    row = lambda x: x.reshape(1, -1)
    out, s_out = pl.pallas_call(
        functools.partial(_rwkv_kernel, nb=nb, chunk=chunk, t_real=t_real),
        grid=(bsz // nb, nc),
        in_specs=[pl.BlockSpec((nb, chunk, 3 * B_WIDTH), lambda b, c: (b, c, 0)),
                  pl.BlockSpec((nb, chunk, LORA_PAD), lambda b, c: (b, c, 0)),
                  pl.BlockSpec((nb, 1, 3 * B_WIDTH), lambda b, c: (b, 0, 0)),
                  pl.BlockSpec((nb, 1, LORA_PAD), lambda b, c: (b, 0, 0)),
                  pl.BlockSpec((nb, 2 * B_PAIRS, HEAD, HEAD), lambda b, c: (b, 0, 0, 0)),
                  vec(3 * B_WIDTH), vec(LORA_PAD), vec(B_WIDTH), _const_spec((2, 128, B_WIDTH)),
                  vec(B_WIDTH), _const_spec((128, B_WIDTH)), _const_spec((256, B_WIDTH)),
                  vec(B_WIDTH), vec(B_WIDTH), vec(B_WIDTH), vec(B_WIDTH), vec(B_WIDTH),
                  _const_spec(tri.shape)],
        out_specs=[pl.BlockSpec((nb, chunk, B_WIDTH), lambda b, c: (b, c, 0)),
                   pl.BlockSpec((nb, 2 * B_PAIRS, HEAD, HEAD), lambda b, c: (b, 0, 0, 0))],
        out_shape=[jax.ShapeDtypeStruct((bsz, t, B_WIDTH), F32),
                   jax.ShapeDtypeStruct((bsz, 2 * B_PAIRS, HEAD, HEAD), F32)],
        scratch_shapes=[pltpu.VMEM((nb, B_PAIRS, 128, 128), F32), pltpu.VMEM((nb, 1, 3 * B_WIDTH), F32),
                        pltpu.VMEM((nb, 1, LORA_PAD), F32)],
        compiler_params=_params(2),
        name="rwkv7",
    )(rkv, lora, sh_rkv[:, None], sh_lora[:, None], s0,
      row(prm["mu_rkv"]), row(prm["mu_lora"]), row(prm["w0"]), prm["w2"], row(prm["a0"]), prm["a2"], prm["g2"],
      row(prm["k_k"]), row(prm["k_a"]), row(prm["r_k"]), row(prm["lnx_w"]), row(prm["lnx_b"]),
      jnp.asarray(tri, BF16))
    return out, s_out


def _gla_kernel(pc_ref, s0_ref, lb_ref, gn_ref, sums_ref, out_ref, s_out_ref, s_scr, *, nb, chunk, t_real, layer):
    c = pl.program_id(1)
    nc = pl.num_programs(1)
    seqs = range(nb)

    @pl.when(c == 0)
    def _():
        for s in seqs:
            for h in range(C_HEADS):
                s_scr[s, h] = s0_ref[s, h].T

    n_rows = nb * chunk
    rsl = [slice(s * chunk, (s + 1) * chunk) for s in seqs]
    pc = pc_ref[...].reshape(n_rows, 4 * D_MODEL)
    raw = lb_ref[...]
    e = jnp.exp(raw - jnp.max(raw, axis=0, keepdims=True))
    sm = e / jnp.sum(e, axis=0, keepdims=True)
    lb = jnp.sum(sm[:layer + 1], axis=0, keepdims=True) - sm[0:1]
    xq = pc[:, :C_FDIM]
    q = xq * _sigmoid(xq)
    fg = lb + (1.0 - lb) * _sigmoid(pc[:, C_FDIM:2 * C_FDIM])
    k = 1.0 - fg
    logf = jnp.log(fg)
    v = pc[:, 2 * C_FDIM:2 * C_FDIM + D_MODEL]
    xg = pc[:, 2 * C_FDIM + D_MODEL:]
    if t_real < chunk:
        live = lax.broadcasted_iota(jnp.int32, (n_rows, 1), 0) % chunk < t_real
        logf = jnp.where(live, logf, 0.0)
        k = jnp.where(live, k, 0.0)
        v = jnp.where(live, v, 0.0)

    sums = _mm_exact_lhs(sums_ref[...], logf)
    cum = sums[:n_rows]
    base = sums[n_rows:]
    q_in = q * jnp.exp(cum)
    q_loc = q * jnp.exp(cum - base)
    k_loc = k * jnp.exp(jnp.minimum(base - cum, EXP_CLAMP))
    nsub = chunk // GLA_SUB
    k_stack, v_stack, k_end, g_end = [], [], [], []
    for s in seqs:
        cum_s, k_s = cum[rsl[s]], k[rsl[s]]
        cum_end = cum_s[chunk - 1:chunk, :]
        k_end.append(k_s * jnp.exp(cum_end - cum_s))
        g_end.append(jnp.exp(cum_end))
        k_var = [k_loc[rsl[s]]]
        for i in range(1, nsub):
            ref_i = cum_s[i * GLA_SUB - 1:i * GLA_SUB, :]
            k_var.append(k_s * jnp.exp(jnp.minimum(ref_i - cum_s, 0.0)))
        k_stack.append(jnp.concatenate(k_var, axis=0) if nsub > 1 else k_var[0])
        v_stack.append(jnp.concatenate([v[rsl[s]]] * nsub, axis=0) if nsub > 1 else v[rsl[s]])

    ri = lax.broadcasted_iota(jnp.int32, (chunk, nsub * chunk), 0)
    cc = lax.broadcasted_iota(jnp.int32, (chunk, nsub * chunk), 1)
    var = cc // chunk
    ci = cc % chunk
    same_sub = ci // GLA_SUB == ri // GLA_SUB
    att_mask = ((var == 0) & same_sub & (ci <= ri)) | ((ri // GLA_SUB == var) & (ci < var * GLA_SUB))

    units = [(s, h) for s in seqs for h in range(C_HEADS)]
    idx = range(len(units))
    csl = [slice(h * 128, (h + 1) * 128) for _, h in units]
    att = [jnp.where(att_mask, _mm(q_loc[rsl[s], csl[i]], k_stack[s][:, csl[i]], NT), 0.0)
           for i, (s, _) in enumerate(units)]
    s_old = [s_scr[s, h] for s, h in units]
    outs = [_mm(q_in[rsl[s], csl[i]], s_old[i], NT) + _mm(att[i], v_stack[s][:, csl[i]])
            for i, (s, _) in enumerate(units)]
    for i, (s, h) in enumerate(units):
        sl = csl[i]
        s_scr[s, h] = s_old[i] * g_end[s][:, sl] + _mm(_stack_rows_128([v[rsl[s], sl]]).T,
                                                       _stack_rows_128([k_end[s][:, sl]]))
    o = jnp.concatenate([jnp.concatenate(outs[s * C_HEADS:(s + 1) * C_HEADS], axis=-1) for s in seqs], axis=0)
    out_ref[...] = (_rms(o, gn_ref[...]) * (xg * _sigmoid(xg))).reshape(nb, chunk, D_MODEL)

    @pl.when(c == nc - 1)
    def _():
        for s in seqs:
            for h in range(C_HEADS):
                s_out_ref[s, h] = s_scr[s, h].T


def _gla(pc, s0, lb_raw, layer, gn, chunk, t_real):
    bsz, t, _ = pc.shape
    nc = t // chunk
    nb = min(bsz, 128 // chunk, MAX_SEQS_PER_STEP)
    idx = np.arange(chunk)
    eye = np.eye(nb)
    tri = np.kron(eye, np.tril(np.ones((chunk, chunk)))).astype(np.float32)
    sel = np.kron(eye, idx[None, :] < (idx[:, None] // GLA_SUB) * GLA_SUB).astype(np.float32)
    return pl.pallas_call(
        functools.partial(_gla_kernel, nb=nb, chunk=chunk, t_real=t_real, layer=layer),
        grid=(bsz // nb, nc),
        in_specs=[pl.BlockSpec((nb, chunk, 4 * D_MODEL), lambda b, c: (b, c, 0)),
                  pl.BlockSpec((nb, C_HEADS, 128, 128), lambda b, c: (b, 0, 0, 0)),
                  _const_spec(lb_raw.shape), _const_spec((1, D_MODEL)),
                  _const_spec((2 * nb * chunk, nb * chunk))],
        out_specs=[pl.BlockSpec((nb, chunk, D_MODEL), lambda b, c: (b, c, 0)),
                   pl.BlockSpec((nb, C_HEADS, 128, 128), lambda b, c: (b, 0, 0, 0))],
        out_shape=[jax.ShapeDtypeStruct((bsz, t, D_MODEL), F32),
                   jax.ShapeDtypeStruct((bsz, C_HEADS, 128, 128), F32)],
        scratch_shapes=[pltpu.VMEM((nb, C_HEADS, 128, 128), F32)],
        compiler_params=_params(2),
        name="hgrn2",
    )(pc, s0, lb_raw, gn.reshape(1, -1), jnp.asarray(np.concatenate([tri, sel]), BF16))


def _post_kernel(*refs, n_mix, final):
    x_ref = refs[0]
    mix_refs = refs[1:1 + n_mix]
    w_refs = refs[1 + n_mix:1 + 2 * n_mix]
    gm_ref, up_ref, down_ref = refs[1 + 2 * n_mix:4 + 2 * n_mix]
    gf_ref = refs[4 + 2 * n_mix] if final else None
    o_ref = refs[-1]
    x = x_ref[...]
    for m_ref, w_ref in zip(mix_refs, w_refs):
        x = x + jnp.dot(m_ref[...].astype(BF16), w_ref[...], preferred_element_type=F32)
    h = _rms(x, gm_ref[...]).astype(BF16)
    u = jnp.dot(h, up_ref[...], preferred_element_type=F32)
    u = jnp.square(jnp.maximum(u, 0.0)).astype(BF16)
    x = x + jnp.dot(u, down_ref[...], preferred_element_type=F32)
    if final:
        x = _rms(x, gf_ref[...])
    o_ref[...] = x


def _post(x, mixes, ws, gm, up, down, gf=None, tm=256):
    m = x.shape[0]
    final = gf is not None
    row_spec = lambda n: pl.BlockSpec((tm, n), lambda i: (i, 0))
    rows = [x] + list(mixes)
    args = rows + [*ws, gm.reshape(1, -1), up, down]
    specs = ([row_spec(a.shape[1]) for a in rows] + [_const_spec(w.shape) for w in ws]
             + [_const_spec((1, D_MODEL)), _const_spec(up.shape), _const_spec(down.shape)])
    if final:
        args.append(gf.reshape(1, -1))
        specs.append(_const_spec((1, D_MODEL)))
    return pl.pallas_call(
        functools.partial(_post_kernel, n_mix=len(mixes), final=final),
        grid=(m // tm,), in_specs=specs, out_specs=row_spec(D_MODEL),
        out_shape=jax.ShapeDtypeStruct((m, D_MODEL), F32), compiler_params=_params(1), name="post_mlp",
    )(*args)


def _lora_pad_cols(x):
    pad = lambda a, n: jnp.pad(a, [(0, 0)] * (a.ndim - 1) + [(0, n - a.shape[-1])])
    return jnp.concatenate([pad(x[..., :64], 128), pad(x[..., 64:128], 128), pad(x[..., 128:], 256)], axis=-1)


def _lora_unpad_cols(x):
    return jnp.concatenate([x[..., :64], x[..., 128:192], x[..., 256:416]], axis=-1)


def _prepare(rel_bias, norm_mix, norm_mlp, norm_final, e_w_in, e_mu, e_w0, e_w2, e_a0, e_a2, e_g2, e_k_k, e_k_a,
             e_r_k, e_lnx_w, e_lnx_b, e_w_out, c_w_in, c_lb_raw, c_norm, c_w_out, mlp_up, mlp_down):
    pad_rows = lambda a, n: jnp.pad(a, [(0, n - a.shape[0]), (0, 0)])
    w_in = e_w_in[0]
    return dict(
        rel_bias=rel_bias,
        w_qkv=w_in[:, :A_QKV].astype(BF16),
        w_rkv=w_in[:, A_QKV:A_QKV + 3 * B_WIDTH].astype(BF16),
        w_lora=_lora_pad_cols(w_in[:, A_QKV + 3 * B_WIDTH:]).astype(BF16),
        rwkv=dict(mu_rkv=e_mu[0, :3 * B_WIDTH], mu_lora=_lora_pad_cols(e_mu[0, 3 * B_WIDTH:]),
                  w0=e_w0[0], w2=jnp.stack(_split2(pad_rows(e_w2[0], 128))), a0=e_a0[0],
                  a2=pad_rows(e_a2[0], 128).astype(BF16), g2=pad_rows(e_g2[0], 256).astype(BF16),
                  k_k=e_k_k[0], k_a=e_k_a[0], r_k=e_r_k[0].reshape(-1),
                  lnx_w=e_lnx_w[0], lnx_b=e_lnx_b[0]),
        w_out_a=e_w_out[0, :A_OUT].astype(BF16), w_out_b=e_w_out[0, A_OUT:].astype(BF16),
        c_w_in=c_w_in[0].astype(BF16), c_w_out=c_w_out[0].astype(BF16), lb_raw=c_lb_raw, c_norm=c_norm[0],
        norm_mix=norm_mix, norm_mlp=norm_mlp, norm_final=norm_final,
        up=[w.astype(BF16) for w in mlp_up], down=[w.astype(BF16) for w in mlp_down])


def _trunk(x, prm, attend, sh_rkv, sh_lora, wkv0, c0, chunk, t_real):
    bsz, t, _ = x.shape
    flat = lambda a: a.reshape(bsz * t, a.shape[-1])
    xf = flat(x)
    qkv, rkv, lora = _norm_proj(xf, prm["norm_mix"][0], [prm["w_qkv"], prm["w_rkv"], prm["w_lora"]])
    qkv = qkv.reshape(bsz, t, A_QKV)
    rkv = rkv.reshape(bsz, t, 3 * B_WIDTH)
    lora = lora.reshape(bsz, t, LORA_PAD)
    a_out = attend(qkv)
    b_out, wkv = _rwkv(rkv, lora, sh_rkv, sh_lora, wkv0, prm["rwkv"], chunk, t_real)
    x1 = _post(xf, [flat(a_out), flat(b_out)], [prm["w_out_a"], prm["w_out_b"]], prm["norm_mlp"][0],
               prm["up"][0], prm["down"][0])
    (pc,) = _norm_proj(x1, prm["norm_mix"][1], [prm["c_w_in"]])
    c_out, c_state = _gla(pc.reshape(bsz, t, 4 * D_MODEL), c0, prm["lb_raw"], 1, prm["c_norm"], chunk, t_real)
    y = _post(x1, [flat(c_out)], [prm["c_w_out"]], prm["norm_mlp"][1], prm["up"][1], prm["down"][1],
              gf=prm["norm_final"])
    kv_rows = []
    for g, (w, _) in enumerate(A_GROUPS):
        n = min(w, t_real)
        rows = qkv[:, t_real - n:t_real]
        kv = jnp.stack([rows[..., (3 * part + g) * A_OUT:(3 * part + g + 1) * A_OUT] for part in (1, 2)], axis=2)
        kv_rows.append(kv.reshape(1, bsz, n, 2, 4, HEAD))
    shift = jnp.concatenate([rkv[:, t_real - 1], _lora_unpad_cols(lora[:, t_real - 1])], axis=-1)[None]
    return y.reshape(bsz, t, D_MODEL)[:, :t_real], kv_rows, shift, wkv[None], c_state[None]


def kernel(x_prompt, x_sample, cache_a0, cache_a1, cache_a2, state_b_shift, state_b_wkv, state_c, rel_bias, norm_mix, norm_mlp, norm_final, e_w_in, e_mu, e_w0, e_w2, e_a0, e_a2, e_g2, e_k_k, e_k_a, e_r_k, e_lnx_w, e_lnx_b, e_w_out, c_w_in, c_lb_raw, c_norm, c_w_out, mlp_up, mlp_down):
    prm = _prepare(rel_bias, norm_mix, norm_mlp, norm_final, e_w_in, e_mu, e_w0, e_w2, e_a0, e_a2, e_g2, e_k_k,
                   e_k_a, e_r_k, e_lnx_w, e_lnx_b, e_w_out, c_w_in, c_lb_raw, c_norm, c_w_out, mlp_up, mlp_down)

    bp, tp, _ = x_prompt.shape

    def attend_prompt(qkv):
        return _attn_prompt(qkv, prm["rel_bias"])

    y_p, p_kv, p_shift, p_wkv, p_c = _trunk(
        x_prompt, prm, attend_prompt,
        jnp.zeros((bp, 3 * B_WIDTH), F32), jnp.zeros((bp, LORA_PAD), F32),
        jnp.zeros((bp, 12, HEAD, HEAD), F32), jnp.zeros((bp, C_HEADS, 128, 128), F32), chunk=64, t_real=tp)

    bs, ts, _ = x_sample.shape
    t_pad = 16
    x_s = jnp.pad(x_sample, ((0, 0), (0, t_pad - ts), (0, 0)))
    caches = [jnp.transpose(c[0], (0, 2, 3, 4, 1)) for c in (cache_a0, cache_a1, cache_a2)]

    def attend_sample(qkv):
        return _attn_sample(qkv, caches, prm["rel_bias"], ts)

    sh = state_b_shift[0]
    y_s, s_kv, s_shift, s_wkv, s_c = _trunk(
        x_s, prm, attend_sample, sh[:, :3 * B_WIDTH], _lora_pad_cols(sh[:, 3 * B_WIDTH:]),
        state_b_wkv[0], state_c[0], chunk=t_pad, t_real=ts)

    return (y_p, y_s, p_kv[0], p_kv[1], p_kv[2], p_shift, p_wkv, p_c,
            s_kv[0], s_kv[1], s_kv[2], s_shift, s_wkv, s_c)
```

```python
import functools
import math

import jax
import jax.numpy as jnp
import numpy as np
from jax import lax
from jax.experimental import pallas as pl
from jax.experimental.pallas import tpu as pltpu

F32 = jnp.float32
BF16 = jnp.bfloat16

D_MODEL = 1024
D_FF = 4 * D_MODEL
EPS = 1e-6
LNX_EPS = 64e-5
HEAD = 64
A_GROUPS = ((128, 1), (512, 4), (2048, 16))
A_KEYS = 128
A_TILES = 4
A_BLOCK = A_KEYS * A_GROUPS[-1][1]
A_QKV = 2304
A_OUT = 256
N_BUCKETS = 32
BUCKET_MAX_DIST = 2048
B_WIDTH = 768
B_PAIRS = B_WIDTH // 128
LORA_PAD = 512
C_FDIM = 1024
C_HEADS = 8
NEG = -1e30
LOG2E = math.log2(math.e)
DECAY_SCALE = math.exp(-0.5)
GLA_SUB = 16
EXP_CLAMP = 80.0
MAX_SEQS_PER_STEP = 4

VMEM_LIMIT = 56 * 1024 * 1024

NN = (((1,), (0,)), ((), ()))
NT = (((1,), (1,)), ((), ()))


def _dg(a, b, dims):
    return lax.dot_general(a, b, dims, preferred_element_type=F32)


def _split2(x):
    hi = x.astype(BF16)
    lo = (x - hi.astype(F32)).astype(BF16)
    return hi, lo


def _split3(x):
    hi = x.astype(BF16)
    r1 = x - hi.astype(F32)
    mid = r1.astype(BF16)
    lo = (r1 - mid.astype(F32)).astype(BF16)
    return hi, mid, lo


def _mm(a, b, dims=NN):
    return _dg(a.astype(BF16), b.astype(BF16), dims)


def _mm_exact_lhs(a_bf16, b, terms=3):
    if terms == 2:
        bh, bl = _split2(b)
        return _dg(a_bf16, bh, NN) + _dg(a_bf16, bl, NN)
    bh, bm, bl = _split3(b)
    return _dg(a_bf16, bh, NN) + (_dg(a_bf16, bm, NN) + _dg(a_bf16, bl, NN))


def _stack_rows_128(parts):
    n = sum(p.shape[0] for p in parts)
    if n < 128:
        parts = list(parts) + [jnp.zeros((128 - n, 128), F32)]
    return jnp.concatenate(parts, axis=0) if len(parts) > 1 else parts[0]


def _stack_halves(top, bottom):
    n = top.shape[0]
    if n == 64:
        return jnp.concatenate([top, bottom], axis=0)
    pad = jnp.zeros((64 - n, 128), F32)
    return jnp.concatenate([top, pad, bottom, pad], axis=0)


def _sigmoid(x):
    return 0.5 * jnp.tanh(0.5 * x) + 0.5


def _rms(x, g):
    ms = jnp.mean(x * x, axis=-1, keepdims=True)
    return x * lax.rsqrt(ms + EPS) * g


def _const_spec(shape):
    nd = len(shape)
    return pl.BlockSpec(shape, lambda *_: (0,) * nd, pipeline_mode=pl.Buffered(1))


def _params(n_grid):
    return pltpu.CompilerParams(dimension_semantics=("arbitrary",) * n_grid,
                                vmem_limit_bytes=VMEM_LIMIT)


def _norm_proj_kernel(x_ref, g_ref, *refs):
    n = len(refs) // 2
    h = _rms(x_ref[...], g_ref[...]).astype(BF16)
    for w_ref, o_ref in zip(refs[:n], refs[n:]):
        o_ref[...] = jnp.dot(h, w_ref[...], preferred_element_type=F32)


def _norm_proj(x, g, ws, tm=512):
    m = x.shape[0]
    return pl.pallas_call(
        _norm_proj_kernel,
        grid=(m // tm,),
        in_specs=[pl.BlockSpec((tm, D_MODEL), lambda i: (i, 0)), _const_spec((1, D_MODEL))]
        + [pl.BlockSpec((D_MODEL, n), lambda i, j=j: (0, j), pipeline_mode=pl.Buffered(1)) for _, n, j in ws],
        out_specs=[pl.BlockSpec((tm, n), lambda i: (i, 0)) for _, n, _ in ws],
        out_shape=[jax.ShapeDtypeStruct((m, n), F32) for _, n, _ in ws],
        compiler_params=_params(1),
        name="norm_proj",
    )(x, g.reshape(1, D_MODEL), *[w for w, _, _ in ws])


def _t5_bucket_np(dist):
    max_exact = N_BUCKETS // 2
    d = np.maximum(dist, 1).astype(np.float32)
    large = max_exact + (np.log(d / np.float32(max_exact)) / np.float32(math.log(BUCKET_MAX_DIST / max_exact))
                         * np.float32(N_BUCKETS - max_exact)).astype(np.int32)
    large = np.minimum(large, N_BUCKETS - 1)
    return np.where(dist < max_exact, dist, large).astype(np.int32)


def _bias_from_buckets(idx, rb_ref, head):
    acc = jnp.full(idx.shape, NEG, F32)
    for b in range(N_BUCKETS):
        acc = jnp.where(idx == b, rb_ref[b, head], acc)
    return acc


def _tile_rows(start, dil):
    if dil > 1:
        return pl.ds(start, A_KEYS, stride=dil)
    return pl.ds(start if isinstance(start, int) else pl.multiple_of(start, A_KEYS), A_KEYS)


def _attn_prompt_kernel(rb_ref, bkt_ref, *refs):
    ng = len(A_GROUPS)
    q_refs, kc_refs, vc_refs, kp_refs, vp_refs = (refs[i * ng:(i + 1) * ng] for i in range(5))
    out_ref, bias_scr, o_scr, m_scr, d_scr = refs[5 * ng:]
    pair = pl.program_id(2)
    first = (pl.program_id(0) == 0) & (pl.program_id(1) == 0) & (pair == 0)

    @pl.when(first)
    def _():
        for g in range(ng):
            for h in range(4):
                bias_scr[g * 4 + h] = _bias_from_buckets(bkt_ref[g], rb_ref, g * 4 + h) * LOG2E

    col = lax.broadcasted_iota(jnp.int32, (1, 2 * A_KEYS), 1)
    prev_pen = jnp.where((col < A_KEYS) & (pl.program_id(1) == 0), NEG, 0.0).astype(F32)
    h1 = lax.broadcasted_iota(jnp.int32, (1, 128), 1) < HEAD
    h2 = jnp.logical_not(h1)

    def attend(g, tiles):
        dil = A_GROUPS[g][1]
        n = range(len(tiles))
        rows = [_tile_rows(start, dil) for start, _, _ in tiles]
        prev = [((kp_refs[g], vp_refs[g]) if edge else (kc_refs[g], vc_refs[g]), _tile_rows(pstart, dil))
                for _, edge, pstart in tiles]
        bias_in = [bias_scr[g * 4 + 2 * pair + j] for j in range(2)]
        bias_edge = [b + prev_pen for b in bias_in]
        q = [q_refs[g][r, :] * (HEAD ** -0.5 * LOG2E) for r in rows]
        k = [jnp.concatenate([prev[i][0][0][prev[i][1], :], kc_refs[g][rows[i], :]], axis=0).astype(BF16)
             for i in n]
        v = [jnp.concatenate([prev[i][0][1][prev[i][1], :], vc_refs[g][rows[i], :]], axis=0).astype(BF16)
             for i in n]
        s = [[_dg(jnp.where(mine, q[i], 0.0).astype(BF16), k[i], NT) + (bias_edge if tiles[i][1] else bias_in)[j]
              for j, mine in enumerate((h1, h2))] for i in n]
        m = [[jnp.max(x, axis=-1, keepdims=True) for x in s[i]] for i in n]
        p = [[jnp.exp2(x - mx) for x, mx in zip(s[i], m[i])] for i in n]
        den = [[jnp.sum(x, axis=-1, keepdims=True) for x in p[i]] for i in n]
        o = [[_dg(x.astype(BF16), v[i], NN) for x in p[i]] for i in n]
        for i in n:
            o_scr[g, rows[i], :] = jnp.where(h1, o[i][0], o[i][1])
            m_scr[g, rows[i], :] = jnp.where(h1, m[i][0], m[i][1])
            d_scr[g, rows[i], :] = jnp.where(h1, den[i][0], den[i][1])

    n_tiles = A_BLOCK // A_KEYS
    for g, (_, dil) in enumerate(A_GROUPS):
        span = A_KEYS * dil
        single_span = span == A_BLOCK

        def batch(it, static, g=g, dil=dil, span=span, single_span=single_span):
            tiles = []
            for u in range(A_TILES):
                t = it * A_TILES + u
                s_idx, r = (t // dil, t % dil) if static else _span_residue(it, u, dil)
                edge = single_span or (static and s_idx == 0)
                tiles.append((s_idx * span + r, edge, r if edge else (s_idx - 1) * span + r))
            attend(g, tiles)

        batch(0, True)

        def body(it, carry, batch=batch):
            batch(it, False)
            return carry
        lax.fori_loop(1, n_tiles // A_TILES, body, 0)

    def merge(i, carry):
        rows = pl.ds(pl.multiple_of(i * A_KEYS, A_KEYS), A_KEYS)
        ms = [m_scr[g, rows, :] for g in range(ng)]
        top = functools.reduce(jnp.maximum, ms)
        ws = [jnp.exp2(x - top) for x in ms]
        num = functools.reduce(lambda a, b: a + b, [w * o_scr[g, rows, :] for g, w in enumerate(ws)])
        den = functools.reduce(lambda a, b: a + b, [w * d_scr[g, rows, :] for g, w in enumerate(ws)])
        out_ref[rows, :] = num / den
        return carry
    lax.fori_loop(0, n_tiles, merge, 0)


def _span_residue(it, u, dil):
    if dil >= A_TILES:
        per = dil // A_TILES
        return it // per, (it % per) * A_TILES + u
    return it * (A_TILES // dil) + u // dil, u % dil


def _prompt_bucket_map(dil):
    qi = np.arange(A_KEYS)[:, None]
    ki = np.arange(2 * A_KEYS)[None, :]
    j = qi + A_KEYS - ki
    return np.where((j >= 0) & (j <= A_KEYS), _t5_bucket_np(np.clip(j, 0, A_KEYS) * dil), -1).astype(np.int32)


def _attn_prompt(qkv, rel_bias):
    bsz, t, _ = qkv.shape
    ng = len(A_GROUPS)

    def cur(part, g):
        return pl.BlockSpec((None, A_BLOCK, 128), lambda b, i, pair: (b, i, (part * ng + g) * 2 + pair))

    def before(part, g):
        span = A_KEYS * A_GROUPS[g][1]
        per = A_BLOCK // span
        return pl.BlockSpec((None, span, 128),
                            lambda b, i, pair: (b, jnp.maximum(i * per - 1, 0), (part * ng + g) * 2 + pair))

    specs = [cur(part, g) for part in range(3) for g in range(ng)] + \
            [before(part, g) for part in (1, 2) for g in range(ng)]
    maps = jnp.asarray(np.stack([_prompt_bucket_map(d) for _, d in A_GROUPS]))
    return pl.pallas_call(
        _attn_prompt_kernel,
        grid=(bsz, t // A_BLOCK, 2),
        in_specs=[pl.BlockSpec(memory_space=pltpu.SMEM), _const_spec(maps.shape)] + specs,
        out_specs=pl.BlockSpec((None, A_BLOCK, 128), lambda b, i, pair: (b, i, pair)),
        out_shape=jax.ShapeDtypeStruct((bsz, t, A_OUT), F32),
        scratch_shapes=[pltpu.VMEM((4 * ng, A_KEYS, 2 * A_KEYS), F32)] + [pltpu.VMEM((ng, A_BLOCK, 128), F32)] * 3,
        compiler_params=_params(3),
        name="attn_prompt",
    )(rel_bias, maps, *([qkv] * (5 * ng)))


def _merge_groups(outs, lses):
    m = functools.reduce(jnp.maximum, lses)
    ws = [jnp.exp(l - m) for l in lses]
    num = functools.reduce(lambda a, b: a + b, [w * o for w, o in zip(ws, outs)])
    return num / functools.reduce(lambda a, b: a + b, ws)


def _attn_sample_kernel(rb_ref, bc0, bc1, bc2, bn0, bn1, bn2, qkv_ref, c0_ref, c1_ref, c2_ref, out_ref,
                        bias_c0, bias_c1, bias_c2, bias_n):
    bias_c = (bias_c0, bias_c1, bias_c2)

    @pl.when(pl.program_id(0) == 0)
    def _():
        for g, (bc, bn) in enumerate(((bc0, bn0), (bc1, bn1), (bc2, bn2))):
            for h in range(4):
                bias_c[g][h] = _bias_from_buckets(bc[...], rb_ref, g * 4 + h)
                bias_n[g * 4 + h] = _bias_from_buckets(bn[...], rb_ref, g * 4 + h)

    qkv = qkv_ref[...]
    c_refs = (c0_ref, c1_ref, c2_ref)
    units = [(g, h) for g in range(len(A_GROUPS)) for h in range(4)]
    cols = [(g * 4 + h) * HEAD for g, h in units]
    q = [(qkv[:, c:c + HEAD] * (HEAD ** -0.5)).astype(BF16) for c in cols]
    kn = [qkv[:, 3 * A_OUT + c:3 * A_OUT + c + HEAD].astype(BF16) for c in cols]
    vn = [qkv[:, 6 * A_OUT + c:6 * A_OUT + c + HEAD].astype(BF16) for c in cols]
    n = range(len(units))
    s_c = [_dg(q[i], c_refs[g][0, h].astype(BF16), NN) + bias_c[g][h] for i, (g, h) in enumerate(units)]
    s_n = [_dg(q[i], kn[i], NT) + bias_n[i] for i in n]
    m = [jnp.maximum(jnp.max(s_c[i], axis=-1, keepdims=True), jnp.max(s_n[i], axis=-1, keepdims=True)) for i in n]
    p_c = [jnp.exp(s_c[i] - m[i]) for i in n]
    p_n = [jnp.exp(s_n[i] - m[i]) for i in n]
    den = [jnp.sum(p_c[i], axis=-1, keepdims=True) + jnp.sum(p_n[i], axis=-1, keepdims=True) for i in n]
    o = [(_dg(p_c[i].astype(BF16), c_refs[g][1, h].astype(BF16), NT) + _dg(p_n[i].astype(BF16), vn[i], NN)) / den[i]
         for i, (g, h) in enumerate(units)]
    lse = [m[i] + jnp.log(den[i]) for i in n]
    out_ref[...] = jnp.concatenate([_merge_groups(o[h::4], lse[h::4]) for h in range(4)], axis=-1)


def _sample_bucket_maps(window, dil, tp, t_real):
    t = np.arange(tp)[:, None]
    dist_c = window + t - np.arange(window)[None, :]
    ok_c = (dist_c % dil == 0) & (dist_c // dil <= A_KEYS) & (t < t_real)
    dist_n = t - np.arange(tp)[None, :]
    ok_n = (dist_n >= 0) & (dist_n % dil == 0) & (dist_n // dil <= A_KEYS)
    mc = np.where(ok_c, _t5_bucket_np(np.maximum(dist_c, 0)), -1).astype(np.int32)
    mn = np.where(ok_n, _t5_bucket_np(np.maximum(dist_n, 0)), -1).astype(np.int32)
    return mc, mn


def _attn_sample(qkv, caches, rel_bias, t_real):
    bsz, tp, _ = qkv.shape
    maps = [_sample_bucket_maps(w, d, tp, t_real) for w, d in A_GROUPS]
    mcs = [jnp.asarray(m[0]) for m in maps]
    mns = [jnp.asarray(m[1]) for m in maps]
    return pl.pallas_call(
        _attn_sample_kernel,
        grid=(bsz,),
        in_specs=[pl.BlockSpec(memory_space=pltpu.SMEM)]
        + [_const_spec(m.shape) for m in mcs] + [_const_spec(m.shape) for m in mns]
        + [pl.BlockSpec((None, tp, A_QKV), lambda b: (b, 0, 0))]
        + [pl.BlockSpec((None, 2, 4, HEAD, w), lambda b: (b, 0, 0, 0, 0)) for w, _ in A_GROUPS],
        out_specs=pl.BlockSpec((None, tp, A_OUT), lambda b: (b, 0, 0)),
        out_shape=jax.ShapeDtypeStruct((bsz, tp, A_OUT), F32),
        scratch_shapes=[pltpu.VMEM((4, tp, w), F32) for w, _ in A_GROUPS] + [pltpu.VMEM((12, tp, tp), F32)],
        compiler_params=_params(1),
        name="attn_sample",
    )(rel_bias, *mcs, *mns, qkv, *caches)


def _head_sums(x, h1, fn=lambda s: s):
    first = jnp.sum(jnp.where(h1, x, 0.0), axis=-1, keepdims=True)
    second = jnp.sum(jnp.where(h1, 0.0, x), axis=-1, keepdims=True)
    return jnp.where(h1, fn(first), fn(second))


def _rwkv_kernel(rkv_ref, lora_ref, sh_rkv_ref, sh_lora_ref, s0_ref, mu_rkv_ref, mu_lora_ref, w0_ref, w2_ref,
                 a0_ref, a2_ref, g2_ref, kk_ref, ka_ref, rk_ref, lnw_ref, lnb_ref, tri_ref,
                 out_ref, s_out_ref, s_scr, prev_rkv, prev_lora, *, nb, chunk, t_real):
    c = pl.program_id(1)
    nc = pl.num_programs(1)
    seqs = range(nb)

    @pl.when(c == 0)
    def _():
        zero_blk = jnp.zeros((HEAD, HEAD), F32)
        for s in seqs:
            for p in range(B_PAIRS):
                s_scr[s, p] = jnp.concatenate([jnp.concatenate([s0_ref[s, 2 * p], zero_blk], axis=1),
                                               jnp.concatenate([zero_blk, s0_ref[s, 2 * p + 1]], axis=1)], axis=0)
        prev_rkv[...] = sh_rkv_ref[...]
        prev_lora[...] = sh_lora_ref[...]

    n_rows = nb * chunk
    rsl = [slice(s * chunk, (s + 1) * chunk) for s in seqs]
    rows = lax.broadcasted_iota(jnp.int32, (n_rows, 1), 0)
    pb = rkv_ref[...].reshape(n_rows, 3 * B_WIDTH)
    lr = lora_ref[...].reshape(n_rows, LORA_PAD)
    pb_prev = pltpu.roll(pb, 1, 0)
    lr_prev = pltpu.roll(lr, 1, 0)
    for s in seqs:
        at = rows == s * chunk
        pb_prev = jnp.where(at, prev_rkv[s], pb_prev)
        lr_prev = jnp.where(at, prev_lora[s], lr_prev)
        prev_rkv[s] = pb[(s + 1) * chunk - 1:(s + 1) * chunk, :]
        prev_lora[s] = lr[(s + 1) * chunk - 1:(s + 1) * chunk, :]
    xs = pb + (pb_prev - pb) * mu_rkv_ref[...]
    xl = lr + (lr_prev - lr) * mu_lora_ref[...]
    r = xs[:, :B_WIDTH]
    k = xs[:, B_WIDTH:2 * B_WIDTH]
    v = xs[:, 2 * B_WIDTH:]
    th_hi, th_lo = _split2(jnp.tanh(xl[:, :128]))
    z = w0_ref[...] + (_dg(th_hi, w2_ref[0], NN) + (_dg(th_hi, w2_ref[1], NN) + _dg(th_lo, w2_ref[0], NN)))
    lam = -DECAY_SCALE * _sigmoid(z)
    a = _sigmoid(a0_ref[...] + _mm(xl[:, 128:256], a2_ref[...]))
    gate = _mm(_sigmoid(xl[:, 256:]), g2_ref[...])
    kk = k * kk_ref[...]
    kmod = k * (1.0 + (a - 1.0) * ka_ref[...])
    if t_real < chunk:
        live = rows % chunk < t_real
        lam = jnp.where(live, lam, 0.0)
        kk = jnp.where(live, kk, 0.0)
        kmod = jnp.where(live, kmod, 0.0)
        v = jnp.where(live, v, 0.0)

    cum = _mm_exact_lhs(tri_ref[...], lam, terms=2)
    cum_end = [cum[(s + 1) * chunk - 1:(s + 1) * chunk, :] for s in seqs]
    e_in = jnp.exp(cum)
    e_prev = jnp.exp(cum - lam)
    e_neg = jnp.exp(-cum)
    e_end = [jnp.exp(cum_end[s] - cum[rsl[s]]) for s in seqs]
    g_end = [jnp.exp(cum_end[s]) for s in seqs]

    ri = lax.broadcasted_iota(jnp.int32, (chunk, 128), 0)
    ci = lax.broadcasted_iota(jnp.int32, (chunk, 128), 1) % HEAD
    strict = ci < ri
    lower = ci <= ri
    lane = lax.broadcasted_iota(jnp.int32, (1, 128), 1)
    h1 = lane < HEAD
    bi = lax.broadcasted_iota(jnp.int32, (128, 128), 0) // HEAD
    bj = lax.broadcasted_iota(jnp.int32, (128, 128), 1) // HEAD
    block_diag = bi == bj
    levels = int(math.log2(chunk))
    zero = jnp.zeros((chunk, 128), F32)
    units = [(s, p) for s in seqs for p in range(B_PAIRS)]
    idx = range(len(units))
    csl = [slice(p * 128, (p + 1) * 128) for _, p in units]
    take = lambda arr: [arr[rsl[s], csl[i]] for i, (s, _) in enumerate(units)]

    kappa = [x * _head_sums(x * x, h1, lambda q: lax.rsqrt(jnp.maximum(q, 1e-24))) for x in take(kk)]
    a_u, r_u, k_u, v_u = take(a), take(r), take(kmod), take(v)
    e_neg_u = take(e_neg)
    bb = [kappa[i] * a_u[i] for i in idx]
    a_t = [-x * e for x, e in zip(kappa, take(e_prev))]
    r_t = [x * e for x, e in zip(r_u, take(e_in))]
    m4 = []
    for i in idx:
        l4 = jnp.concatenate([jnp.where(h1, a_t[i], zero), jnp.where(h1, zero, a_t[i]),
                              jnp.where(h1, r_t[i], zero), jnp.where(h1, zero, r_t[i])], axis=0)
        m4.append(_mm(l4, _stack_halves(bb[i] * e_neg_u[i], k_u[i] * e_neg_u[i]), NT))
    na = [[jnp.where(strict, m4[i][j * chunk:(j + 1) * chunk], 0.0) for j in range(2)] for i in idx]
    nr = [[jnp.where(lower, m4[i][(2 + j) * chunk:(3 + j) * chunk], 0.0) for j in range(2)] for i in idx]
    zv = [_stack_halves(zero, pltpu.roll(v_u[i], HEAD, 1)).astype(BF16) for i in idx]
    zs = [[jnp.where(h1, a_t[i], _mm(na[i][0], zv[i])), jnp.where(h1, _mm(na[i][1], zv[i]), a_t[i])] for i in idx]
    ps = [[na[i][j][:, :chunk].astype(BF16) for j in range(2)] for i in idx]
    for lvl in range(levels):
        last = lvl == levels - 1
        for i in idx:
            for j in range(2):
                z_b = zs[i][j].astype(BF16)
                upd = _dg(ps[i][j], z_b if last else jnp.concatenate([z_b, ps[i][j]], axis=-1), NN)
                zs[i][j] = zs[i][j] + upd[:, :128]
                if not last:
                    ps[i][j] = upd[:, 128:].astype(BF16)
    ta = [jnp.where(h1, zs[i][0], zs[i][1]) for i in idx]
    pv = [pltpu.roll(jnp.where(h1, zs[i][1], zs[i][0]), HEAD, 1) for i in idx]
    s_old = [s_scr[s, p] for s, p in units]
    s_b = [x.astype(BF16) for x in s_old]
    u = [_mm(ta[i], s_b[i], NT) + pv[i] for i in idx]
    uv = [_stack_halves(u[i], v_u[i]) for i in idx]
    uv_b = [x.astype(BF16) for x in uv]
    y = [_mm(r_t[i], s_b[i], NT) + jnp.where(h1, _mm(nr[i][0], uv_b[i]), _mm(nr[i][1], uv_b[i])) for i in idx]
    for i, (s, p) in enumerate(units):
        e = e_end[s][:, csl[i]]
        s_new = s_old[i] * g_end[s][:, csl[i]] + _mm(uv[i].T, _stack_halves(bb[i] * e, k_u[i] * e))
        s_scr[s, p] = jnp.where(block_diag, s_new, 0.0)

    mean = [_head_sums(y[i], h1) * (1.0 / HEAD) for i in idx]
    dlt = [y[i] - mean[i] for i in idx]
    var = [_head_sums(d * d, h1) * (1.0 / HEAD) for d in dlt]
    gate_u = take(gate)
    for i, (s, p) in enumerate(units):
        sl = csl[i]
        bonus = _head_sums(r_u[i] * k_u[i] * rk_ref[:, sl], h1)
        yn = dlt[i] * lax.rsqrt(var[i] + LNX_EPS) * lnw_ref[:, sl] + lnb_ref[:, sl]
        out_ref[s, :, sl] = (yn + bonus * v_u[i]) * gate_u[i]

    @pl.when(c == nc - 1)
    def _():
        for s in seqs:
            for p in range(B_PAIRS):
                s_pair = s_scr[s, p]
                s_out_ref[s, 2 * p] = s_pair[:HEAD, :HEAD]
                s_out_ref[s, 2 * p + 1] = s_pair[HEAD:, HEAD:]


def _rwkv(rkv, lora, sh_rkv, sh_lora, s0, prm, chunk, t_real):
    bsz, t, _ = rkv.shape
    nc = t // chunk
    nb = min(bsz, MAX_SEQS_PER_STEP)
    tri = np.kron(np.eye(nb), np.tril(np.ones((chunk, chunk)))).astype(np.float32)
    vec = lambda n: _const_spec((1, n))
    row = lambda x: x.reshape(1, -1)
    out, s_out = pl.pallas_call(
        functools.partial(_rwkv_kernel, nb=nb, chunk=chunk, t_real=t_real),
        grid=(bsz // nb, nc),
        in_specs=[pl.BlockSpec((nb, chunk, 3 * B_WIDTH), lambda b, c: (b, c, 0)),
                  pl.BlockSpec((nb, chunk, LORA_PAD), lambda b, c: (b, c, 0)),
                  pl.BlockSpec((nb, 1, 3 * B_WIDTH), lambda b, c: (b, 0, 0)),
                  pl.BlockSpec((nb, 1, LORA_PAD), lambda b, c: (b, 0, 0)),
                  pl.BlockSpec((nb, 2 * B_PAIRS, HEAD, HEAD), lambda b, c: (b, 0, 0, 0)),
                  vec(3 * B_WIDTH), vec(LORA_PAD), vec(B_WIDTH), _const_spec((2, 128, B_WIDTH)),
                  vec(B_WIDTH), _const_spec((128, B_WIDTH)), _const_spec((256, B_WIDTH)),
                  vec(B_WIDTH), vec(B_WIDTH), vec(B_WIDTH), vec(B_WIDTH), vec(B_WIDTH),
                  _const_spec(tri.shape)],
        out_specs=[pl.BlockSpec((nb, chunk, B_WIDTH), lambda b, c: (b, c, 0)),
                   pl.BlockSpec((nb, 2 * B_PAIRS, HEAD, HEAD), lambda b, c: (b, 0, 0, 0))],
        out_shape=[jax.ShapeDtypeStruct((bsz, t, B_WIDTH), F32),
                   jax.ShapeDtypeStruct((bsz, 2 * B_PAIRS, HEAD, HEAD), F32)],
        scratch_shapes=[pltpu.VMEM((nb, B_PAIRS, 128, 128), F32), pltpu.VMEM((nb, 1, 3 * B_WIDTH), F32),
                        pltpu.VMEM((nb, 1, LORA_PAD), F32)],
        compiler_params=_params(2),
        name="rwkv7",
    )(rkv, lora, sh_rkv[:, None], sh_lora[:, None], s0,
      row(prm["mu_rkv"]), row(prm["mu_lora"]), row(prm["w0"]), prm["w2"], row(prm["a0"]), prm["a2"], prm["g2"],
      row(prm["k_k"]), row(prm["k_a"]), row(prm["r_k"]), row(prm["lnx_w"]), row(prm["lnx_b"]),
      jnp.asarray(tri, BF16))
    return out, s_out


def _gla_kernel(pc_ref, s0_ref, lb_ref, gn_ref, sums_ref, out_ref, s_out_ref, s_scr, *, nb, chunk, t_real, layer):
    c = pl.program_id(1)
    nc = pl.num_programs(1)
    seqs = range(nb)

    @pl.when(c == 0)
    def _():
        for s in seqs:
            for h in range(C_HEADS):
                s_scr[s, h] = s0_ref[s, h].T

    n_rows = nb * chunk
    rsl = [slice(s * chunk, (s + 1) * chunk) for s in seqs]
    pc = pc_ref[...].reshape(n_rows, 4 * D_MODEL)
    raw = lb_ref[...]
    e = jnp.exp(raw - jnp.max(raw, axis=0, keepdims=True))
    sm = e / jnp.sum(e, axis=0, keepdims=True)
    lb = jnp.sum(sm[:layer + 1], axis=0, keepdims=True) - sm[0:1]
    xq = pc[:, :C_FDIM]
    q = xq * _sigmoid(xq)
    fg = lb + (1.0 - lb) * _sigmoid(pc[:, C_FDIM:2 * C_FDIM])
    k = 1.0 - fg
    logf = jnp.log(fg)
    v = pc[:, 2 * C_FDIM:2 * C_FDIM + D_MODEL]
    xg = pc[:, 2 * C_FDIM + D_MODEL:]
    if t_real < chunk:
        live = lax.broadcasted_iota(jnp.int32, (n_rows, 1), 0) % chunk < t_real
        logf = jnp.where(live, logf, 0.0)
        k = jnp.where(live, k, 0.0)
        v = jnp.where(live, v, 0.0)

    sums = _mm_exact_lhs(sums_ref[...], logf)
    cum = sums[:n_rows]
    base = sums[n_rows:]
    q_in = q * jnp.exp(cum)
    q_loc = q * jnp.exp(cum - base)
    k_loc = k * jnp.exp(jnp.minimum(base - cum, EXP_CLAMP))
    nsub = chunk // GLA_SUB
    k_stack, v_stack, k_end, g_end = [], [], [], []
    for s in seqs:
        cum_s, k_s = cum[rsl[s]], k[rsl[s]]
        cum_end = cum_s[chunk - 1:chunk, :]
        k_end.append(k_s * jnp.exp(cum_end - cum_s))
        g_end.append(jnp.exp(cum_end))
        k_var = [k_loc[rsl[s]]]
        for i in range(1, nsub):
            ref_i = cum_s[i * GLA_SUB - 1:i * GLA_SUB, :]
            k_var.append(k_s * jnp.exp(jnp.minimum(ref_i - cum_s, 0.0)))
        k_stack.append(jnp.concatenate(k_var, axis=0) if nsub > 1 else k_var[0])
        v_stack.append(jnp.concatenate([v[rsl[s]]] * nsub, axis=0) if nsub > 1 else v[rsl[s]])

    ri = lax.broadcasted_iota(jnp.int32, (chunk, nsub * chunk), 0)
    cc = lax.broadcasted_iota(jnp.int32, (chunk, nsub * chunk), 1)
    var = cc // chunk
    ci = cc % chunk
    same_sub = ci // GLA_SUB == ri // GLA_SUB
    att_mask = ((var == 0) & same_sub & (ci <= ri)) | ((ri // GLA_SUB == var) & (ci < var * GLA_SUB))

    units = [(s, h) for s in seqs for h in range(C_HEADS)]
    idx = range(len(units))
    csl = [slice(h * 128, (h + 1) * 128) for _, h in units]
    att = [jnp.where(att_mask, _mm(q_loc[rsl[s], csl[i]], k_stack[s][:, csl[i]], NT), 0.0)
           for i, (s, _) in enumerate(units)]
    s_old = [s_scr[s, h] for s, h in units]
    outs = [_mm(q_in[rsl[s], csl[i]], s_old[i], NT) + _mm(att[i], v_stack[s][:, csl[i]])
            for i, (s, _) in enumerate(units)]
    for i, (s, h) in enumerate(units):
        sl = csl[i]
        s_scr[s, h] = s_old[i] * g_end[s][:, sl] + _mm(_stack_rows_128([v[rsl[s], sl]]).T,
                                                       _stack_rows_128([k_end[s][:, sl]]))
    o = jnp.concatenate([jnp.concatenate(outs[s * C_HEADS:(s + 1) * C_HEADS], axis=-1) for s in seqs], axis=0)
    out_ref[...] = (_rms(o, gn_ref[...]) * (xg * _sigmoid(xg))).reshape(nb, chunk, D_MODEL)

    @pl.when(c == nc - 1)
    def _():
        for s in seqs:
            for h in range(C_HEADS):
                s_out_ref[s, h] = s_scr[s, h].T


def _gla(pc, s0, lb_raw, layer, gn, chunk, t_real):
    bsz, t, _ = pc.shape
    nc = t // chunk
    nb = min(bsz, MAX_SEQS_PER_STEP)
    idx = np.arange(chunk)
    eye = np.eye(nb)
    tri = np.kron(eye, np.tril(np.ones((chunk, chunk)))).astype(np.float32)
    sel = np.kron(eye, idx[None, :] < (idx[:, None] // GLA_SUB) * GLA_SUB).astype(np.float32)
    return pl.pallas_call(
        functools.partial(_gla_kernel, nb=nb, chunk=chunk, t_real=t_real, layer=layer),
        grid=(bsz // nb, nc),
        in_specs=[pl.BlockSpec((nb, chunk, 4 * D_MODEL), lambda b, c: (b, c, 0)),
                  pl.BlockSpec((nb, C_HEADS, 128, 128), lambda b, c: (b, 0, 0, 0)),
                  _const_spec(lb_raw.shape), _const_spec((1, D_MODEL)),
                  _const_spec((2 * nb * chunk, nb * chunk))],
        out_specs=[pl.BlockSpec((nb, chunk, D_MODEL), lambda b, c: (b, c, 0)),
                   pl.BlockSpec((nb, C_HEADS, 128, 128), lambda b, c: (b, 0, 0, 0))],
        out_shape=[jax.ShapeDtypeStruct((bsz, t, D_MODEL), F32),
                   jax.ShapeDtypeStruct((bsz, C_HEADS, 128, 128), F32)],
        scratch_shapes=[pltpu.VMEM((nb, C_HEADS, 128, 128), F32)],
        compiler_params=_params(2),
        name="hgrn2",
    )(pc, s0, lb_raw, gn.reshape(1, -1), jnp.asarray(np.concatenate([tri, sel]), BF16))


def _post_kernel(*refs, n_mix, final):
    x_ref = refs[0]
    mix_refs = refs[1:1 + n_mix]
    w_refs = refs[1 + n_mix:1 + 2 * n_mix]
    gm_ref, up_ref, down_ref = refs[1 + 2 * n_mix:4 + 2 * n_mix]
    gf_ref = refs[4 + 2 * n_mix] if final else None
    o_ref = refs[-1]
    x = x_ref[...]
    for m_ref, w_ref in zip(mix_refs, w_refs):
        x = x + jnp.dot(m_ref[...].astype(BF16), w_ref[...], preferred_element_type=F32)
    h = _rms(x, gm_ref[...]).astype(BF16)
    u = jnp.dot(h, up_ref[...], preferred_element_type=F32)
    u = jnp.square(jnp.maximum(u, 0.0)).astype(BF16)
    x = x + jnp.dot(u, down_ref[...], preferred_element_type=F32)
    if final:
        x = _rms(x, gf_ref[...])
    o_ref[...] = x


def _post(x, mixes, ws, gm, up, down, layer, gf=None, tm=512):
    m = x.shape[0]
    final = gf is not None
    row_spec = lambda n: pl.BlockSpec((tm, n), lambda i: (i, 0))
    layer_spec = lambda w: pl.BlockSpec((None,) + w.shape[1:], lambda i: (layer, 0, 0),
                                        pipeline_mode=pl.Buffered(1))
    rows = [x] + list(mixes)
    args = rows + [*ws, gm.reshape(1, -1), up, down]
    specs = ([row_spec(a.shape[1]) for a in rows] + [_const_spec(w.shape) for w in ws]
             + [_const_spec((1, D_MODEL)), layer_spec(up), layer_spec(down)])
    if final:
        args.append(gf.reshape(1, -1))
        specs.append(_const_spec((1, D_MODEL)))
    return pl.pallas_call(
        functools.partial(_post_kernel, n_mix=len(mixes), final=final),
        grid=(m // tm,), in_specs=specs, out_specs=row_spec(D_MODEL),
        out_shape=jax.ShapeDtypeStruct((m, D_MODEL), F32), compiler_params=_params(1), name="post_mlp",
    )(*args)


def _lora_pad_cols(x):
    pad = lambda a, n: jnp.pad(a, [(0, 0)] * (a.ndim - 1) + [(0, n - a.shape[-1])])
    return jnp.concatenate([pad(x[..., :64], 128), pad(x[..., 64:128], 128), pad(x[..., 128:], 256)], axis=-1)


def _lora_unpad_cols(x):
    return jnp.concatenate([x[..., :64], x[..., 128:192], x[..., 256:416]], axis=-1)


def _prepare(rel_bias, norm_mix, norm_mlp, norm_final, e_w_in, e_mu, e_w0, e_w2, e_a0, e_a2, e_g2, e_k_k, e_k_a,
             e_r_k, e_lnx_w, e_lnx_b, e_w_out, c_w_in, c_lb_raw, c_norm, c_w_out, mlp_up, mlp_down):
    pad_rows = lambda a, n: jnp.pad(a, [(0, n - a.shape[0]), (0, 0)])
    w_in = e_w_in[0]
    return dict(
        rel_bias=rel_bias,
        w_in=w_in.astype(BF16), w_lora=_lora_pad_cols(w_in[:, A_QKV + 3 * B_WIDTH:]).astype(BF16),
        rwkv=dict(mu_rkv=e_mu[0, :3 * B_WIDTH], mu_lora=_lora_pad_cols(e_mu[0, 3 * B_WIDTH:]),
                  w0=e_w0[0], w2=jnp.stack(_split2(pad_rows(e_w2[0], 128))), a0=e_a0[0],
                  a2=pad_rows(e_a2[0], 128).astype(BF16), g2=pad_rows(e_g2[0], 256).astype(BF16),
                  k_k=e_k_k[0], k_a=e_k_a[0], r_k=e_r_k[0].reshape(-1),
                  lnx_w=e_lnx_w[0], lnx_b=e_lnx_b[0]),
        w_out_a=e_w_out[0, :A_OUT].astype(BF16), w_out_b=e_w_out[0, A_OUT:].astype(BF16),
        c_w_in=c_w_in[0].astype(BF16), c_w_out=c_w_out[0].astype(BF16), lb_raw=c_lb_raw, c_norm=c_norm[0],
        norm_mix=norm_mix, norm_mlp=norm_mlp, norm_final=norm_final,
        up=mlp_up.astype(BF16), down=mlp_down.astype(BF16))


def _trunk(x, prm, attend, sh_rkv, sh_lora, wkv0, c0, chunk, t_real):
    bsz, t, _ = x.shape
    flat = lambda a: a.reshape(bsz * t, a.shape[-1])
    xf = flat(x)
    assert A_QKV == 3 * B_WIDTH
    qkv, rkv, lora = _norm_proj(xf, prm["norm_mix"][0],
                                [(prm["w_in"], A_QKV, 0), (prm["w_in"], A_QKV, 1), (prm["w_lora"], LORA_PAD, 0)])
    qkv = qkv.reshape(bsz, t, A_QKV)
    rkv = rkv.reshape(bsz, t, 3 * B_WIDTH)
    lora = lora.reshape(bsz, t, LORA_PAD)
    a_out = attend(qkv)
    b_out, wkv = _rwkv(rkv, lora, sh_rkv, sh_lora, wkv0, prm["rwkv"], chunk, t_real)
    x1 = _post(xf, [flat(a_out), flat(b_out)], [prm["w_out_a"], prm["w_out_b"]], prm["norm_mlp"][0],
               prm["up"], prm["down"], 0)
    (pc,) = _norm_proj(x1, prm["norm_mix"][1], [(prm["c_w_in"], 4 * D_MODEL, 0)])
    c_out, c_state = _gla(pc.reshape(bsz, t, 4 * D_MODEL), c0, prm["lb_raw"], 1, prm["c_norm"], chunk, t_real)
    y = _post(x1, [flat(c_out)], [prm["c_w_out"]], prm["norm_mlp"][1], prm["up"], prm["down"], 1,
              gf=prm["norm_final"])
    kv_rows = []
    for g, (w, _) in enumerate(A_GROUPS):
        n = min(w, t_real)
        rows = qkv[:, t_real - n:t_real]
        kv = jnp.stack([rows[..., (3 * part + g) * A_OUT:(3 * part + g + 1) * A_OUT] for part in (1, 2)], axis=2)
        kv_rows.append(kv.reshape(1, bsz, n, 2, 4, HEAD))
    shift = jnp.concatenate([rkv[:, t_real - 1], _lora_unpad_cols(lora[:, t_real - 1])], axis=-1)[None]
    return y.reshape(bsz, t, D_MODEL)[:, :t_real], kv_rows, shift, wkv[None], c_state[None]


def kernel(x_prompt, x_sample, cache_a0, cache_a1, cache_a2, state_b_shift, state_b_wkv, state_c, rel_bias, norm_mix, norm_mlp, norm_final, e_w_in, e_mu, e_w0, e_w2, e_a0, e_a2, e_g2, e_k_k, e_k_a, e_r_k, e_lnx_w, e_lnx_b, e_w_out, c_w_in, c_lb_raw, c_norm, c_w_out, mlp_up, mlp_down):
    prm = _prepare(rel_bias, norm_mix, norm_mlp, norm_final, e_w_in, e_mu, e_w0, e_w2, e_a0, e_a2, e_g2, e_k_k,
                   e_k_a, e_r_k, e_lnx_w, e_lnx_b, e_w_out, c_w_in, c_lb_raw, c_norm, c_w_out, mlp_up, mlp_down)

    bp, tp, _ = x_prompt.shape

    def attend_prompt(qkv):
        return _attn_prompt(qkv, prm["rel_bias"])

    y_p, p_kv, p_shift, p_wkv, p_c = _trunk(
        x_prompt, prm, attend_prompt,
        jnp.zeros((bp, 3 * B_WIDTH), F32), jnp.zeros((bp, LORA_PAD), F32),
        jnp.zeros((bp, 12, HEAD, HEAD), F32), jnp.zeros((bp, C_HEADS, 128, 128), F32), chunk=64, t_real=tp)

    bs, ts, _ = x_sample.shape
    t_pad = 16
    x_s = jnp.pad(x_sample, ((0, 0), (0, t_pad - ts), (0, 0)))
    caches = [jnp.transpose(c[0], (0, 2, 3, 4, 1)) for c in (cache_a0, cache_a1, cache_a2)]

    def attend_sample(qkv):
        return _attn_sample(qkv, caches, prm["rel_bias"], ts)

    sh = state_b_shift[0]
    y_s, s_kv, s_shift, s_wkv, s_c = _trunk(
        x_s, prm, attend_sample, sh[:, :3 * B_WIDTH], _lora_pad_cols(sh[:, 3 * B_WIDTH:]),
        state_b_wkv[0], state_c[0], chunk=t_pad, t_real=ts)

    return (y_p, y_s, p_kv[0], p_kv[1], p_kv[2], p_shift, p_wkv, p_c,
            s_kv[0], s_kv[1], s_kv[2], s_shift, s_wkv, s_c)
```

```python
import functools
import math

import jax
import jax.numpy as jnp
import numpy as np
from jax import lax
from jax.experimental import pallas as pl
from jax.experimental.pallas import tpu as pltpu

F32 = jnp.float32
BF16 = jnp.bfloat16

D_MODEL = 1024
D_FF = 4 * D_MODEL
EPS = 1e-6
LNX_EPS = 64e-5
HEAD = 64
A_GROUPS = ((128, 1), (512, 4), (2048, 16))
A_KEYS = 128
A_TILES = 4
A_BLOCK = A_KEYS * A_GROUPS[-1][1]
A_QKV = 2304
A_OUT = 256
N_BUCKETS = 32
BUCKET_MAX_DIST = 2048
B_WIDTH = 768
B_PAIRS = B_WIDTH // 128
LORA_PAD = 512
C_FDIM = 1024
C_HEADS = 8
NEG = -1e30
LOG2E = math.log2(math.e)
DECAY_SCALE = math.exp(-0.5)
GLA_SUB = 16
EXP_CLAMP = 80.0
MAX_SEQS_PER_STEP = 4

VMEM_LIMIT = 56 * 1024 * 1024

NN = (((1,), (0,)), ((), ()))
NT = (((1,), (1,)), ((), ()))


def _dg(a, b, dims):
    return lax.dot_general(a, b, dims, preferred_element_type=F32)


def _split2(x):
    hi = x.astype(BF16)
    lo = (x - hi.astype(F32)).astype(BF16)
    return hi, lo


def _split3(x):
    hi = x.astype(BF16)
    r1 = x - hi.astype(F32)
    mid = r1.astype(BF16)
    lo = (r1 - mid.astype(F32)).astype(BF16)
    return hi, mid, lo


def _mm(a, b, dims=NN):
    return _dg(a.astype(BF16), b.astype(BF16), dims)


def _mm_exact_lhs(a_bf16, b, terms=3):
    if terms == 2:
        bh, bl = _split2(b)
        return _dg(a_bf16, bh, NN) + _dg(a_bf16, bl, NN)
    bh, bm, bl = _split3(b)
    return _dg(a_bf16, bh, NN) + (_dg(a_bf16, bm, NN) + _dg(a_bf16, bl, NN))


def _stack_rows_128(parts):
    n = sum(p.shape[0] for p in parts)
    if n < 128:
        parts = list(parts) + [jnp.zeros((128 - n, 128), F32)]
    return jnp.concatenate(parts, axis=0) if len(parts) > 1 else parts[0]


def _stack_halves(top, bottom):
    n = top.shape[0]
    if n == 64:
        return jnp.concatenate([top, bottom], axis=0)
    pad = jnp.zeros((64 - n, 128), F32)
    return jnp.concatenate([top, pad, bottom, pad], axis=0)


def _sigmoid(x):
    return 0.5 * jnp.tanh(0.5 * x) + 0.5


def _rms(x, g):
    ms = jnp.mean(x * x, axis=-1, keepdims=True)
    return x * lax.rsqrt(ms + EPS) * g


def _const_spec(shape):
    nd = len(shape)
    return pl.BlockSpec(shape, lambda *_: (0,) * nd, pipeline_mode=pl.Buffered(1))


def _params(n_grid):
    return pltpu.CompilerParams(dimension_semantics=("arbitrary",) * n_grid,
                                vmem_limit_bytes=VMEM_LIMIT)


def _norm_proj_kernel(x_ref, g_ref, *refs):
    n = len(refs) // 2
    h = _rms(x_ref[...], g_ref[...]).astype(BF16)
    for w_ref, o_ref in zip(refs[:n], refs[n:]):
        o_ref[...] = jnp.dot(h, w_ref[...], preferred_element_type=F32)


def _norm_proj(x, g, ws, tm=512):
    m = x.shape[0]
    return pl.pallas_call(
        _norm_proj_kernel,
        grid=(m // tm,),
        in_specs=[pl.BlockSpec((tm, D_MODEL), lambda i: (i, 0)), _const_spec((1, D_MODEL))]
        + [pl.BlockSpec((D_MODEL, n), lambda i, j=j: (0, j), pipeline_mode=pl.Buffered(1)) for _, n, j in ws],
        out_specs=[pl.BlockSpec((tm, n), lambda i: (i, 0)) for _, n, _ in ws],
        out_shape=[jax.ShapeDtypeStruct((m, n), F32) for _, n, _ in ws],
        compiler_params=_params(1),
        name="norm_proj",
    )(x, g.reshape(1, D_MODEL), *[w for w, _, _ in ws])


def _t5_bucket_np(dist):
    max_exact = N_BUCKETS // 2
    d = np.maximum(dist, 1).astype(np.float32)
    large = max_exact + (np.log(d / np.float32(max_exact)) / np.float32(math.log(BUCKET_MAX_DIST / max_exact))
                         * np.float32(N_BUCKETS - max_exact)).astype(np.int32)
    large = np.minimum(large, N_BUCKETS - 1)
    return np.where(dist < max_exact, dist, large).astype(np.int32)


def _bias_from_buckets(idx, rb_ref, head):
    acc = jnp.full(idx.shape, NEG, F32)
    for b in range(N_BUCKETS):
        acc = jnp.where(idx == b, rb_ref[b, head], acc)
    return acc


def _tile_rows(start, dil):
    if dil > 1:
        return pl.ds(start, A_KEYS, stride=dil)
    return pl.ds(start if isinstance(start, int) else pl.multiple_of(start, A_KEYS), A_KEYS)


def _attn_prompt_kernel(rb_ref, bkt_ref, *refs):
    ng = len(A_GROUPS)
    q_refs, kc_refs, vc_refs, kp_refs, vp_refs = (refs[i * ng:(i + 1) * ng] for i in range(5))
    out_ref, bias_scr, o_scr, m_scr, d_scr = refs[5 * ng:]
    pair = pl.program_id(2)
    first = (pl.program_id(0) == 0) & (pl.program_id(1) == 0) & (pair == 0)

    @pl.when(first)
    def _():
        for g in range(ng):
            for h in range(4):
                bias_scr[g * 4 + h] = _bias_from_buckets(bkt_ref[g], rb_ref, g * 4 + h) * LOG2E

    col = lax.broadcasted_iota(jnp.int32, (1, 2 * A_KEYS), 1)
    prev_pen = jnp.where((col < A_KEYS) & (pl.program_id(1) == 0), NEG, 0.0).astype(F32)
    h1 = lax.broadcasted_iota(jnp.int32, (1, 128), 1) < HEAD
    h2 = jnp.logical_not(h1)

    def attend(g, tiles):
        dil = A_GROUPS[g][1]
        n = range(len(tiles))
        rows = [_tile_rows(start, dil) for start, _, _ in tiles]
        prev = [((kp_refs[g], vp_refs[g]) if edge else (kc_refs[g], vc_refs[g]), _tile_rows(pstart, dil))
                for _, edge, pstart in tiles]
        bias_in = [bias_scr[g * 4 + 2 * pair + j] for j in range(2)]
        bias_edge = [b + prev_pen for b in bias_in]
        q = [q_refs[g][r, :] * (HEAD ** -0.5 * LOG2E) for r in rows]
        k = [jnp.concatenate([prev[i][0][0][prev[i][1], :], kc_refs[g][rows[i], :]], axis=0).astype(BF16)
             for i in n]
        v = [jnp.concatenate([prev[i][0][1][prev[i][1], :], vc_refs[g][rows[i], :]], axis=0).astype(BF16)
             for i in n]
        s = [[_dg(jnp.where(mine, q[i], 0.0).astype(BF16), k[i], NT) + (bias_edge if tiles[i][1] else bias_in)[j]
              for j, mine in enumerate((h1, h2))] for i in n]
        m = [[jnp.max(x, axis=-1, keepdims=True) for x in s[i]] for i in n]
        p = [[jnp.exp2(x - mx) for x, mx in zip(s[i], m[i])] for i in n]
        den = [[jnp.sum(x, axis=-1, keepdims=True) for x in p[i]] for i in n]
        o = [[_dg(x.astype(BF16), v[i], NN) for x in p[i]] for i in n]
        for i in n:
            o_scr[g, rows[i], :] = jnp.where(h1, o[i][0], o[i][1])
            m_scr[g, rows[i], :] = jnp.where(h1, m[i][0], m[i][1])
            d_scr[g, rows[i], :] = jnp.where(h1, den[i][0], den[i][1])

    n_tiles = A_BLOCK // A_KEYS
    for g, (_, dil) in enumerate(A_GROUPS):
        span = A_KEYS * dil
        single_span = span == A_BLOCK

        def batch(it, static, g=g, dil=dil, span=span, single_span=single_span):
            tiles = []
            for u in range(A_TILES):
                t = it * A_TILES + u
                s_idx, r = (t // dil, t % dil) if static else _span_residue(it, u, dil)
                edge = single_span or (static and s_idx == 0)
                tiles.append((s_idx * span + r, edge, r if edge else (s_idx - 1) * span + r))
            attend(g, tiles)

        batch(0, True)

        def body(it, carry, batch=batch):
            batch(it, False)
            return carry
        lax.fori_loop(1, n_tiles // A_TILES, body, 0)

    def merge(i, carry):
        rows = pl.ds(pl.multiple_of(i * A_KEYS, A_KEYS), A_KEYS)
        ms = [m_scr[g, rows, :] for g in range(ng)]
        top = functools.reduce(jnp.maximum, ms)
        ws = [jnp.exp2(x - top) for x in ms]
        num = functools.reduce(lambda a, b: a + b, [w * o_scr[g, rows, :] for g, w in enumerate(ws)])
        den = functools.reduce(lambda a, b: a + b, [w * d_scr[g, rows, :] for g, w in enumerate(ws)])
        out_ref[rows, :] = num / den
        return carry
    lax.fori_loop(0, n_tiles, merge, 0)


def _span_residue(it, u, dil):
    if dil >= A_TILES:
        per = dil // A_TILES
        return it // per, (it % per) * A_TILES + u
    return it * (A_TILES // dil) + u // dil, u % dil


def _prompt_bucket_map(dil):
    qi = np.arange(A_KEYS)[:, None]
    ki = np.arange(2 * A_KEYS)[None, :]
    j = qi + A_KEYS - ki
    return np.where((j >= 0) & (j <= A_KEYS), _t5_bucket_np(np.clip(j, 0, A_KEYS) * dil), -1).astype(np.int32)


def _attn_prompt(qkv, rel_bias):
    bsz, t, _ = qkv.shape
    ng = len(A_GROUPS)

    def cur(part, g):
        return pl.BlockSpec((None, A_BLOCK, 128), lambda b, i, pair: (b, i, (part * ng + g) * 2 + pair))

    def before(part, g):
        span = A_KEYS * A_GROUPS[g][1]
        per = A_BLOCK // span
        return pl.BlockSpec((None, span, 128),
                            lambda b, i, pair: (b, jnp.maximum(i * per - 1, 0), (part * ng + g) * 2 + pair))

    specs = [cur(part, g) for part in range(3) for g in range(ng)] + \
            [before(part, g) for part in (1, 2) for g in range(ng)]
    maps = jnp.asarray(np.stack([_prompt_bucket_map(d) for _, d in A_GROUPS]))
    return pl.pallas_call(
        _attn_prompt_kernel,
        grid=(bsz, t // A_BLOCK, 2),
        in_specs=[pl.BlockSpec(memory_space=pltpu.SMEM), _const_spec(maps.shape)] + specs,
        out_specs=pl.BlockSpec((None, A_BLOCK, 128), lambda b, i, pair: (b, i, pair)),
        out_shape=jax.ShapeDtypeStruct((bsz, t, A_OUT), F32),
        scratch_shapes=[pltpu.VMEM((4 * ng, A_KEYS, 2 * A_KEYS), F32)] + [pltpu.VMEM((ng, A_BLOCK, 128), F32)] * 3,
        compiler_params=_params(3),
        name="attn_prompt",
    )(rel_bias, maps, *([qkv] * (5 * ng)))


def _merge_groups(outs, lses):
    m = functools.reduce(jnp.maximum, lses)
    ws = [jnp.exp(l - m) for l in lses]
    num = functools.reduce(lambda a, b: a + b, [w * o for w, o in zip(ws, outs)])
    return num / functools.reduce(lambda a, b: a + b, ws)


def _attn_sample_kernel(rb_ref, bc0, bc1, bc2, bn0, bn1, bn2, qkv_ref, c0_ref, c1_ref, c2_ref, out_ref,
                        bias_c0, bias_c1, bias_c2, bias_n):
    bias_c = (bias_c0, bias_c1, bias_c2)

    @pl.when(pl.program_id(0) == 0)
    def _():
        for g, (bc, bn) in enumerate(((bc0, bn0), (bc1, bn1), (bc2, bn2))):
            for h in range(4):
                bias_c[g][h] = _bias_from_buckets(bc[...], rb_ref, g * 4 + h)
                bias_n[g * 4 + h] = _bias_from_buckets(bn[...], rb_ref, g * 4 + h)

    qkv = qkv_ref[...]
    c_refs = (c0_ref, c1_ref, c2_ref)
    units = [(g, h) for g in range(len(A_GROUPS)) for h in range(4)]
    cols = [(g * 4 + h) * HEAD for g, h in units]
    q = [(qkv[:, c:c + HEAD] * (HEAD ** -0.5)).astype(BF16) for c in cols]
    kn = [qkv[:, 3 * A_OUT + c:3 * A_OUT + c + HEAD].astype(BF16) for c in cols]
    vn = [qkv[:, 6 * A_OUT + c:6 * A_OUT + c + HEAD].astype(BF16) for c in cols]
    n = range(len(units))
    s_c = [_dg(q[i], c_refs[g][0, h].astype(BF16), NN) + bias_c[g][h] for i, (g, h) in enumerate(units)]
    s_n = [_dg(q[i], kn[i], NT) + bias_n[i] for i in n]
    m = [jnp.maximum(jnp.max(s_c[i], axis=-1, keepdims=True), jnp.max(s_n[i], axis=-1, keepdims=True)) for i in n]
    p_c = [jnp.exp(s_c[i] - m[i]) for i in n]
    p_n = [jnp.exp(s_n[i] - m[i]) for i in n]
    den = [jnp.sum(p_c[i], axis=-1, keepdims=True) + jnp.sum(p_n[i], axis=-1, keepdims=True) for i in n]
    o = [(_dg(p_c[i].astype(BF16), c_refs[g][1, h].astype(BF16), NT) + _dg(p_n[i].astype(BF16), vn[i], NN)) / den[i]
         for i, (g, h) in enumerate(units)]
    lse = [m[i] + jnp.log(den[i]) for i in n]
    out_ref[...] = jnp.concatenate([_merge_groups(o[h::4], lse[h::4]) for h in range(4)], axis=-1)


def _sample_bucket_maps(window, dil, tp, t_real):
    t = np.arange(tp)[:, None]
    dist_c = window + t - np.arange(window)[None, :]
    ok_c = (dist_c % dil == 0) & (dist_c // dil <= A_KEYS) & (t < t_real)
    dist_n = t - np.arange(tp)[None, :]
    ok_n = (dist_n >= 0) & (dist_n % dil == 0) & (dist_n // dil <= A_KEYS)
    mc = np.where(ok_c, _t5_bucket_np(np.maximum(dist_c, 0)), -1).astype(np.int32)
    mn = np.where(ok_n, _t5_bucket_np(np.maximum(dist_n, 0)), -1).astype(np.int32)
    return mc, mn


def _attn_sample(qkv, caches, rel_bias, t_real):
    bsz, tp, _ = qkv.shape
    maps = [_sample_bucket_maps(w, d, tp, t_real) for w, d in A_GROUPS]
    mcs = [jnp.asarray(m[0]) for m in maps]
    mns = [jnp.asarray(m[1]) for m in maps]
    return pl.pallas_call(
        _attn_sample_kernel,
        grid=(bsz,),
        in_specs=[pl.BlockSpec(memory_space=pltpu.SMEM)]
        + [_const_spec(m.shape) for m in mcs] + [_const_spec(m.shape) for m in mns]
        + [pl.BlockSpec((None, tp, A_QKV), lambda b: (b, 0, 0))]
        + [pl.BlockSpec((None, 2, 4, HEAD, w), lambda b: (b, 0, 0, 0, 0)) for w, _ in A_GROUPS],
        out_specs=pl.BlockSpec((None, tp, A_OUT), lambda b: (b, 0, 0)),
        out_shape=jax.ShapeDtypeStruct((bsz, tp, A_OUT), F32),
        scratch_shapes=[pltpu.VMEM((4, tp, w), F32) for w, _ in A_GROUPS] + [pltpu.VMEM((12, tp, tp), F32)],
        compiler_params=_params(1),
        name="attn_sample",
    )(rel_bias, *mcs, *mns, qkv, *caches)


def _head_sums(x, h1, fn=lambda s: s):
    first = jnp.sum(jnp.where(h1, x, 0.0), axis=-1, keepdims=True)
    second = jnp.sum(jnp.where(h1, 0.0, x), axis=-1, keepdims=True)
    return jnp.where(h1, fn(first), fn(second))


def _rwkv_kernel(x0_ref, xn_ref, gin_ref, wrkv_ref, wlora_ref, sh_rkv_ref, sh_lora_ref, s0_ref, mu_rkv_ref,
                 mu_lora_ref, w0_ref, w2_ref, a0_ref, a2_ref, g2_ref, kk_ref, ka_ref, rk_ref, lnw_ref, lnb_ref,
                 tri_ref, out_ref, s_out_ref, sh_rkv_out, sh_lora_out, s_scr, prev_rkv, prev_lora, pc_scr,
                 *, nb, chunk, t_real):
    c = pl.program_id(1)
    nc = pl.num_programs(1)
    seqs = range(nb)
    n_rows = nb * chunk

    def project(x_ref):
        h = _rms(x_ref[...].reshape(n_rows, D_MODEL), gin_ref[...]).astype(BF16)
        pc_scr[:, :3 * B_WIDTH] = jnp.dot(h, wrkv_ref[...], preferred_element_type=F32)
        pc_scr[:, 3 * B_WIDTH:] = jnp.dot(h, wlora_ref[...], preferred_element_type=F32)

    @pl.when(c == 0)
    def _():
        project(x0_ref)
        zero_blk = jnp.zeros((HEAD, HEAD), F32)
        for s in seqs:
            for p in range(B_PAIRS):
                s_scr[s, p] = jnp.concatenate([jnp.concatenate([s0_ref[s, 2 * p], zero_blk], axis=1),
                                               jnp.concatenate([zero_blk, s0_ref[s, 2 * p + 1]], axis=1)], axis=0)
        prev_rkv[...] = sh_rkv_ref[...]
        prev_lora[...] = sh_lora_ref[...]

    rsl = [slice(s * chunk, (s + 1) * chunk) for s in seqs]
    rows = lax.broadcasted_iota(jnp.int32, (n_rows, 1), 0)
    pb = pc_scr[:, :3 * B_WIDTH]
    lr = pc_scr[:, 3 * B_WIDTH:]
    pb_prev = pltpu.roll(pb, 1, 0)
    lr_prev = pltpu.roll(lr, 1, 0)
    for s in seqs:
        at = rows == s * chunk
        pb_prev = jnp.where(at, prev_rkv[s], pb_prev)
        lr_prev = jnp.where(at, prev_lora[s], lr_prev)
        prev_rkv[s] = pb[(s + 1) * chunk - 1:(s + 1) * chunk, :]
        prev_lora[s] = lr[(s + 1) * chunk - 1:(s + 1) * chunk, :]
        last = s * chunk + (t_real - 1) % chunk
        sh_rkv_out[s] = pb[last:last + 1, :]
        sh_lora_out[s] = lr[last:last + 1, :]
    xs = pb + (pb_prev - pb) * mu_rkv_ref[...]
    xl = lr + (lr_prev - lr) * mu_lora_ref[...]
    r = xs[:, :B_WIDTH]
    k = xs[:, B_WIDTH:2 * B_WIDTH]
    v = xs[:, 2 * B_WIDTH:]
    th_hi, th_lo = _split2(jnp.tanh(xl[:, :128]))
    z = w0_ref[...] + (_dg(th_hi, w2_ref[0], NN) + (_dg(th_hi, w2_ref[1], NN) + _dg(th_lo, w2_ref[0], NN)))
    lam = -DECAY_SCALE * _sigmoid(z)
    a = _sigmoid(a0_ref[...] + _mm(xl[:, 128:256], a2_ref[...]))
    gate = _mm(_sigmoid(xl[:, 256:]), g2_ref[...])
    kk = k * kk_ref[...]
    kmod = k * (1.0 + (a - 1.0) * ka_ref[...])
    if t_real < chunk:
        live = rows % chunk < t_real
        lam = jnp.where(live, lam, 0.0)
        kk = jnp.where(live, kk, 0.0)
        kmod = jnp.where(live, kmod, 0.0)
        v = jnp.where(live, v, 0.0)

    cum = _mm_exact_lhs(tri_ref[...], lam, terms=2)
    cum_end = [cum[(s + 1) * chunk - 1:(s + 1) * chunk, :] for s in seqs]
    e_in = jnp.exp(cum)
    e_prev = jnp.exp(cum - lam)
    e_neg = jnp.exp(-cum)
    e_end = [jnp.exp(cum_end[s] - cum[rsl[s]]) for s in seqs]
    g_end = [jnp.exp(cum_end[s]) for s in seqs]

    ri = lax.broadcasted_iota(jnp.int32, (chunk, 128), 0)
    ci = lax.broadcasted_iota(jnp.int32, (chunk, 128), 1) % HEAD
    strict = ci < ri
    lower = ci <= ri
    lane = lax.broadcasted_iota(jnp.int32, (1, 128), 1)
    h1 = lane < HEAD
    bi = lax.broadcasted_iota(jnp.int32, (128, 128), 0) // HEAD
    bj = lax.broadcasted_iota(jnp.int32, (128, 128), 1) // HEAD
    block_diag = bi == bj
    levels = int(math.log2(chunk))
    zero = jnp.zeros((chunk, 128), F32)
    units = [(s, p) for s in seqs for p in range(B_PAIRS)]
    idx = range(len(units))
    csl = [slice(p * 128, (p + 1) * 128) for _, p in units]
    take = lambda arr: [arr[rsl[s], csl[i]] for i, (s, _) in enumerate(units)]

    kappa = [x * _head_sums(x * x, h1, lambda q: lax.rsqrt(jnp.maximum(q, 1e-24))) for x in take(kk)]
    a_u, r_u, k_u, v_u = take(a), take(r), take(kmod), take(v)
    e_neg_u = take(e_neg)
    bb = [kappa[i] * a_u[i] for i in idx]
    a_t = [-x * e for x, e in zip(kappa, take(e_prev))]
    r_t = [x * e for x, e in zip(r_u, take(e_in))]
    m4 = []
    for i in idx:
        l4 = jnp.concatenate([jnp.where(h1, a_t[i], zero), jnp.where(h1, zero, a_t[i]),
                              jnp.where(h1, r_t[i], zero), jnp.where(h1, zero, r_t[i])], axis=0)
        m4.append(_mm(l4, _stack_halves(bb[i] * e_neg_u[i], k_u[i] * e_neg_u[i]), NT))
    na = [[jnp.where(strict, m4[i][j * chunk:(j + 1) * chunk], 0.0) for j in range(2)] for i in idx]
    nr = [[jnp.where(lower, m4[i][(2 + j) * chunk:(3 + j) * chunk], 0.0) for j in range(2)] for i in idx]
    zv = [_stack_halves(zero, pltpu.roll(v_u[i], HEAD, 1)).astype(BF16) for i in idx]
    zs = [[jnp.where(h1, a_t[i], _mm(na[i][0], zv[i])), jnp.where(h1, _mm(na[i][1], zv[i]), a_t[i])] for i in idx]
    ps = [[na[i][j][:, :chunk].astype(BF16) for j in range(2)] for i in idx]
    for lvl in range(levels):
        last = lvl == levels - 1
        for i in idx:
            for j in range(2):
                z_b = zs[i][j].astype(BF16)
                upd = _dg(ps[i][j], z_b if last else jnp.concatenate([z_b, ps[i][j]], axis=-1), NN)
                zs[i][j] = zs[i][j] + upd[:, :128]
                if not last:
                    ps[i][j] = upd[:, 128:].astype(BF16)
    ta = [jnp.where(h1, zs[i][0], zs[i][1]) for i in idx]
    pv = [pltpu.roll(jnp.where(h1, zs[i][1], zs[i][0]), HEAD, 1) for i in idx]
    s_old = [s_scr[s, p] for s, p in units]
    s_b = [x.astype(BF16) for x in s_old]
    u = [_mm(ta[i], s_b[i], NT) + pv[i] for i in idx]
    uv = [_stack_halves(u[i], v_u[i]) for i in idx]
    uv_b = [x.astype(BF16) for x in uv]
    y = [_mm(r_t[i], s_b[i], NT) + jnp.where(h1, _mm(nr[i][0], uv_b[i]), _mm(nr[i][1], uv_b[i])) for i in idx]
    for i, (s, p) in enumerate(units):
        e = e_end[s][:, csl[i]]
        s_new = s_old[i] * g_end[s][:, csl[i]] + _mm(uv[i].T, _stack_halves(bb[i] * e, k_u[i] * e))
        s_scr[s, p] = jnp.where(block_diag, s_new, 0.0)

    mean = [_head_sums(y[i], h1) * (1.0 / HEAD) for i in idx]
    dlt = [y[i] - mean[i] for i in idx]
    var = [_head_sums(d * d, h1) * (1.0 / HEAD) for d in dlt]
    gate_u = take(gate)
    for i, (s, p) in enumerate(units):
        sl = csl[i]
        bonus = _head_sums(r_u[i] * k_u[i] * rk_ref[:, sl], h1)
        yn = dlt[i] * lax.rsqrt(var[i] + LNX_EPS) * lnw_ref[:, sl] + lnb_ref[:, sl]
        out_ref[s, :, sl] = (yn + bonus * v_u[i]) * gate_u[i]

    project(xn_ref)

    @pl.when(c == nc - 1)
    def _():
        for s in seqs:
            for p in range(B_PAIRS):
                s_pair = s_scr[s, p]
                s_out_ref[s, 2 * p] = s_pair[:HEAD, :HEAD]
                s_out_ref[s, 2 * p + 1] = s_pair[HEAD:, HEAD:]


def _rwkv(x, g_in, w_in, w_lora, sh_rkv, sh_lora, s0, prm, chunk, t_real):
    bsz, t, _ = x.shape
    nc = t // chunk
    nb = min(bsz, MAX_SEQS_PER_STEP)
    tri = np.kron(np.eye(nb), np.tril(np.ones((chunk, chunk)))).astype(np.float32)
    vec = lambda n: _const_spec((1, n))
    row = lambda a: a.reshape(1, -1)
    return pl.pallas_call(
        functools.partial(_rwkv_kernel, nb=nb, chunk=chunk, t_real=t_real),
        grid=(bsz // nb, nc),
        in_specs=[pl.BlockSpec((nb, chunk, D_MODEL), lambda b, c: (b, 0, 0)),
                  pl.BlockSpec((nb, chunk, D_MODEL), lambda b, c: (b, jnp.minimum(c + 1, nc - 1), 0)),
                  vec(D_MODEL),
                  pl.BlockSpec((D_MODEL, 3 * B_WIDTH), lambda b, c: (0, 1), pipeline_mode=pl.Buffered(1)),
                  _const_spec(w_lora.shape),
                  pl.BlockSpec((nb, 1, 3 * B_WIDTH), lambda b, c: (b, 0, 0)),
                  pl.BlockSpec((nb, 1, LORA_PAD), lambda b, c: (b, 0, 0)),
                  pl.BlockSpec((nb, 2 * B_PAIRS, HEAD, HEAD), lambda b, c: (b, 0, 0, 0)),
                  vec(3 * B_WIDTH), vec(LORA_PAD), vec(B_WIDTH), _const_spec((2, 128, B_WIDTH)),
                  vec(B_WIDTH), _const_spec((128, B_WIDTH)), _const_spec((256, B_WIDTH)),
                  vec(B_WIDTH), vec(B_WIDTH), vec(B_WIDTH), vec(B_WIDTH), vec(B_WIDTH),
                  _const_spec(tri.shape)],
        out_specs=[pl.BlockSpec((nb, chunk, B_WIDTH), lambda b, c: (b, c, 0)),
                   pl.BlockSpec((nb, 2 * B_PAIRS, HEAD, HEAD), lambda b, c: (b, 0, 0, 0)),
                   pl.BlockSpec((nb, 1, 3 * B_WIDTH), lambda b, c: (b, 0, 0)),
                   pl.BlockSpec((nb, 1, LORA_PAD), lambda b, c: (b, 0, 0))],
        out_shape=[jax.ShapeDtypeStruct((bsz, t, B_WIDTH), F32),
                   jax.ShapeDtypeStruct((bsz, 2 * B_PAIRS, HEAD, HEAD), F32),
                   jax.ShapeDtypeStruct((bsz, 1, 3 * B_WIDTH), F32),
                   jax.ShapeDtypeStruct((bsz, 1, LORA_PAD), F32)],
        scratch_shapes=[pltpu.VMEM((nb, B_PAIRS, 128, 128), F32), pltpu.VMEM((nb, 1, 3 * B_WIDTH), F32),
                        pltpu.VMEM((nb, 1, LORA_PAD), F32), pltpu.VMEM((nb * chunk, 3 * B_WIDTH + LORA_PAD), F32)],
        compiler_params=_params(2),
        name="rwkv7",
    )(x, x, row(g_in), w_in, w_lora, sh_rkv[:, None], sh_lora[:, None], s0,
      row(prm["mu_rkv"]), row(prm["mu_lora"]), row(prm["w0"]), prm["w2"], row(prm["a0"]), prm["a2"], prm["g2"],
      row(prm["k_k"]), row(prm["k_a"]), row(prm["r_k"]), row(prm["lnx_w"]), row(prm["lnx_b"]),
      jnp.asarray(tri, BF16))


def _gla_kernel(x0_ref, xn_ref, gin_ref, win_ref, s0_ref, lb_ref, gn_ref, sums_ref, out_ref, s_out_ref, s_scr, pc_scr,
                *, nb, chunk, t_real, layer):
    c = pl.program_id(1)
    nc = pl.num_programs(1)
    seqs = range(nb)
    n_rows = nb * chunk

    def project(x_ref):
        h = _rms(x_ref[...].reshape(n_rows, D_MODEL), gin_ref[...]).astype(BF16)
        return jnp.dot(h, win_ref[...], preferred_element_type=F32)

    @pl.when(c == 0)
    def _():
        pc_scr[...] = project(x0_ref)
        for s in seqs:
            for h in range(C_HEADS):
                s_scr[s, h] = s0_ref[s, h].T

    rsl = [slice(s * chunk, (s + 1) * chunk) for s in seqs]
    raw = lb_ref[...]
    e = jnp.exp(raw - jnp.max(raw, axis=0, keepdims=True))
    sm = e / jnp.sum(e, axis=0, keepdims=True)
    lb = jnp.sum(sm[:layer + 1], axis=0, keepdims=True) - sm[0:1]
    xq = pc_scr[:, :C_FDIM]
    q = xq * _sigmoid(xq)
    fg = lb + (1.0 - lb) * _sigmoid(pc_scr[:, C_FDIM:2 * C_FDIM])
    k = 1.0 - fg
    logf = jnp.log(fg)
    v = pc_scr[:, 2 * C_FDIM:2 * C_FDIM + D_MODEL]
    xg = pc_scr[:, 2 * C_FDIM + D_MODEL:]
    if t_real < chunk:
        live = lax.broadcasted_iota(jnp.int32, (n_rows, 1), 0) % chunk < t_real
        logf = jnp.where(live, logf, 0.0)
        k = jnp.where(live, k, 0.0)
        v = jnp.where(live, v, 0.0)

    sums = _mm_exact_lhs(sums_ref[...], logf)
    cum = sums[:n_rows]
    base = sums[n_rows:]
    q_in = q * jnp.exp(cum)
    q_loc = q * jnp.exp(cum - base)
    k_loc = k * jnp.exp(jnp.minimum(base - cum, EXP_CLAMP))
    nsub = chunk // GLA_SUB
    k_stack, v_stack, k_end, g_end = [], [], [], []
    for s in seqs:
        cum_s, k_s = cum[rsl[s]], k[rsl[s]]
        cum_end = cum_s[chunk - 1:chunk, :]
        k_end.append(k_s * jnp.exp(cum_end - cum_s))
        g_end.append(jnp.exp(cum_end))
        k_var = [k_loc[rsl[s]]]
        for i in range(1, nsub):
            ref_i = cum_s[i * GLA_SUB - 1:i * GLA_SUB, :]
            k_var.append(k_s * jnp.exp(jnp.minimum(ref_i - cum_s, 0.0)))
        k_stack.append(jnp.concatenate(k_var, axis=0) if nsub > 1 else k_var[0])
        v_stack.append(jnp.concatenate([v[rsl[s]]] * nsub, axis=0) if nsub > 1 else v[rsl[s]])

    ri = lax.broadcasted_iota(jnp.int32, (chunk, nsub * chunk), 0)
    cc = lax.broadcasted_iota(jnp.int32, (chunk, nsub * chunk), 1)
    var = cc // chunk
    ci = cc % chunk
    same_sub = ci // GLA_SUB == ri // GLA_SUB
    att_mask = ((var == 0) & same_sub & (ci <= ri)) | ((ri // GLA_SUB == var) & (ci < var * GLA_SUB))

    units = [(s, h) for s in seqs for h in range(C_HEADS)]
    idx = range(len(units))
    csl = [slice(h * 128, (h + 1) * 128) for _, h in units]
    att = [jnp.where(att_mask, _mm(q_loc[rsl[s], csl[i]], k_stack[s][:, csl[i]], NT), 0.0)
           for i, (s, _) in enumerate(units)]
    s_old = [s_scr[s, h] for s, h in units]
    outs = [_mm(q_in[rsl[s], csl[i]], s_old[i], NT) + _mm(att[i], v_stack[s][:, csl[i]])
            for i, (s, _) in enumerate(units)]
    for i, (s, h) in enumerate(units):
        sl = csl[i]
        s_scr[s, h] = s_old[i] * g_end[s][:, sl] + _mm(_stack_rows_128([v[rsl[s], sl]]).T,
                                                       _stack_rows_128([k_end[s][:, sl]]))
    o = jnp.concatenate([jnp.concatenate(outs[s * C_HEADS:(s + 1) * C_HEADS], axis=-1) for s in seqs], axis=0)
    out_ref[...] = (_rms(o, gn_ref[...]) * (xg * _sigmoid(xg))).reshape(nb, chunk, D_MODEL)
    pc_scr[...] = project(xn_ref)

    @pl.when(c == nc - 1)
    def _():
        for s in seqs:
            for h in range(C_HEADS):
                s_out_ref[s, h] = s_scr[s, h].T


def _gla(x, g_in, w_in, s0, lb_raw, layer, gn, chunk, t_real):
    bsz, t, _ = x.shape
    nc = t // chunk
    nb = min(bsz, MAX_SEQS_PER_STEP)
    idx = np.arange(chunk)
    eye = np.eye(nb)
    tri = np.kron(eye, np.tril(np.ones((chunk, chunk)))).astype(np.float32)
    sel = np.kron(eye, idx[None, :] < (idx[:, None] // GLA_SUB) * GLA_SUB).astype(np.float32)
    return pl.pallas_call(
        functools.partial(_gla_kernel, nb=nb, chunk=chunk, t_real=t_real, layer=layer),
        grid=(bsz // nb, nc),
        in_specs=[pl.BlockSpec((nb, chunk, D_MODEL), lambda b, c: (b, 0, 0)),
                  pl.BlockSpec((nb, chunk, D_MODEL), lambda b, c: (b, jnp.minimum(c + 1, nc - 1), 0)),
                  _const_spec((1, D_MODEL)), _const_spec(w_in.shape),
                  pl.BlockSpec((nb, C_HEADS, 128, 128), lambda b, c: (b, 0, 0, 0)),
                  _const_spec(lb_raw.shape), _const_spec((1, D_MODEL)),
                  _const_spec((2 * nb * chunk, nb * chunk))],
        out_specs=[pl.BlockSpec((nb, chunk, D_MODEL), lambda b, c: (b, c, 0)),
                   pl.BlockSpec((nb, C_HEADS, 128, 128), lambda b, c: (b, 0, 0, 0))],
        out_shape=[jax.ShapeDtypeStruct((bsz, t, D_MODEL), F32),
                   jax.ShapeDtypeStruct((bsz, C_HEADS, 128, 128), F32)],
        scratch_shapes=[pltpu.VMEM((nb, C_HEADS, 128, 128), F32), pltpu.VMEM((nb * chunk, 4 * D_MODEL), F32)],
        compiler_params=_params(2),
        name="hgrn2",
    )(x, x, g_in.reshape(1, -1), w_in, s0, lb_raw, gn.reshape(1, -1),
      jnp.asarray(np.concatenate([tri, sel]), BF16))


def _post_kernel(*refs, n_mix, final):
    x_ref = refs[0]
    mix_refs = refs[1:1 + n_mix]
    w_refs = refs[1 + n_mix:1 + 2 * n_mix]
    gm_ref, up_ref, down_ref = refs[1 + 2 * n_mix:4 + 2 * n_mix]
    gf_ref = refs[4 + 2 * n_mix] if final else None
    o_ref = refs[-1]
    x = x_ref[...]
    for m_ref, w_ref in zip(mix_refs, w_refs):
        x = x + jnp.dot(m_ref[...].astype(BF16), w_ref[...], preferred_element_type=F32)
    h = _rms(x, gm_ref[...]).astype(BF16)
    u = jnp.dot(h, up_ref[...], preferred_element_type=F32)
    u = jnp.square(jnp.maximum(u, 0.0)).astype(BF16)
    x = x + jnp.dot(u, down_ref[...], preferred_element_type=F32)
    if final:
        x = _rms(x, gf_ref[...])
    o_ref[...] = x


def _post(x, mixes, ws, gm, up, down, layer, gf=None, tm=512):
    m = x.shape[0]
    final = gf is not None
    row_spec = lambda n: pl.BlockSpec((tm, n), lambda i: (i, 0))
    layer_spec = lambda w: pl.BlockSpec((None,) + w.shape[1:], lambda i: (layer, 0, 0),
                                        pipeline_mode=pl.Buffered(1))
    rows = [x] + list(mixes)
    args = rows + [*ws, gm.reshape(1, -1), up, down]
    specs = ([row_spec(a.shape[1]) for a in rows] + [_const_spec(w.shape) for w in ws]
             + [_const_spec((1, D_MODEL)), layer_spec(up), layer_spec(down)])
    if final:
        args.append(gf.reshape(1, -1))
        specs.append(_const_spec((1, D_MODEL)))
    return pl.pallas_call(
        functools.partial(_post_kernel, n_mix=len(mixes), final=final),
        grid=(m // tm,), in_specs=specs, out_specs=row_spec(D_MODEL),
        out_shape=jax.ShapeDtypeStruct((m, D_MODEL), F32), compiler_params=_params(1), name="post_mlp",
    )(*args)


def _lora_pad_cols(x):
    pad = lambda a, n: jnp.pad(a, [(0, 0)] * (a.ndim - 1) + [(0, n - a.shape[-1])])
    return jnp.concatenate([pad(x[..., :64], 128), pad(x[..., 64:128], 128), pad(x[..., 128:], 256)], axis=-1)


def _lora_unpad_cols(x):
    return jnp.concatenate([x[..., :64], x[..., 128:192], x[..., 256:416]], axis=-1)


def _prepare(rel_bias, norm_mix, norm_mlp, norm_final, e_w_in, e_mu, e_w0, e_w2, e_a0, e_a2, e_g2, e_k_k, e_k_a,
             e_r_k, e_lnx_w, e_lnx_b, e_w_out, c_w_in, c_lb_raw, c_norm, c_w_out, mlp_up, mlp_down):
    pad_rows = lambda a, n: jnp.pad(a, [(0, n - a.shape[0]), (0, 0)])
    w_in = e_w_in[0]
    return dict(
        rel_bias=rel_bias,
        w_in=w_in.astype(BF16), w_lora=_lora_pad_cols(w_in[:, A_QKV + 3 * B_WIDTH:]).astype(BF16),
        rwkv=dict(mu_rkv=e_mu[0, :3 * B_WIDTH], mu_lora=_lora_pad_cols(e_mu[0, 3 * B_WIDTH:]),
                  w0=e_w0[0], w2=jnp.stack(_split2(pad_rows(e_w2[0], 128))), a0=e_a0[0],
                  a2=pad_rows(e_a2[0], 128).astype(BF16), g2=pad_rows(e_g2[0], 256).astype(BF16),
                  k_k=e_k_k[0], k_a=e_k_a[0], r_k=e_r_k[0].reshape(-1),
                  lnx_w=e_lnx_w[0], lnx_b=e_lnx_b[0]),
        w_out_a=e_w_out[0, :A_OUT].astype(BF16), w_out_b=e_w_out[0, A_OUT:].astype(BF16),
        c_w_in=c_w_in[0].astype(BF16), c_w_out=c_w_out[0].astype(BF16), lb_raw=c_lb_raw, c_norm=c_norm[0],
        norm_mix=norm_mix, norm_mlp=norm_mlp, norm_final=norm_final,
        up=mlp_up.astype(BF16), down=mlp_down.astype(BF16))


def _trunk(x, prm, attend, sh_rkv, sh_lora, wkv0, c0, chunk, t_real):
    bsz, t, _ = x.shape
    flat = lambda a: a.reshape(bsz * t, a.shape[-1])
    xf = flat(x)
    assert A_QKV == 3 * B_WIDTH
    (qkv,) = _norm_proj(xf, prm["norm_mix"][0], [(prm["w_in"], A_QKV, 0)])
    qkv = qkv.reshape(bsz, t, A_QKV)
    a_out = attend(qkv)
    b_out, wkv, last_rkv, last_lora = _rwkv(x, prm["norm_mix"][0], prm["w_in"], prm["w_lora"], sh_rkv, sh_lora,
                                            wkv0, prm["rwkv"], chunk, t_real)
    x1 = _post(xf, [flat(a_out), flat(b_out)], [prm["w_out_a"], prm["w_out_b"]], prm["norm_mlp"][0],
               prm["up"], prm["down"], 0)
    c_out, c_state = _gla(x1.reshape(bsz, t, D_MODEL), prm["norm_mix"][1], prm["c_w_in"], c0, prm["lb_raw"], 1,
                          prm["c_norm"], chunk, t_real)
    y = _post(x1, [flat(c_out)], [prm["c_w_out"]], prm["norm_mlp"][1], prm["up"], prm["down"], 1,
              gf=prm["norm_final"])
    kv_rows = []
    for g, (w, _) in enumerate(A_GROUPS):
        n = min(w, t_real)
        rows = qkv[:, t_real - n:t_real]
        kv = jnp.stack([rows[..., (3 * part + g) * A_OUT:(3 * part + g + 1) * A_OUT] for part in (1, 2)], axis=2)
        kv_rows.append(kv.reshape(1, bsz, n, 2, 4, HEAD))
    shift = jnp.concatenate([last_rkv[:, 0], _lora_unpad_cols(last_lora[:, 0])], axis=-1)[None]
    return y.reshape(bsz, t, D_MODEL)[:, :t_real], kv_rows, shift, wkv[None], c_state[None]


def kernel(x_prompt, x_sample, cache_a0, cache_a1, cache_a2, state_b_shift, state_b_wkv, state_c, rel_bias, norm_mix, norm_mlp, norm_final, e_w_in, e_mu, e_w0, e_w2, e_a0, e_a2, e_g2, e_k_k, e_k_a, e_r_k, e_lnx_w, e_lnx_b, e_w_out, c_w_in, c_lb_raw, c_norm, c_w_out, mlp_up, mlp_down):
    prm = _prepare(rel_bias, norm_mix, norm_mlp, norm_final, e_w_in, e_mu, e_w0, e_w2, e_a0, e_a2, e_g2, e_k_k,
                   e_k_a, e_r_k, e_lnx_w, e_lnx_b, e_w_out, c_w_in, c_lb_raw, c_norm, c_w_out, mlp_up, mlp_down)

    bp, tp, _ = x_prompt.shape

    def attend_prompt(qkv):
        return _attn_prompt(qkv, prm["rel_bias"])

    y_p, p_kv, p_shift, p_wkv, p_c = _trunk(
        x_prompt, prm, attend_prompt,
        jnp.zeros((bp, 3 * B_WIDTH), F32), jnp.zeros((bp, LORA_PAD), F32),
        jnp.zeros((bp, 12, HEAD, HEAD), F32), jnp.zeros((bp, C_HEADS, 128, 128), F32), chunk=64, t_real=tp)

    bs, ts, _ = x_sample.shape
    t_pad = 16
    x_s = jnp.pad(x_sample, ((0, 0), (0, t_pad - ts), (0, 0)))
    caches = [jnp.transpose(c[0], (0, 2, 3, 4, 1)) for c in (cache_a0, cache_a1, cache_a2)]

    def attend_sample(qkv):
        return _attn_sample(qkv, caches, prm["rel_bias"], ts)

    sh = state_b_shift[0]
    y_s, s_kv, s_shift, s_wkv, s_c = _trunk(
        x_s, prm, attend_sample, sh[:, :3 * B_WIDTH], _lora_pad_cols(sh[:, 3 * B_WIDTH:]),
        state_b_wkv[0], state_c[0], chunk=t_pad, t_real=ts)

    return (y_p, y_s, p_kv[0], p_kv[1], p_kv[2], p_shift, p_wkv, p_c,
            s_kv[0], s_kv[1], s_kv[2], s_shift, s_wkv, s_c)
```

```python
import functools
import math

import jax
import jax.numpy as jnp
import numpy as np
from jax import lax
from jax.experimental import pallas as pl
from jax.experimental.pallas import tpu as pltpu

F32 = jnp.float32
BF16 = jnp.bfloat16

LANES = 128
D_MODEL = 1024
EPS = 1e-6
LNX_EPS = 64e-5
HEAD = 64
CHUNK = 64
SAMPLE_ROWS = 16
A_GROUPS = ((128, 1), (512, 4), (2048, 16))
A_KEYS = 128
A_TILES = 4
A_BLOCK = A_KEYS * A_GROUPS[-1][1]
A_QKV = 2304
A_OUT = 256
N_BUCKETS = 32
BUCKET_MAX_DIST = 2048
B_WIDTH = 768
B_PAIRS = B_WIDTH // LANES
LORA_DIMS = (64, 64, 160)
LORA_TILES = (1, 1, 2)
LORA_PAD = sum(LORA_TILES) * LANES
C_FDIM = 1024
C_HEADS = 8
C_HEAD = 128
NEG = -1e30
LOG2E = math.log2(math.e)
DECAY_SCALE = math.exp(-0.5)
GLA_SUB = 16
EXP_CLAMP = 80.0
MAX_SEQS_PER_STEP = 4

VMEM_LIMIT = 56 * 1024 * 1024
assert 2 * HEAD == LANES and C_HEAD == LANES and A_KEYS == LANES

NN = (((1,), (0,)), ((), ()))
NT = (((1,), (1,)), ((), ()))


def _dg(a, b, dims):
    return lax.dot_general(a, b, dims, preferred_element_type=F32)


def _split2(x):
    hi = x.astype(BF16)
    lo = (x - hi.astype(F32)).astype(BF16)
    return hi, lo


def _split3(x):
    hi = x.astype(BF16)
    r1 = x - hi.astype(F32)
    mid = r1.astype(BF16)
    lo = (r1 - mid.astype(F32)).astype(BF16)
    return hi, mid, lo


def _mm(a, b, dims=NN):
    return _dg(a.astype(BF16), b.astype(BF16), dims)


def _mm_exact_lhs(a_bf16, b, terms=3):
    if terms == 2:
        bh, bl = _split2(b)
        return _dg(a_bf16, bh, NN) + _dg(a_bf16, bl, NN)
    bh, bm, bl = _split3(b)
    return _dg(a_bf16, bh, NN) + (_dg(a_bf16, bm, NN) + _dg(a_bf16, bl, NN))


def _pad_rows_to_tile(x):
    n = x.shape[0]
    if n == LANES:
        return x
    return jnp.concatenate([x, jnp.zeros((LANES - n, LANES), F32)], axis=0)


def _stack_halves(top, bottom):
    n = top.shape[0]
    half = LANES // 2
    if n == half:
        return jnp.concatenate([top, bottom], axis=0)
    pad = jnp.zeros((half - n, LANES), F32)
    return jnp.concatenate([top, pad, bottom, pad], axis=0)


def _sigmoid(x):
    return 0.5 * jnp.tanh(0.5 * x) + 0.5


def _rms(x, g):
    ms = jnp.mean(x * x, axis=-1, keepdims=True)
    return x * lax.rsqrt(ms + EPS) * g


def _const_spec(shape):
    nd = len(shape)
    return pl.BlockSpec(shape, lambda *_: (0,) * nd, pipeline_mode=pl.Buffered(1))


def _params(n_grid):
    return pltpu.CompilerParams(dimension_semantics=("arbitrary",) * n_grid,
                                vmem_limit_bytes=VMEM_LIMIT)


def _norm_proj_kernel(x_ref, g_ref, *refs):
    n = len(refs) // 2
    h = _rms(x_ref[...], g_ref[...]).astype(BF16)
    for w_ref, o_ref in zip(refs[:n], refs[n:]):
        o_ref[...] = jnp.dot(h, w_ref[...], preferred_element_type=F32)


def _norm_proj(x, g, ws, tm=512):
    m = x.shape[0]
    return pl.pallas_call(
        _norm_proj_kernel,
        grid=(m // tm,),
        in_specs=[pl.BlockSpec((tm, D_MODEL), lambda i: (i, 0)), _const_spec((1, D_MODEL))]
        + [pl.BlockSpec((D_MODEL, n), lambda i, j=j: (0, j), pipeline_mode=pl.Buffered(1)) for _, n, j in ws],
        out_specs=[pl.BlockSpec((tm, n), lambda i: (i, 0)) for _, n, _ in ws],
        out_shape=[jax.ShapeDtypeStruct((m, n), F32) for _, n, _ in ws],
        compiler_params=_params(1),
        name="norm_proj",
    )(x, g.reshape(1, D_MODEL), *[w for w, _, _ in ws])


def _t5_bucket_np(dist):
    max_exact = N_BUCKETS // 2
    d = np.maximum(dist, 1).astype(np.float32)
    large = max_exact + (np.log(d / np.float32(max_exact)) / np.float32(math.log(BUCKET_MAX_DIST / max_exact))
                         * np.float32(N_BUCKETS - max_exact)).astype(np.int32)
    large = np.minimum(large, N_BUCKETS - 1)
    return np.where(dist < max_exact, dist, large).astype(np.int32)


def _bias_from_buckets(idx, rb_ref, head):
    acc = jnp.full(idx.shape, NEG, F32)
    for b in range(N_BUCKETS):
        acc = jnp.where(idx == b, rb_ref[b, head], acc)
    return acc


def _tile_rows(start, dil):
    if dil > 1:
        return pl.ds(start, A_KEYS, stride=dil)
    return pl.ds(start if isinstance(start, int) else pl.multiple_of(start, A_KEYS), A_KEYS)


def _attn_prompt_kernel(rb_ref, bkt_ref, *refs):
    ng = len(A_GROUPS)
    q_refs, kc_refs, vc_refs, kp_refs, vp_refs = (refs[i * ng:(i + 1) * ng] for i in range(5))
    out_ref, bias_scr, o_scr, m_scr, d_scr = refs[5 * ng:]
    pair = pl.program_id(2)
    first = (pl.program_id(0) == 0) & (pl.program_id(1) == 0) & (pair == 0)

    @pl.when(first)
    def _():
        for g in range(ng):
            for h in range(4):
                bias_scr[g * 4 + h] = _bias_from_buckets(bkt_ref[g], rb_ref, g * 4 + h) * LOG2E

    col = lax.broadcasted_iota(jnp.int32, (1, 2 * A_KEYS), 1)
    prev_pen = jnp.where((col < A_KEYS) & (pl.program_id(1) == 0), NEG, 0.0).astype(F32)
    h1 = lax.broadcasted_iota(jnp.int32, (1, LANES), 1) < HEAD
    h2 = jnp.logical_not(h1)

    def attend(g, tiles):
        dil = A_GROUPS[g][1]
        n = range(len(tiles))
        rows = [_tile_rows(start, dil) for start, _, _ in tiles]
        prev = [((kp_refs[g], vp_refs[g]) if edge else (kc_refs[g], vc_refs[g]), _tile_rows(pstart, dil))
                for _, edge, pstart in tiles]
        bias_in = [bias_scr[g * 4 + 2 * pair + j] for j in range(2)]
        bias_edge = [b + prev_pen for b in bias_in]
        q = [q_refs[g][r, :] * (HEAD ** -0.5 * LOG2E) for r in rows]
        k = [jnp.concatenate([prev[i][0][0][prev[i][1], :], kc_refs[g][rows[i], :]], axis=0).astype(BF16)
             for i in n]
        v = [jnp.concatenate([prev[i][0][1][prev[i][1], :], vc_refs[g][rows[i], :]], axis=0).astype(BF16)
             for i in n]
        s = [[_dg(jnp.where(mine, q[i], 0.0).astype(BF16), k[i], NT) + (bias_edge if tiles[i][1] else bias_in)[j]
              for j, mine in enumerate((h1, h2))] for i in n]
        m = [[jnp.max(x, axis=-1, keepdims=True) for x in s[i]] for i in n]
        p = [[jnp.exp2(x - mx) for x, mx in zip(s[i], m[i])] for i in n]
        den = [[jnp.sum(x, axis=-1, keepdims=True) for x in p[i]] for i in n]
        o = [[_dg(x.astype(BF16), v[i], NN) for x in p[i]] for i in n]
        for i in n:
            o_scr[g, rows[i], :] = jnp.where(h1, o[i][0], o[i][1])
            m_scr[g, rows[i], :] = jnp.where(h1, m[i][0], m[i][1])
            d_scr[g, rows[i], :] = jnp.where(h1, den[i][0], den[i][1])

    n_tiles = A_BLOCK // A_KEYS
    for g, (_, dil) in enumerate(A_GROUPS):
        span = A_KEYS * dil
        single_span = span == A_BLOCK

        def batch(it, static, g=g, dil=dil, span=span, single_span=single_span):
            tiles = []
            for u in range(A_TILES):
                t = it * A_TILES + u
                s_idx, r = (t // dil, t % dil) if static else _span_residue(it, u, dil)
                edge = single_span or (static and s_idx == 0)
                tiles.append((s_idx * span + r, edge, r if edge else (s_idx - 1) * span + r))
            attend(g, tiles)

        batch(0, True)

        def body(it, carry, batch=batch):
            batch(it, False)
            return carry
        lax.fori_loop(1, n_tiles // A_TILES, body, 0)

    def merge(i, carry):
        rows = pl.ds(pl.multiple_of(i * A_KEYS, A_KEYS), A_KEYS)
        ms = [m_scr[g, rows, :] for g in range(ng)]
        top = functools.reduce(jnp.maximum, ms)
        ws = [jnp.exp2(x - top) for x in ms]
        num = functools.reduce(lambda a, b: a + b, [w * o_scr[g, rows, :] for g, w in enumerate(ws)])
        den = functools.reduce(lambda a, b: a + b, [w * d_scr[g, rows, :] for g, w in enumerate(ws)])
        out_ref[rows, :] = num / den
        return carry
    lax.fori_loop(0, n_tiles, merge, 0)


def _span_residue(it, u, dil):
    if dil >= A_TILES:
        per = dil // A_TILES
        return it // per, (it % per) * A_TILES + u
    return it * (A_TILES // dil) + u // dil, u % dil


def _prompt_bucket_map(dil):
    qi = np.arange(A_KEYS)[:, None]
    ki = np.arange(2 * A_KEYS)[None, :]
    j = qi + A_KEYS - ki
    return np.where((j >= 0) & (j <= A_KEYS), _t5_bucket_np(np.clip(j, 0, A_KEYS) * dil), -1).astype(np.int32)


def _attn_prompt(qkv, rel_bias):
    bsz, t, _ = qkv.shape
    ng = len(A_GROUPS)

    def cur(part, g):
        return pl.BlockSpec((None, A_BLOCK, LANES), lambda b, i, pair: (b, i, (part * ng + g) * 2 + pair))

    def before(part, g):
        span = A_KEYS * A_GROUPS[g][1]
        per = A_BLOCK // span
        return pl.BlockSpec((None, span, LANES),
                            lambda b, i, pair: (b, jnp.maximum(i * per - 1, 0), (part * ng + g) * 2 + pair))

    specs = [cur(part, g) for part in range(3) for g in range(ng)] + \
            [before(part, g) for part in (1, 2) for g in range(ng)]
    maps = jnp.asarray(np.stack([_prompt_bucket_map(d) for _, d in A_GROUPS]))
    return pl.pallas_call(
        _attn_prompt_kernel,
        grid=(bsz, t // A_BLOCK, 2),
        in_specs=[pl.BlockSpec(memory_space=pltpu.SMEM), _const_spec(maps.shape)] + specs,
        out_specs=pl.BlockSpec((None, A_BLOCK, LANES), lambda b, i, pair: (b, i, pair)),
        out_shape=jax.ShapeDtypeStruct((bsz, t, A_OUT), F32),
        scratch_shapes=[pltpu.VMEM((4 * ng, A_KEYS, 2 * A_KEYS), F32)] + [pltpu.VMEM((ng, A_BLOCK, LANES), F32)] * 3,
        compiler_params=_params(3),
        name="attn_prompt",
    )(rel_bias, maps, *([qkv] * (5 * ng)))


def _merge_groups(outs, lses):
    m = functools.reduce(jnp.maximum, lses)
    ws = [jnp.exp(l - m) for l in lses]
    num = functools.reduce(lambda a, b: a + b, [w * o for w, o in zip(ws, outs)])
    return num / functools.reduce(lambda a, b: a + b, ws)


def _attn_sample_kernel(rb_ref, bc0, bc1, bc2, bn0, bn1, bn2, qkv_ref, c0_ref, c1_ref, c2_ref, out_ref,
                        bias_c0, bias_c1, bias_c2, bias_n):
    bias_c = (bias_c0, bias_c1, bias_c2)

    @pl.when(pl.program_id(0) == 0)
    def _():
        for g, (bc, bn) in enumerate(((bc0, bn0), (bc1, bn1), (bc2, bn2))):
            for h in range(4):
                bias_c[g][h] = _bias_from_buckets(bc[...], rb_ref, g * 4 + h)
                bias_n[g * 4 + h] = _bias_from_buckets(bn[...], rb_ref, g * 4 + h)

    qkv = qkv_ref[...]
    c_refs = (c0_ref, c1_ref, c2_ref)
    units = [(g, h) for g in range(len(A_GROUPS)) for h in range(4)]
    cols = [(g * 4 + h) * HEAD for g, h in units]
    q = [(qkv[:, c:c + HEAD] * (HEAD ** -0.5)).astype(BF16) for c in cols]
    kn = [qkv[:, 3 * A_OUT + c:3 * A_OUT + c + HEAD].astype(BF16) for c in cols]
    vn = [qkv[:, 6 * A_OUT + c:6 * A_OUT + c + HEAD].astype(BF16) for c in cols]
    n = range(len(units))
    s_c = [_dg(q[i], c_refs[g][0, h].astype(BF16), NN) + bias_c[g][h] for i, (g, h) in enumerate(units)]
    s_n = [_dg(q[i], kn[i], NT) + bias_n[i] for i in n]
    m = [jnp.maximum(jnp.max(s_c[i], axis=-1, keepdims=True), jnp.max(s_n[i], axis=-1, keepdims=True)) for i in n]
    p_c = [jnp.exp(s_c[i] - m[i]) for i in n]
    p_n = [jnp.exp(s_n[i] - m[i]) for i in n]
    den = [jnp.sum(p_c[i], axis=-1, keepdims=True) + jnp.sum(p_n[i], axis=-1, keepdims=True) for i in n]
    o = [(_dg(p_c[i].astype(BF16), c_refs[g][1, h].astype(BF16), NT) + _dg(p_n[i].astype(BF16), vn[i], NN)) / den[i]
         for i, (g, h) in enumerate(units)]
    lse = [m[i] + jnp.log(den[i]) for i in n]
    out_ref[...] = jnp.concatenate([_merge_groups(o[h::4], lse[h::4]) for h in range(4)], axis=-1)


def _sample_bucket_maps(window, dil, tp, t_real):
    t = np.arange(tp)[:, None]
    dist_c = window + t - np.arange(window)[None, :]
    ok_c = (dist_c % dil == 0) & (dist_c // dil <= A_KEYS) & (t < t_real)
    dist_n = t - np.arange(tp)[None, :]
    ok_n = (dist_n >= 0) & (dist_n % dil == 0) & (dist_n // dil <= A_KEYS)
    mc = np.where(ok_c, _t5_bucket_np(np.maximum(dist_c, 0)), -1).astype(np.int32)
    mn = np.where(ok_n, _t5_bucket_np(np.maximum(dist_n, 0)), -1).astype(np.int32)
    return mc, mn


def _attn_sample(qkv, caches, rel_bias, t_real):
    bsz, tp, _ = qkv.shape
    maps = [_sample_bucket_maps(w, d, tp, t_real) for w, d in A_GROUPS]
    mcs = [jnp.asarray(m[0]) for m in maps]
    mns = [jnp.asarray(m[1]) for m in maps]
    return pl.pallas_call(
        _attn_sample_kernel,
        grid=(bsz,),
        in_specs=[pl.BlockSpec(memory_space=pltpu.SMEM)]
        + [_const_spec(m.shape) for m in mcs] + [_const_spec(m.shape) for m in mns]
        + [pl.BlockSpec((None, tp, A_QKV), lambda b: (b, 0, 0))]
        + [pl.BlockSpec((None, 2, 4, HEAD, w), lambda b: (b, 0, 0, 0, 0)) for w, _ in A_GROUPS],
        out_specs=pl.BlockSpec((None, tp, A_OUT), lambda b: (b, 0, 0)),
        out_shape=jax.ShapeDtypeStruct((bsz, tp, A_OUT), F32),
        scratch_shapes=[pltpu.VMEM((4, tp, w), F32) for w, _ in A_GROUPS] + [pltpu.VMEM((12, tp, tp), F32)],
        compiler_params=_params(1),
        name="attn_sample",
    )(rel_bias, *mcs, *mns, qkv, *caches)


def _head_sums(x, h1, fn=lambda s: s):
    first = jnp.sum(jnp.where(h1, x, 0.0), axis=-1, keepdims=True)
    second = jnp.sum(jnp.where(h1, 0.0, x), axis=-1, keepdims=True)
    return jnp.where(h1, fn(first), fn(second))


def _rwkv_kernel(rkv_ref, lora_ref, sh_rkv_ref, sh_lora_ref, s0_ref, mu_rkv_ref, mu_lora_ref, w0_ref, w2_ref,
                 a0_ref, a2_ref, g2_ref, kk_ref, ka_ref, rk_ref, lnw_ref, lnb_ref, tri_ref,
                 out_ref, s_out_ref, s_scr, prev_rkv, prev_lora, *, nb, chunk, t_real):
    c = pl.program_id(1)
    nc = pl.num_programs(1)
    seqs = range(nb)

    @pl.when(c == 0)
    def _():
        zero_blk = jnp.zeros((HEAD, HEAD), F32)
        for s in seqs:
            for p in range(B_PAIRS):
                s_scr[s, p] = jnp.concatenate([jnp.concatenate([s0_ref[s, 2 * p], zero_blk], axis=1),
                                               jnp.concatenate([zero_blk, s0_ref[s, 2 * p + 1]], axis=1)], axis=0)
        prev_rkv[...] = sh_rkv_ref[...]
        prev_lora[...] = sh_lora_ref[...]

    n_rows = nb * chunk
    rsl = [slice(s * chunk, (s + 1) * chunk) for s in seqs]
    rows = lax.broadcasted_iota(jnp.int32, (n_rows, 1), 0)
    pb = rkv_ref[...].reshape(n_rows, 3 * B_WIDTH)
    lr = lora_ref[...].reshape(n_rows, LORA_PAD)
    pb_prev = pltpu.roll(pb, 1, 0)
    lr_prev = pltpu.roll(lr, 1, 0)
    for s in seqs:
        at = rows == s * chunk
        pb_prev = jnp.where(at, prev_rkv[s], pb_prev)
        lr_prev = jnp.where(at, prev_lora[s], lr_prev)
        prev_rkv[s] = pb[(s + 1) * chunk - 1:(s + 1) * chunk, :]
        prev_lora[s] = lr[(s + 1) * chunk - 1:(s + 1) * chunk, :]
    xs = pb + (pb_prev - pb) * mu_rkv_ref[...]
    xl = lr + (lr_prev - lr) * mu_lora_ref[...]
    r = xs[:, :B_WIDTH]
    k = xs[:, B_WIDTH:2 * B_WIDTH]
    v = xs[:, 2 * B_WIDTH:]
    o_aaa, o_gate = LORA_TILES[0] * LANES, (LORA_TILES[0] + LORA_TILES[1]) * LANES
    th_hi, th_lo = _split2(jnp.tanh(xl[:, :o_aaa]))
    z = w0_ref[...] + (_dg(th_hi, w2_ref[0], NN) + (_dg(th_hi, w2_ref[1], NN) + _dg(th_lo, w2_ref[0], NN)))
    lam = -DECAY_SCALE * _sigmoid(z)
    a = _sigmoid(a0_ref[...] + _mm(xl[:, o_aaa:o_gate], a2_ref[...]))
    gate = _mm(_sigmoid(xl[:, o_gate:]), g2_ref[...])
    kk = k * kk_ref[...]
    kmod = k * (1.0 + (a - 1.0) * ka_ref[...])
    if t_real < chunk:
        live = rows % chunk < t_real
        lam = jnp.where(live, lam, 0.0)
        kk = jnp.where(live, kk, 0.0)
        kmod = jnp.where(live, kmod, 0.0)
        v = jnp.where(live, v, 0.0)

    cum = _mm_exact_lhs(tri_ref[...], lam, terms=2)
    cum_end = [cum[(s + 1) * chunk - 1:(s + 1) * chunk, :] for s in seqs]
    e_in = jnp.exp(cum)
    e_prev = jnp.exp(cum - lam)
    e_neg = jnp.exp(-cum)
    e_end = [jnp.exp(cum_end[s] - cum[rsl[s]]) for s in seqs]
    g_end = [jnp.exp(cum_end[s]) for s in seqs]

    ri = lax.broadcasted_iota(jnp.int32, (chunk, LANES), 0)
    ci = lax.broadcasted_iota(jnp.int32, (chunk, LANES), 1) % HEAD
    strict = ci < ri
    lower = ci <= ri
    lane = lax.broadcasted_iota(jnp.int32, (1, LANES), 1)
    h1 = lane < HEAD
    bi = lax.broadcasted_iota(jnp.int32, (LANES, LANES), 0) // HEAD
    bj = lax.broadcasted_iota(jnp.int32, (LANES, LANES), 1) // HEAD
    block_diag = bi == bj
    levels = int(math.log2(chunk))
    zero = jnp.zeros((chunk, LANES), F32)
    units = [(s, p) for s in seqs for p in range(B_PAIRS)]
    idx = range(len(units))
    csl = [slice(p * LANES, (p + 1) * LANES) for _, p in units]
    take = lambda arr: [arr[rsl[s], csl[i]] for i, (s, _) in enumerate(units)]

    kappa = [x * _head_sums(x * x, h1, lambda q: lax.rsqrt(jnp.maximum(q, 1e-24))) for x in take(kk)]
    a_u, r_u, k_u, v_u = take(a), take(r), take(kmod), take(v)
    e_neg_u = take(e_neg)
    bb = [kappa[i] * a_u[i] for i in idx]
    a_t = [-x * e for x, e in zip(kappa, take(e_prev))]
    r_t = [x * e for x, e in zip(r_u, take(e_in))]
    m4 = []
    for i in idx:
        l4 = jnp.concatenate([jnp.where(h1, a_t[i], zero), jnp.where(h1, zero, a_t[i]),
                              jnp.where(h1, r_t[i], zero), jnp.where(h1, zero, r_t[i])], axis=0)
        m4.append(_mm(l4, _stack_halves(bb[i] * e_neg_u[i], k_u[i] * e_neg_u[i]), NT))
    na = [[jnp.where(strict, m4[i][j * chunk:(j + 1) * chunk], 0.0) for j in range(2)] for i in idx]
    nr = [[jnp.where(lower, m4[i][(2 + j) * chunk:(3 + j) * chunk], 0.0) for j in range(2)] for i in idx]
    zv = [_stack_halves(zero, pltpu.roll(v_u[i], HEAD, 1)).astype(BF16) for i in idx]
    zs = [[jnp.where(h1, a_t[i], _mm(na[i][0], zv[i])), jnp.where(h1, _mm(na[i][1], zv[i]), a_t[i])] for i in idx]
    ps = [[na[i][j][:, :chunk].astype(BF16) for j in range(2)] for i in idx]
    for lvl in range(levels):
        last = lvl == levels - 1
        for i in idx:
            for j in range(2):
                z_b = zs[i][j].astype(BF16)
                upd = _dg(ps[i][j], z_b if last else jnp.concatenate([z_b, ps[i][j]], axis=-1), NN)
                zs[i][j] = zs[i][j] + upd[:, :LANES]
                if not last:
                    ps[i][j] = upd[:, LANES:].astype(BF16)
    ta = [jnp.where(h1, zs[i][0], zs[i][1]) for i in idx]
    pv = [pltpu.roll(jnp.where(h1, zs[i][1], zs[i][0]), HEAD, 1) for i in idx]
    s_old = [s_scr[s, p] for s, p in units]
    s_b = [x.astype(BF16) for x in s_old]
    u = [_mm(ta[i], s_b[i], NT) + pv[i] for i in idx]
    uv = [_stack_halves(u[i], v_u[i]) for i in idx]
    uv_b = [x.astype(BF16) for x in uv]
    y = [_mm(r_t[i], s_b[i], NT) + jnp.where(h1, _mm(nr[i][0], uv_b[i]), _mm(nr[i][1], uv_b[i])) for i in idx]
    for i, (s, p) in enumerate(units):
        e = e_end[s][:, csl[i]]
        s_new = s_old[i] * g_end[s][:, csl[i]] + _mm(uv[i].T, _stack_halves(bb[i] * e, k_u[i] * e))
        s_scr[s, p] = jnp.where(block_diag, s_new, 0.0)

    mean = [_head_sums(y[i], h1) * (1.0 / HEAD) for i in idx]
    dlt = [y[i] - mean[i] for i in idx]
    var = [_head_sums(d * d, h1) * (1.0 / HEAD) for d in dlt]
    gate_u = take(gate)
    for i, (s, p) in enumerate(units):
        sl = csl[i]
        bonus = _head_sums(r_u[i] * k_u[i] * rk_ref[:, sl], h1)
        yn = dlt[i] * lax.rsqrt(var[i] + LNX_EPS) * lnw_ref[:, sl] + lnb_ref[:, sl]
        out_ref[s, :, sl] = (yn + bonus * v_u[i]) * gate_u[i]

    @pl.when(c == nc - 1)
    def _():
        for s in seqs:
            for p in range(B_PAIRS):
                s_pair = s_scr[s, p]
                s_out_ref[s, 2 * p] = s_pair[:HEAD, :HEAD]
                s_out_ref[s, 2 * p + 1] = s_pair[HEAD:, HEAD:]


def _rwkv(rkv, lora, sh_rkv, sh_lora, s0, prm, chunk, t_real):
    bsz, t, _ = rkv.shape
    nc = t // chunk
    nb = min(bsz, MAX_SEQS_PER_STEP)
    tri = np.kron(np.eye(nb), np.tril(np.ones((chunk, chunk)))).astype(np.float32)
    vec = lambda n: _const_spec((1, n))
    row = lambda x: x.reshape(1, -1)
    out, s_out = pl.pallas_call(
        functools.partial(_rwkv_kernel, nb=nb, chunk=chunk, t_real=t_real),
        grid=(bsz // nb, nc),
        in_specs=[pl.BlockSpec((nb, chunk, 3 * B_WIDTH), lambda b, c: (b, c, 0)),
                  pl.BlockSpec((nb, chunk, LORA_PAD), lambda b, c: (b, c, 0)),
                  pl.BlockSpec((nb, 1, 3 * B_WIDTH), lambda b, c: (b, 0, 0)),
                  pl.BlockSpec((nb, 1, LORA_PAD), lambda b, c: (b, 0, 0)),
                  pl.BlockSpec((nb, 2 * B_PAIRS, HEAD, HEAD), lambda b, c: (b, 0, 0, 0)),
                  vec(3 * B_WIDTH), vec(LORA_PAD), vec(B_WIDTH), _const_spec(prm["w2"].shape),
                  vec(B_WIDTH), _const_spec(prm["a2"].shape), _const_spec(prm["g2"].shape),
                  vec(B_WIDTH), vec(B_WIDTH), vec(B_WIDTH), vec(B_WIDTH), vec(B_WIDTH),
                  _const_spec(tri.shape)],
        out_specs=[pl.BlockSpec((nb, chunk, B_WIDTH), lambda b, c: (b, c, 0)),
                   pl.BlockSpec((nb, 2 * B_PAIRS, HEAD, HEAD), lambda b, c: (b, 0, 0, 0))],
        out_shape=[jax.ShapeDtypeStruct((bsz, t, B_WIDTH), F32),
                   jax.ShapeDtypeStruct((bsz, 2 * B_PAIRS, HEAD, HEAD), F32)],
        scratch_shapes=[pltpu.VMEM((nb, B_PAIRS, LANES, LANES), F32), pltpu.VMEM((nb, 1, 3 * B_WIDTH), F32),
                        pltpu.VMEM((nb, 1, LORA_PAD), F32)],
        compiler_params=_params(2),
        name="rwkv7",
    )(rkv, lora, sh_rkv[:, None], sh_lora[:, None], s0,
      row(prm["mu_rkv"]), row(prm["mu_lora"]), row(prm["w0"]), prm["w2"], row(prm["a0"]), prm["a2"], prm["g2"],
      row(prm["k_k"]), row(prm["k_a"]), row(prm["r_k"]), row(prm["lnx_w"]), row(prm["lnx_b"]),
      jnp.asarray(tri, BF16))
    return out, s_out


def _gla_kernel(pc_ref, s0_ref, lb_ref, gn_ref, sums_ref, out_ref, s_out_ref, s_scr, *, nb, chunk, t_real, layer):
    c = pl.program_id(1)
    nc = pl.num_programs(1)
    seqs = range(nb)

    @pl.when(c == 0)
    def _():
        for s in seqs:
            for h in range(C_HEADS):
                s_scr[s, h] = s0_ref[s, h].T

    n_rows = nb * chunk
    rsl = [slice(s * chunk, (s + 1) * chunk) for s in seqs]
    pc = pc_ref[...].reshape(n_rows, 4 * D_MODEL)
    raw = lb_ref[...]
    e = jnp.exp(raw - jnp.max(raw, axis=0, keepdims=True))
    sm = e / jnp.sum(e, axis=0, keepdims=True)
    lb = jnp.sum(sm[:layer + 1], axis=0, keepdims=True) - sm[0:1]
    xq = pc[:, :C_FDIM]
    q = xq * _sigmoid(xq)
    fg = lb + (1.0 - lb) * _sigmoid(pc[:, C_FDIM:2 * C_FDIM])
    k = 1.0 - fg
    logf = jnp.log(fg)
    v = pc[:, 2 * C_FDIM:2 * C_FDIM + D_MODEL]
    xg = pc[:, 2 * C_FDIM + D_MODEL:]
    if t_real < chunk:
        live = lax.broadcasted_iota(jnp.int32, (n_rows, 1), 0) % chunk < t_real
        logf = jnp.where(live, logf, 0.0)
        k = jnp.where(live, k, 0.0)
        v = jnp.where(live, v, 0.0)

    sums = _mm_exact_lhs(sums_ref[...], logf)
    cum = sums[:n_rows]
    base = sums[n_rows:]
    q_in = q * jnp.exp(cum)
    q_loc = q * jnp.exp(cum - base)
    k_loc = k * jnp.exp(jnp.minimum(base - cum, EXP_CLAMP))
    nsub = chunk // GLA_SUB
    k_stack, v_stack, k_end, g_end = [], [], [], []
    for s in seqs:
        cum_s, k_s = cum[rsl[s]], k[rsl[s]]
        cum_end = cum_s[chunk - 1:chunk, :]
        k_end.append(k_s * jnp.exp(cum_end - cum_s))
        g_end.append(jnp.exp(cum_end))
        k_var = [k_loc[rsl[s]]]
        for i in range(1, nsub):
            ref_i = cum_s[i * GLA_SUB - 1:i * GLA_SUB, :]
            k_var.append(k_s * jnp.exp(jnp.minimum(ref_i - cum_s, 0.0)))
        k_stack.append(jnp.concatenate(k_var, axis=0) if nsub > 1 else k_var[0])
        v_stack.append(jnp.concatenate([v[rsl[s]]] * nsub, axis=0) if nsub > 1 else v[rsl[s]])

    ri = lax.broadcasted_iota(jnp.int32, (chunk, nsub * chunk), 0)
    cc = lax.broadcasted_iota(jnp.int32, (chunk, nsub * chunk), 1)
    var = cc // chunk
    ci = cc % chunk
    same_sub = ci // GLA_SUB == ri // GLA_SUB
    att_mask = ((var == 0) & same_sub & (ci <= ri)) | ((ri // GLA_SUB == var) & (ci < var * GLA_SUB))

    units = [(s, h) for s in seqs for h in range(C_HEADS)]
    idx = range(len(units))
    csl = [slice(h * C_HEAD, (h + 1) * C_HEAD) for _, h in units]
    att = [jnp.where(att_mask, _mm(q_loc[rsl[s], csl[i]], k_stack[s][:, csl[i]], NT), 0.0)
           for i, (s, _) in enumerate(units)]
    s_old = [s_scr[s, h] for s, h in units]
    outs = [_mm(q_in[rsl[s], csl[i]], s_old[i], NT) + _mm(att[i], v_stack[s][:, csl[i]])
            for i, (s, _) in enumerate(units)]
    for i, (s, h) in enumerate(units):
        sl = csl[i]
        s_scr[s, h] = s_old[i] * g_end[s][:, sl] + _mm(_pad_rows_to_tile(v[rsl[s], sl]).T,
                                                       _pad_rows_to_tile(k_end[s][:, sl]))
    o = jnp.concatenate([jnp.concatenate(outs[s * C_HEADS:(s + 1) * C_HEADS], axis=-1) for s in seqs], axis=0)
    out_ref[...] = (_rms(o, gn_ref[...]) * (xg * _sigmoid(xg))).reshape(nb, chunk, D_MODEL)

    @pl.when(c == nc - 1)
    def _():
        for s in seqs:
            for h in range(C_HEADS):
                s_out_ref[s, h] = s_scr[s, h].T


def _gla(pc, s0, lb_raw, layer, gn, chunk, t_real):
    bsz, t, _ = pc.shape
    nc = t // chunk
    nb = min(bsz, MAX_SEQS_PER_STEP)
    idx = np.arange(chunk)
    eye = np.eye(nb)
    tri = np.kron(eye, np.tril(np.ones((chunk, chunk)))).astype(np.float32)
    sel = np.kron(eye, idx[None, :] < (idx[:, None] // GLA_SUB) * GLA_SUB).astype(np.float32)
    return pl.pallas_call(
        functools.partial(_gla_kernel, nb=nb, chunk=chunk, t_real=t_real, layer=layer),
        grid=(bsz // nb, nc),
        in_specs=[pl.BlockSpec((nb, chunk, 4 * D_MODEL), lambda b, c: (b, c, 0)),
                  pl.BlockSpec((nb, C_HEADS, C_HEAD, C_HEAD), lambda b, c: (b, 0, 0, 0)),
                  _const_spec(lb_raw.shape), _const_spec((1, D_MODEL)),
                  _const_spec((2 * nb * chunk, nb * chunk))],
        out_specs=[pl.BlockSpec((nb, chunk, D_MODEL), lambda b, c: (b, c, 0)),
                   pl.BlockSpec((nb, C_HEADS, C_HEAD, C_HEAD), lambda b, c: (b, 0, 0, 0))],
        out_shape=[jax.ShapeDtypeStruct((bsz, t, D_MODEL), F32),
                   jax.ShapeDtypeStruct((bsz, C_HEADS, C_HEAD, C_HEAD), F32)],
        scratch_shapes=[pltpu.VMEM((nb, C_HEADS, C_HEAD, C_HEAD), F32)],
        compiler_params=_params(2),
        name="hgrn2",
    )(pc, s0, lb_raw, gn.reshape(1, -1), jnp.asarray(np.concatenate([tri, sel]), BF16))


def _post_kernel(*refs, n_mix, final):
    x_ref = refs[0]
    mix_refs = refs[1:1 + n_mix]
    w_refs = refs[1 + n_mix:1 + 2 * n_mix]
    gm_ref, up_ref, down_ref = refs[1 + 2 * n_mix:4 + 2 * n_mix]
    gf_ref = refs[4 + 2 * n_mix] if final else None
    o_ref = refs[-1]
    x = x_ref[...]
    for m_ref, w_ref in zip(mix_refs, w_refs):
        x = x + jnp.dot(m_ref[...].astype(BF16), w_ref[...], preferred_element_type=F32)
    h = _rms(x, gm_ref[...]).astype(BF16)
    u = jnp.dot(h, up_ref[...], preferred_element_type=F32)
    u = jnp.square(jnp.maximum(u, 0.0)).astype(BF16)
    x = x + jnp.dot(u, down_ref[...], preferred_element_type=F32)
    if final:
        x = _rms(x, gf_ref[...])
    o_ref[...] = x


def _post(x, mixes, ws, gm, up, down, layer, gf=None, tm=512):
    m = x.shape[0]
    final = gf is not None
    row_spec = lambda n: pl.BlockSpec((tm, n), lambda i: (i, 0))
    layer_spec = lambda w: pl.BlockSpec((None,) + w.shape[1:], lambda i: (layer, 0, 0),
                                        pipeline_mode=pl.Buffered(1))
    rows = [x] + list(mixes)
    args = rows + [*ws, gm.reshape(1, -1), up, down]
    specs = ([row_spec(a.shape[1]) for a in rows] + [_const_spec(w.shape) for w in ws]
             + [_const_spec((1, D_MODEL)), layer_spec(up), layer_spec(down)])
    if final:
        args.append(gf.reshape(1, -1))
        specs.append(_const_spec((1, D_MODEL)))
    return pl.pallas_call(
        functools.partial(_post_kernel, n_mix=len(mixes), final=final),
        grid=(m // tm,), in_specs=specs, out_specs=row_spec(D_MODEL),
        out_shape=jax.ShapeDtypeStruct((m, D_MODEL), F32), compiler_params=_params(1), name="post_mlp",
    )(*args)


def _lora_pad_cols(x):
    parts, start = [], 0
    for n, tiles in zip(LORA_DIMS, LORA_TILES):
        part = x[..., start:start + n]
        parts.append(jnp.pad(part, [(0, 0)] * (part.ndim - 1) + [(0, tiles * LANES - n)]))
        start += n
    return jnp.concatenate(parts, axis=-1)


def _lora_unpad_cols(x):
    parts, start = [], 0
    for n, tiles in zip(LORA_DIMS, LORA_TILES):
        parts.append(x[..., start:start + n])
        start += tiles * LANES
    return jnp.concatenate(parts, axis=-1)


def _prepare(rel_bias, norm_mix, norm_mlp, norm_final, e_w_in, e_mu, e_w0, e_w2, e_a0, e_a2, e_g2, e_k_k, e_k_a,
             e_r_k, e_lnx_w, e_lnx_b, e_w_out, c_w_in, c_lb_raw, c_norm, c_w_out, mlp_up, mlp_down):
    pad_rows = lambda a, n: jnp.pad(a, [(0, n - a.shape[0]), (0, 0)])
    w_in = e_w_in[0]
    return dict(
        rel_bias=rel_bias,
        w_in=w_in.astype(BF16), w_lora=_lora_pad_cols(w_in[:, A_QKV + 3 * B_WIDTH:]).astype(BF16),
        rwkv=dict(mu_rkv=e_mu[0, :3 * B_WIDTH], mu_lora=_lora_pad_cols(e_mu[0, 3 * B_WIDTH:]),
                  w0=e_w0[0], w2=jnp.stack(_split2(pad_rows(e_w2[0], LORA_TILES[0] * LANES))), a0=e_a0[0],
                  a2=pad_rows(e_a2[0], LORA_TILES[1] * LANES).astype(BF16),
                  g2=pad_rows(e_g2[0], LORA_TILES[2] * LANES).astype(BF16),
                  k_k=e_k_k[0], k_a=e_k_a[0], r_k=e_r_k[0].reshape(-1),
                  lnx_w=e_lnx_w[0], lnx_b=e_lnx_b[0]),
        w_out_a=e_w_out[0, :A_OUT].astype(BF16), w_out_b=e_w_out[0, A_OUT:].astype(BF16),
        c_w_in=c_w_in[0].astype(BF16), c_w_out=c_w_out[0].astype(BF16), lb_raw=c_lb_raw, c_norm=c_norm[0],
        norm_mix=norm_mix, norm_mlp=norm_mlp, norm_final=norm_final,
        up=mlp_up.astype(BF16), down=mlp_down.astype(BF16))


def _trunk(x, prm, attend, sh_rkv, sh_lora, wkv0, c0, chunk, t_real):
    bsz, t, _ = x.shape
    flat = lambda a: a.reshape(bsz * t, a.shape[-1])
    xf = flat(x)
    assert A_QKV == 3 * B_WIDTH
    qkv, rkv, lora = _norm_proj(xf, prm["norm_mix"][0],
                                [(prm["w_in"], A_QKV, 0), (prm["w_in"], A_QKV, 1), (prm["w_lora"], LORA_PAD, 0)])
    qkv = qkv.reshape(bsz, t, A_QKV)
    rkv = rkv.reshape(bsz, t, 3 * B_WIDTH)
    lora = lora.reshape(bsz, t, LORA_PAD)
    a_out = attend(qkv)
    b_out, wkv = _rwkv(rkv, lora, sh_rkv, sh_lora, wkv0, prm["rwkv"], chunk, t_real)
    x1 = _post(xf, [flat(a_out), flat(b_out)], [prm["w_out_a"], prm["w_out_b"]], prm["norm_mlp"][0],
               prm["up"], prm["down"], 0)
    (pc,) = _norm_proj(x1, prm["norm_mix"][1], [(prm["c_w_in"], 4 * D_MODEL, 0)])
    c_out, c_state = _gla(pc.reshape(bsz, t, 4 * D_MODEL), c0, prm["lb_raw"], 1, prm["c_norm"], chunk, t_real)
    y = _post(x1, [flat(c_out)], [prm["c_w_out"]], prm["norm_mlp"][1], prm["up"], prm["down"], 1,
              gf=prm["norm_final"])
    kv_rows = []
    for g, (w, _) in enumerate(A_GROUPS):
        n = min(w, t_real)
        rows = qkv[:, t_real - n:t_real]
        kv = jnp.stack([rows[..., (3 * part + g) * A_OUT:(3 * part + g + 1) * A_OUT] for part in (1, 2)], axis=2)
        kv_rows.append(kv.reshape(1, bsz, n, 2, 4, HEAD))
    shift = jnp.concatenate([rkv[:, t_real - 1], _lora_unpad_cols(lora[:, t_real - 1])], axis=-1)[None]
    return y.reshape(bsz, t, D_MODEL)[:, :t_real], kv_rows, shift, wkv[None], c_state[None]


def kernel(x_prompt, x_sample, cache_a0, cache_a1, cache_a2, state_b_shift, state_b_wkv, state_c, rel_bias, norm_mix, norm_mlp, norm_final, e_w_in, e_mu, e_w0, e_w2, e_a0, e_a2, e_g2, e_k_k, e_k_a, e_r_k, e_lnx_w, e_lnx_b, e_w_out, c_w_in, c_lb_raw, c_norm, c_w_out, mlp_up, mlp_down):
    prm = _prepare(rel_bias, norm_mix, norm_mlp, norm_final, e_w_in, e_mu, e_w0, e_w2, e_a0, e_a2, e_g2, e_k_k,
                   e_k_a, e_r_k, e_lnx_w, e_lnx_b, e_w_out, c_w_in, c_lb_raw, c_norm, c_w_out, mlp_up, mlp_down)

    bp, tp, _ = x_prompt.shape

    def attend_prompt(qkv):
        return _attn_prompt(qkv, prm["rel_bias"])

    y_p, p_kv, p_shift, p_wkv, p_c = _trunk(
        x_prompt, prm, attend_prompt,
        jnp.zeros((bp, 3 * B_WIDTH), F32), jnp.zeros((bp, LORA_PAD), F32),
        jnp.zeros((bp, 2 * B_PAIRS, HEAD, HEAD), F32), jnp.zeros((bp, C_HEADS, C_HEAD, C_HEAD), F32),
        chunk=CHUNK, t_real=tp)

    bs, ts, _ = x_sample.shape
    t_pad = SAMPLE_ROWS
    x_s = jnp.pad(x_sample, ((0, 0), (0, t_pad - ts), (0, 0)))
    caches = [jnp.transpose(c[0], (0, 2, 3, 4, 1)) for c in (cache_a0, cache_a1, cache_a2)]

    def attend_sample(qkv):
        return _attn_sample(qkv, caches, prm["rel_bias"], ts)

    sh = state_b_shift[0]
    y_s, s_kv, s_shift, s_wkv, s_c = _trunk(
        x_s, prm, attend_sample, sh[:, :3 * B_WIDTH], _lora_pad_cols(sh[:, 3 * B_WIDTH:]),
        state_b_wkv[0], state_c[0], chunk=t_pad, t_real=ts)

    return (y_p, y_s, p_kv[0], p_kv[1], p_kv[2], p_shift, p_wkv, p_c,
            s_kv[0], s_kv[1], s_kv[2], s_shift, s_wkv, s_c)
```

```python
import functools
import math

import jax
import jax.numpy as jnp
import numpy as np
from jax import lax
from jax.experimental import pallas as pl
from jax.experimental.pallas import tpu as pltpu

F32 = jnp.float32
BF16 = jnp.bfloat16

LANES = 128
D_MODEL = 1024
EPS = 1e-6
LNX_EPS = 64e-5
HEAD = 64
CHUNK = 64
SAMPLE_ROWS = 16
A_GROUPS = ((128, 1), (512, 4), (2048, 16))
A_KEYS = 128
A_TILES = 4
A_BLOCK = A_KEYS * A_GROUPS[-1][1]
A_STRIDE = 4
A_QKV = 2304
A_OUT = 256
N_BUCKETS = 32
BUCKET_MAX_DIST = 2048
B_WIDTH = 768
B_PAIRS = B_WIDTH // LANES
LORA_DIMS = (64, 64, 160)
LORA_TILES = (1, 1, 2)
LORA_PAD = sum(LORA_TILES) * LANES
C_FDIM = 1024
C_HEADS = 8
C_HEAD = 128
NEG = -1e30
LOG2E = math.log2(math.e)
DECAY_SCALE = math.exp(-0.5)
GLA_SUB = 16
EXP_CLAMP = 80.0
MAX_SEQS_PER_STEP = 4

VMEM_LIMIT = 56 * 1024 * 1024
assert 2 * HEAD == LANES and C_HEAD == LANES and A_KEYS == LANES

NN = (((1,), (0,)), ((), ()))
NT = (((1,), (1,)), ((), ()))


def _dg(a, b, dims):
    return lax.dot_general(a, b, dims, preferred_element_type=F32)


def _split2(x):
    hi = x.astype(BF16)
    lo = (x - hi.astype(F32)).astype(BF16)
    return hi, lo


def _split3(x):
    hi = x.astype(BF16)
    r1 = x - hi.astype(F32)
    mid = r1.astype(BF16)
    lo = (r1 - mid.astype(F32)).astype(BF16)
    return hi, mid, lo


def _mm(a, b, dims=NN):
    return _dg(a.astype(BF16), b.astype(BF16), dims)


def _mm_exact_lhs(a_bf16, b, terms=3):
    if terms == 2:
        bh, bl = _split2(b)
        return _dg(a_bf16, bh, NN) + _dg(a_bf16, bl, NN)
    bh, bm, bl = _split3(b)
    return _dg(a_bf16, bh, NN) + (_dg(a_bf16, bm, NN) + _dg(a_bf16, bl, NN))


def _pad_rows_to_tile(x):
    n = x.shape[0]
    if n == LANES:
        return x
    return jnp.concatenate([x, jnp.zeros((LANES - n, LANES), F32)], axis=0)


def _stack_halves(top, bottom):
    n = top.shape[0]
    half = LANES // 2
    if n == half:
        return jnp.concatenate([top, bottom], axis=0)
    pad = jnp.zeros((half - n, LANES), F32)
    return jnp.concatenate([top, pad, bottom, pad], axis=0)


def _sigmoid(x):
    return 0.5 * jnp.tanh(0.5 * x) + 0.5


def _rms(x, g):
    ms = jnp.mean(x * x, axis=-1, keepdims=True)
    return x * lax.rsqrt(ms + EPS) * g


def _const_spec(shape):
    nd = len(shape)
    return pl.BlockSpec(shape, lambda *_: (0,) * nd, pipeline_mode=pl.Buffered(1))


def _params(n_grid):
    return pltpu.CompilerParams(dimension_semantics=("arbitrary",) * n_grid,
                                vmem_limit_bytes=VMEM_LIMIT)


def _norm_proj_kernel(x_ref, g_ref, *refs):
    n = len(refs) // 2
    h = _rms(x_ref[...], g_ref[...]).astype(BF16)
    for w_ref, o_ref in zip(refs[:n], refs[n:]):
        o_ref[...] = jnp.dot(h, w_ref[...], preferred_element_type=F32)


def _norm_proj(x, g, ws, tm=512):
    m = x.shape[0]
    return pl.pallas_call(
        _norm_proj_kernel,
        grid=(m // tm,),
        in_specs=[pl.BlockSpec((tm, D_MODEL), lambda i: (i, 0)), _const_spec((1, D_MODEL))]
        + [pl.BlockSpec((D_MODEL, n), lambda i, j=j: (0, j), pipeline_mode=pl.Buffered(1)) for _, n, j in ws],
        out_specs=[pl.BlockSpec((tm, n), lambda i: (i, 0)) for _, n, _ in ws],
        out_shape=[jax.ShapeDtypeStruct((m, n), F32) for _, n, _ in ws],
        compiler_params=_params(1),
        name="norm_proj",
    )(x, g.reshape(1, D_MODEL), *[w for w, _, _ in ws])


def _t5_bucket_np(dist):
    max_exact = N_BUCKETS // 2
    d = np.maximum(dist, 1).astype(np.float32)
    large = max_exact + (np.log(d / np.float32(max_exact)) / np.float32(math.log(BUCKET_MAX_DIST / max_exact))
                         * np.float32(N_BUCKETS - max_exact)).astype(np.int32)
    large = np.minimum(large, N_BUCKETS - 1)
    return np.where(dist < max_exact, dist, large).astype(np.int32)


def _bias_from_buckets(idx, rb_ref, head):
    acc = jnp.full(idx.shape, NEG, F32)
    for b in range(N_BUCKETS):
        acc = jnp.where(idx == b, rb_ref[b, head], acc)
    return acc


def _tile_rows(start, dil):
    if dil > 1:
        return pl.ds(start, A_KEYS, stride=dil)
    return pl.ds(start if isinstance(start, int) else pl.multiple_of(start, A_KEYS), A_KEYS)


def _attn_prompt_kernel(rb_ref, bkt_ref, *refs):
    ng = len(A_GROUPS)
    q_refs, kc_refs, vc_refs, kp_refs, vp_refs = (refs[i * ng:(i + 1) * ng] for i in range(5))
    out_ref, bias_scr, o_scr, m_scr, d_scr, stage_scr = refs[5 * ng:]
    pair = pl.program_id(2)
    first = (pl.program_id(0) == 0) & (pl.program_id(1) == 0) & (pair == 0)

    @pl.when(first)
    def _():
        for g in range(ng):
            for h in range(4):
                bias_scr[g * 4 + h] = _bias_from_buckets(bkt_ref[g], rb_ref, g * 4 + h) * LOG2E

    col = lax.broadcasted_iota(jnp.int32, (1, 2 * A_KEYS), 1)
    prev_pen = jnp.where((col < A_KEYS) & (pl.program_id(1) == 0), NEG, 0.0).astype(F32)
    h1 = lax.broadcasted_iota(jnp.int32, (1, LANES), 1) < HEAD
    h2 = jnp.logical_not(h1)

    def attend(g, tiles):
        n = range(len(tiles))
        get = lambda src: src[0][src[1], :]
        bias_in = [bias_scr[g * 4 + 2 * pair + j] for j in range(2)]
        bias_edge = [b + prev_pen for b in bias_in]
        q = [get(t[0]) * (HEAD ** -0.5 * LOG2E) for t in tiles]
        k = [jnp.concatenate([get(t[3]), get(t[1])], axis=0).astype(BF16) for t in tiles]
        v = [jnp.concatenate([get(t[4]), get(t[2])], axis=0).astype(BF16) for t in tiles]
        s = [[_dg(jnp.where(mine, q[i], 0.0).astype(BF16), k[i], NT) + (bias_edge if tiles[i][6] else bias_in)[j]
              for j, mine in enumerate((h1, h2))] for i in n]
        m = [[jnp.max(x, axis=-1, keepdims=True) for x in s[i]] for i in n]
        p = [[jnp.exp2(x - mx) for x, mx in zip(s[i], m[i])] for i in n]
        den = [[jnp.sum(x, axis=-1, keepdims=True) for x in p[i]] for i in n]
        o = [[_dg(x.astype(BF16), v[i], NN) for x in p[i]] for i in n]
        for i in n:
            (o_dst, m_dst, d_dst), rows = tiles[i][5]
            o_dst[rows, :] = jnp.where(h1, o[i][0], o[i][1])
            m_dst[rows, :] = jnp.where(h1, m[i][0], m[i][1])
            d_dst[rows, :] = jnp.where(h1, den[i][0], den[i][1])

    n_tiles = A_BLOCK // A_KEYS
    for g, (_, dil) in enumerate(A_GROUPS):
        span = A_KEYS * dil
        single_span = span == A_BLOCK
        cur = (q_refs[g], kc_refs[g], vc_refs[g])
        before = (kp_refs[g], vp_refs[g])
        outs = (o_scr.at[g], m_scr.at[g], d_scr.at[g])

        def batch(it, static, g=g, dil=dil, span=span, single_span=single_span, cur=cur, before=before, outs=outs):
            if dil > A_STRIDE:
                inner = dil // A_STRIDE
                assert single_span and inner == A_TILES and inner <= A_STRIDE
                union = pl.ds(it, A_KEYS * inner, stride=A_STRIDE)
                n_in = len(cur + before)
                for a, ref in enumerate(cur + before):
                    stage_scr[a] = ref[union, :]
                slabs = [stage_scr.at[a] for a in range(n_in + len(outs))]
                attend(g, [tuple((slab, pl.ds(u, A_KEYS, stride=inner)) for slab in slabs[:n_in])
                           + ((tuple(slabs[n_in:]), pl.ds(u, A_KEYS, stride=inner)), True) for u in range(A_TILES)])
                for slab, dst in zip(slabs[n_in:], outs):
                    dst[union, :] = slab[...]
                return
            tiles = []
            for u in range(A_TILES):
                t = it * A_TILES + u
                s_idx, r = (t // dil, t % dil) if static else _span_residue(it, u, dil)
                edge = single_span or (static and s_idx == 0)
                rows = _tile_rows(s_idx * span + r, dil)
                prows = _tile_rows(r if edge else (s_idx - 1) * span + r, dil)
                tiles.append(tuple((ref, rows) for ref in cur)
                             + tuple((ref, prows) for ref in (before if edge else cur[1:])) + ((outs, rows), edge))
            attend(g, tiles)

        batch(0, True)

        def body(it, carry, batch=batch):
            batch(it, False)
            return carry
        lax.fori_loop(1, n_tiles // A_TILES, body, 0)

    def merge(i, carry):
        rows = pl.ds(pl.multiple_of(i * A_KEYS, A_KEYS), A_KEYS)
        ms = [m_scr[g, rows, :] for g in range(ng)]
        top = functools.reduce(jnp.maximum, ms)
        ws = [jnp.exp2(x - top) for x in ms]
        num = functools.reduce(lambda a, b: a + b, [w * o_scr[g, rows, :] for g, w in enumerate(ws)])
        den = functools.reduce(lambda a, b: a + b, [w * d_scr[g, rows, :] for g, w in enumerate(ws)])
        out_ref[rows, :] = num / den
        return carry
    lax.fori_loop(0, n_tiles, merge, 0)


def _span_residue(it, u, dil):
    if dil >= A_TILES:
        per = dil // A_TILES
        return it // per, (it % per) * A_TILES + u
    return it * (A_TILES // dil) + u // dil, u % dil


def _prompt_bucket_map(dil):
    qi = np.arange(A_KEYS)[:, None]
    ki = np.arange(2 * A_KEYS)[None, :]
    j = qi + A_KEYS - ki
    return np.where((j >= 0) & (j <= A_KEYS), _t5_bucket_np(np.clip(j, 0, A_KEYS) * dil), -1).astype(np.int32)


def _attn_prompt(qkv, rel_bias):
    bsz, t, _ = qkv.shape
    ng = len(A_GROUPS)

    def cur(part, g):
        return pl.BlockSpec((None, A_BLOCK, LANES), lambda b, i, pair: (b, i, (part * ng + g) * 2 + pair))

    def before(part, g):
        span = A_KEYS * A_GROUPS[g][1]
        per = A_BLOCK // span
        return pl.BlockSpec((None, span, LANES),
                            lambda b, i, pair: (b, jnp.maximum(i * per - 1, 0), (part * ng + g) * 2 + pair))

    specs = [cur(part, g) for part in range(3) for g in range(ng)] + \
            [before(part, g) for part in (1, 2) for g in range(ng)]
    maps = jnp.asarray(np.stack([_prompt_bucket_map(d) for _, d in A_GROUPS]))
    return pl.pallas_call(
        _attn_prompt_kernel,
        grid=(bsz, t // A_BLOCK, 2),
        in_specs=[pl.BlockSpec(memory_space=pltpu.SMEM), _const_spec(maps.shape)] + specs,
        out_specs=pl.BlockSpec((None, A_BLOCK, LANES), lambda b, i, pair: (b, i, pair)),
        out_shape=jax.ShapeDtypeStruct((bsz, t, A_OUT), F32),
        scratch_shapes=[pltpu.VMEM((4 * ng, A_KEYS, 2 * A_KEYS), F32)] + [pltpu.VMEM((ng, A_BLOCK, LANES), F32)] * 3
        + [pltpu.VMEM((8, A_KEYS * A_TILES, LANES), F32)],
        compiler_params=_params(3),
        name="attn_prompt",
    )(rel_bias, maps, *([qkv] * (5 * ng)))


def _merge_groups(outs, lses):
    m = functools.reduce(jnp.maximum, lses)
    ws = [jnp.exp(l - m) for l in lses]
    num = functools.reduce(lambda a, b: a + b, [w * o for w, o in zip(ws, outs)])
    return num / functools.reduce(lambda a, b: a + b, ws)


def _attn_sample_kernel(rb_ref, bc0, bc1, bc2, bn0, bn1, bn2, qkv_ref, c0_ref, c1_ref, c2_ref, out_ref,
                        bias_c0, bias_c1, bias_c2, bias_n):
    bias_c = (bias_c0, bias_c1, bias_c2)

    @pl.when(pl.program_id(0) == 0)
    def _():
        for g, (bc, bn) in enumerate(((bc0, bn0), (bc1, bn1), (bc2, bn2))):
            for h in range(4):
                bias_c[g][h] = _bias_from_buckets(bc[...], rb_ref, g * 4 + h)
                bias_n[g * 4 + h] = _bias_from_buckets(bn[...], rb_ref, g * 4 + h)

    qkv = qkv_ref[...]
    c_refs = (c0_ref, c1_ref, c2_ref)
    units = [(g, h) for g in range(len(A_GROUPS)) for h in range(4)]
    cols = [(g * 4 + h) * HEAD for g, h in units]
    q = [(qkv[:, c:c + HEAD] * (HEAD ** -0.5)).astype(BF16) for c in cols]
    kn = [qkv[:, 3 * A_OUT + c:3 * A_OUT + c + HEAD].astype(BF16) for c in cols]
    vn = [qkv[:, 6 * A_OUT + c:6 * A_OUT + c + HEAD].astype(BF16) for c in cols]
    n = range(len(units))
    s_c = [_dg(q[i], c_refs[g][0, h].astype(BF16), NN) + bias_c[g][h] for i, (g, h) in enumerate(units)]
    s_n = [_dg(q[i], kn[i], NT) + bias_n[i] for i in n]
    m = [jnp.maximum(jnp.max(s_c[i], axis=-1, keepdims=True), jnp.max(s_n[i], axis=-1, keepdims=True)) for i in n]
    p_c = [jnp.exp(s_c[i] - m[i]) for i in n]
    p_n = [jnp.exp(s_n[i] - m[i]) for i in n]
    den = [jnp.sum(p_c[i], axis=-1, keepdims=True) + jnp.sum(p_n[i], axis=-1, keepdims=True) for i in n]
    o = [(_dg(p_c[i].astype(BF16), c_refs[g][1, h].astype(BF16), NT) + _dg(p_n[i].astype(BF16), vn[i], NN)) / den[i]
         for i, (g, h) in enumerate(units)]
    lse = [m[i] + jnp.log(den[i]) for i in n]
    out_ref[...] = jnp.concatenate([_merge_groups(o[h::4], lse[h::4]) for h in range(4)], axis=-1)


def _sample_bucket_maps(window, dil, tp, t_real):
    t = np.arange(tp)[:, None]
    dist_c = window + t - np.arange(window)[None, :]
    ok_c = (dist_c % dil == 0) & (dist_c // dil <= A_KEYS) & (t < t_real)
    dist_n = t - np.arange(tp)[None, :]
    ok_n = (dist_n >= 0) & (dist_n % dil == 0) & (dist_n // dil <= A_KEYS)
    mc = np.where(ok_c, _t5_bucket_np(np.maximum(dist_c, 0)), -1).astype(np.int32)
    mn = np.where(ok_n, _t5_bucket_np(np.maximum(dist_n, 0)), -1).astype(np.int32)
    return mc, mn


def _attn_sample(qkv, caches, rel_bias, t_real):
    bsz, tp, _ = qkv.shape
    maps = [_sample_bucket_maps(w, d, tp, t_real) for w, d in A_GROUPS]
    mcs = [jnp.asarray(m[0]) for m in maps]
    mns = [jnp.asarray(m[1]) for m in maps]
    return pl.pallas_call(
        _attn_sample_kernel,
        grid=(bsz,),
        in_specs=[pl.BlockSpec(memory_space=pltpu.SMEM)]
        + [_const_spec(m.shape) for m in mcs] + [_const_spec(m.shape) for m in mns]
        + [pl.BlockSpec((None, tp, A_QKV), lambda b: (b, 0, 0))]
        + [pl.BlockSpec((None, 2, 4, HEAD, w), lambda b: (b, 0, 0, 0, 0)) for w, _ in A_GROUPS],
        out_specs=pl.BlockSpec((None, tp, A_OUT), lambda b: (b, 0, 0)),
        out_shape=jax.ShapeDtypeStruct((bsz, tp, A_OUT), F32),
        scratch_shapes=[pltpu.VMEM((4, tp, w), F32) for w, _ in A_GROUPS] + [pltpu.VMEM((12, tp, tp), F32)],
        compiler_params=_params(1),
        name="attn_sample",
    )(rel_bias, *mcs, *mns, qkv, *caches)


def _head_sums(x, h1, fn=lambda s: s):
    first = jnp.sum(jnp.where(h1, x, 0.0), axis=-1, keepdims=True)
    second = jnp.sum(jnp.where(h1, 0.0, x), axis=-1, keepdims=True)
    return jnp.where(h1, fn(first), fn(second))


def _rwkv_kernel(rkv_ref, lora_ref, sh_rkv_ref, sh_lora_ref, s0_ref, mu_rkv_ref, mu_lora_ref, w0_ref, w2_ref,
                 a0_ref, a2_ref, g2_ref, kk_ref, ka_ref, rk_ref, lnw_ref, lnb_ref, tri_ref,
                 out_ref, s_out_ref, s_scr, prev_rkv, prev_lora, *, nb, chunk, t_real):
    c = pl.program_id(1)
    nc = pl.num_programs(1)
    seqs = range(nb)

    @pl.when(c == 0)
    def _():
        zero_blk = jnp.zeros((HEAD, HEAD), F32)
        for s in seqs:
            for p in range(B_PAIRS):
                s_scr[s, p] = jnp.concatenate([jnp.concatenate([s0_ref[s, 2 * p], zero_blk], axis=1),
                                               jnp.concatenate([zero_blk, s0_ref[s, 2 * p + 1]], axis=1)], axis=0)
        prev_rkv[...] = sh_rkv_ref[...]
        prev_lora[...] = sh_lora_ref[...]

    n_rows = nb * chunk
    rsl = [slice(s * chunk, (s + 1) * chunk) for s in seqs]
    rows = lax.broadcasted_iota(jnp.int32, (n_rows, 1), 0)
    pb = rkv_ref[...].reshape(n_rows, 3 * B_WIDTH)
    lr = lora_ref[...].reshape(n_rows, LORA_PAD)
    pb_prev = pltpu.roll(pb, 1, 0)
    lr_prev = pltpu.roll(lr, 1, 0)
    for s in seqs:
        at = rows == s * chunk
        pb_prev = jnp.where(at, prev_rkv[s], pb_prev)
        lr_prev = jnp.where(at, prev_lora[s], lr_prev)
        prev_rkv[s] = pb[(s + 1) * chunk - 1:(s + 1) * chunk, :]
        prev_lora[s] = lr[(s + 1) * chunk - 1:(s + 1) * chunk, :]
    xs = pb + (pb_prev - pb) * mu_rkv_ref[...]
    xl = lr + (lr_prev - lr) * mu_lora_ref[...]
    r = xs[:, :B_WIDTH]
    k = xs[:, B_WIDTH:2 * B_WIDTH]
    v = xs[:, 2 * B_WIDTH:]
    o_aaa, o_gate = LORA_TILES[0] * LANES, (LORA_TILES[0] + LORA_TILES[1]) * LANES
    th_hi, th_lo = _split2(jnp.tanh(xl[:, :o_aaa]))
    z = w0_ref[...] + (_dg(th_hi, w2_ref[0], NN) + (_dg(th_hi, w2_ref[1], NN) + _dg(th_lo, w2_ref[0], NN)))
    lam = -DECAY_SCALE * _sigmoid(z)
    a = _sigmoid(a0_ref[...] + _mm(xl[:, o_aaa:o_gate], a2_ref[...]))
    gate = _mm(_sigmoid(xl[:, o_gate:]), g2_ref[...])
    kk = k * kk_ref[...]
    kmod = k * (1.0 + (a - 1.0) * ka_ref[...])
    if t_real < chunk:
        live = rows % chunk < t_real
        lam = jnp.where(live, lam, 0.0)
        kk = jnp.where(live, kk, 0.0)
        kmod = jnp.where(live, kmod, 0.0)
        v = jnp.where(live, v, 0.0)

    cum = _mm_exact_lhs(tri_ref[...], lam, terms=2)
    cum_end = [cum[(s + 1) * chunk - 1:(s + 1) * chunk, :] for s in seqs]
    e_in = jnp.exp(cum)
    e_prev = jnp.exp(cum - lam)
    e_neg = jnp.exp(-cum)
    e_end = [jnp.exp(cum_end[s] - cum[rsl[s]]) for s in seqs]
    g_end = [jnp.exp(cum_end[s]) for s in seqs]

    ri = lax.broadcasted_iota(jnp.int32, (chunk, LANES), 0)
    ci = lax.broadcasted_iota(jnp.int32, (chunk, LANES), 1) % HEAD
    strict = ci < ri
    lower = ci <= ri
    lane = lax.broadcasted_iota(jnp.int32, (1, LANES), 1)
    h1 = lane < HEAD
    bi = lax.broadcasted_iota(jnp.int32, (LANES, LANES), 0) // HEAD
    bj = lax.broadcasted_iota(jnp.int32, (LANES, LANES), 1) // HEAD
    block_diag = bi == bj
    levels = int(math.log2(chunk))
    zero = jnp.zeros((chunk, LANES), F32)
    units = [(s, p) for s in seqs for p in range(B_PAIRS)]
    idx = range(len(units))
    csl = [slice(p * LANES, (p + 1) * LANES) for _, p in units]
    take = lambda arr: [arr[rsl[s], csl[i]] for i, (s, _) in enumerate(units)]

    kappa = [x * _head_sums(x * x, h1, lambda q: lax.rsqrt(jnp.maximum(q, 1e-24))) for x in take(kk)]
    a_u, r_u, k_u, v_u = take(a), take(r), take(kmod), take(v)
    e_neg_u = take(e_neg)
    bb = [kappa[i] * a_u[i] for i in idx]
    a_t = [-x * e for x, e in zip(kappa, take(e_prev))]
    r_t = [x * e for x, e in zip(r_u, take(e_in))]
    m4 = []
    for i in idx:
        l4 = jnp.concatenate([jnp.where(h1, a_t[i], zero), jnp.where(h1, zero, a_t[i]),
                              jnp.where(h1, r_t[i], zero), jnp.where(h1, zero, r_t[i])], axis=0)
        m4.append(_mm(l4, _stack_halves(bb[i] * e_neg_u[i], k_u[i] * e_neg_u[i]), NT))
    na = [[jnp.where(strict, m4[i][j * chunk:(j + 1) * chunk], 0.0) for j in range(2)] for i in idx]
    nr = [[jnp.where(lower, m4[i][(2 + j) * chunk:(3 + j) * chunk], 0.0) for j in range(2)] for i in idx]
    zv = [_stack_halves(zero, pltpu.roll(v_u[i], HEAD, 1)).astype(BF16) for i in idx]
    zs = [[jnp.where(h1, a_t[i], _mm(na[i][0], zv[i])), jnp.where(h1, _mm(na[i][1], zv[i]), a_t[i])] for i in idx]
    ps = [[na[i][j][:, :chunk].astype(BF16) for j in range(2)] for i in idx]
    for lvl in range(levels):
        last = lvl == levels - 1
        for i in idx:
            for j in range(2):
                z_b = zs[i][j].astype(BF16)
                upd = _dg(ps[i][j], z_b if last else jnp.concatenate([z_b, ps[i][j]], axis=-1), NN)
                zs[i][j] = zs[i][j] + upd[:, :LANES]
                if not last:
                    ps[i][j] = upd[:, LANES:].astype(BF16)
    ta = [jnp.where(h1, zs[i][0], zs[i][1]) for i in idx]
    pv = [pltpu.roll(jnp.where(h1, zs[i][1], zs[i][0]), HEAD, 1) for i in idx]
    s_old = [s_scr[s, p] for s, p in units]
    s_b = [x.astype(BF16) for x in s_old]
    u = [_mm(ta[i], s_b[i], NT) + pv[i] for i in idx]
    uv = [_stack_halves(u[i], v_u[i]) for i in idx]
    uv_b = [x.astype(BF16) for x in uv]
    y = [_mm(r_t[i], s_b[i], NT) + jnp.where(h1, _mm(nr[i][0], uv_b[i]), _mm(nr[i][1], uv_b[i])) for i in idx]
    for i, (s, p) in enumerate(units):
        e = e_end[s][:, csl[i]]
        s_new = s_old[i] * g_end[s][:, csl[i]] + _mm(uv[i].T, _stack_halves(bb[i] * e, k_u[i] * e))
        s_scr[s, p] = jnp.where(block_diag, s_new, 0.0)

    mean = [_head_sums(y[i], h1) * (1.0 / HEAD) for i in idx]
    dlt = [y[i] - mean[i] for i in idx]
    var = [_head_sums(d * d, h1) * (1.0 / HEAD) for d in dlt]
    gate_u = take(gate)
    for i, (s, p) in enumerate(units):
        sl = csl[i]
        bonus = _head_sums(r_u[i] * k_u[i] * rk_ref[:, sl], h1)
        yn = dlt[i] * lax.rsqrt(var[i] + LNX_EPS) * lnw_ref[:, sl] + lnb_ref[:, sl]
        out_ref[s, :, sl] = (yn + bonus * v_u[i]) * gate_u[i]

    @pl.when(c == nc - 1)
    def _():
        for s in seqs:
            for p in range(B_PAIRS):
                s_pair = s_scr[s, p]
                s_out_ref[s, 2 * p] = s_pair[:HEAD, :HEAD]
                s_out_ref[s, 2 * p + 1] = s_pair[HEAD:, HEAD:]


def _rwkv(rkv, lora, sh_rkv, sh_lora, s0, prm, chunk, t_real):
    bsz, t, _ = rkv.shape
    nc = t // chunk
    nb = min(bsz, MAX_SEQS_PER_STEP)
    tri = np.kron(np.eye(nb), np.tril(np.ones((chunk, chunk)))).astype(np.float32)
    vec = lambda n: _const_spec((1, n))
    row = lambda x: x.reshape(1, -1)
    out, s_out = pl.pallas_call(
        functools.partial(_rwkv_kernel, nb=nb, chunk=chunk, t_real=t_real),
        grid=(bsz // nb, nc),
        in_specs=[pl.BlockSpec((nb, chunk, 3 * B_WIDTH), lambda b, c: (b, c, 0)),
                  pl.BlockSpec((nb, chunk, LORA_PAD), lambda b, c: (b, c, 0)),
                  pl.BlockSpec((nb, 1, 3 * B_WIDTH), lambda b, c: (b, 0, 0)),
                  pl.BlockSpec((nb, 1, LORA_PAD), lambda b, c: (b, 0, 0)),
                  pl.BlockSpec((nb, 2 * B_PAIRS, HEAD, HEAD), lambda b, c: (b, 0, 0, 0)),
                  vec(3 * B_WIDTH), vec(LORA_PAD), vec(B_WIDTH), _const_spec(prm["w2"].shape),
                  vec(B_WIDTH), _const_spec(prm["a2"].shape), _const_spec(prm["g2"].shape),
                  vec(B_WIDTH), vec(B_WIDTH), vec(B_WIDTH), vec(B_WIDTH), vec(B_WIDTH),
                  _const_spec(tri.shape)],
        out_specs=[pl.BlockSpec((nb, chunk, B_WIDTH), lambda b, c: (b, c, 0)),
                   pl.BlockSpec((nb, 2 * B_PAIRS, HEAD, HEAD), lambda b, c: (b, 0, 0, 0))],
        out_shape=[jax.ShapeDtypeStruct((bsz, t, B_WIDTH), F32),
                   jax.ShapeDtypeStruct((bsz, 2 * B_PAIRS, HEAD, HEAD), F32)],
        scratch_shapes=[pltpu.VMEM((nb, B_PAIRS, LANES, LANES), F32), pltpu.VMEM((nb, 1, 3 * B_WIDTH), F32),
                        pltpu.VMEM((nb, 1, LORA_PAD), F32)],
        compiler_params=_params(2),
        name="rwkv7",
    )(rkv, lora, sh_rkv[:, None], sh_lora[:, None], s0,
      row(prm["mu_rkv"]), row(prm["mu_lora"]), row(prm["w0"]), prm["w2"], row(prm["a0"]), prm["a2"], prm["g2"],
      row(prm["k_k"]), row(prm["k_a"]), row(prm["r_k"]), row(prm["lnx_w"]), row(prm["lnx_b"]),
      jnp.asarray(tri, BF16))
    return out, s_out


def _gla_kernel(pc_ref, s0_ref, lb_ref, gn_ref, sums_ref, out_ref, s_out_ref, s_scr, *, nb, chunk, t_real, layer):
    c = pl.program_id(1)
    nc = pl.num_programs(1)
    seqs = range(nb)

    @pl.when(c == 0)
    def _():
        for s in seqs:
            for h in range(C_HEADS):
                s_scr[s, h] = s0_ref[s, h].T

    n_rows = nb * chunk
    rsl = [slice(s * chunk, (s + 1) * chunk) for s in seqs]
    pc = pc_ref[...].reshape(n_rows, 4 * D_MODEL)
    raw = lb_ref[...]
    e = jnp.exp(raw - jnp.max(raw, axis=0, keepdims=True))
    sm = e / jnp.sum(e, axis=0, keepdims=True)
    lb = jnp.sum(sm[:layer + 1], axis=0, keepdims=True) - sm[0:1]
    xq = pc[:, :C_FDIM]
    q = xq * _sigmoid(xq)
    fg = lb + (1.0 - lb) * _sigmoid(pc[:, C_FDIM:2 * C_FDIM])
    k = 1.0 - fg
    logf = jnp.log(fg)
    v = pc[:, 2 * C_FDIM:2 * C_FDIM + D_MODEL]
    xg = pc[:, 2 * C_FDIM + D_MODEL:]
    if t_real < chunk:
        live = lax.broadcasted_iota(jnp.int32, (n_rows, 1), 0) % chunk < t_real
        logf = jnp.where(live, logf, 0.0)
        k = jnp.where(live, k, 0.0)
        v = jnp.where(live, v, 0.0)

    sums = _mm_exact_lhs(sums_ref[...], logf)
    cum = sums[:n_rows]
    base = sums[n_rows:]
    q_in = q * jnp.exp(cum)
    q_loc = q * jnp.exp(cum - base)
    k_loc = k * jnp.exp(jnp.minimum(base - cum, EXP_CLAMP))
    nsub = chunk // GLA_SUB
    k_stack, v_stack, k_end, g_end = [], [], [], []
    for s in seqs:
        cum_s, k_s = cum[rsl[s]], k[rsl[s]]
        cum_end = cum_s[chunk - 1:chunk, :]
        k_end.append(k_s * jnp.exp(cum_end - cum_s))
        g_end.append(jnp.exp(cum_end))
        k_var = [k_loc[rsl[s]]]
        for i in range(1, nsub):
            ref_i = cum_s[i * GLA_SUB - 1:i * GLA_SUB, :]
            k_var.append(k_s * jnp.exp(jnp.minimum(ref_i - cum_s, 0.0)))
        k_stack.append(jnp.concatenate(k_var, axis=0) if nsub > 1 else k_var[0])
        v_stack.append(jnp.concatenate([v[rsl[s]]] * nsub, axis=0) if nsub > 1 else v[rsl[s]])

    ri = lax.broadcasted_iota(jnp.int32, (chunk, nsub * chunk), 0)
    cc = lax.broadcasted_iota(jnp.int32, (chunk, nsub * chunk), 1)
    var = cc // chunk
    ci = cc % chunk
    same_sub = ci // GLA_SUB == ri // GLA_SUB
    att_mask = ((var == 0) & same_sub & (ci <= ri)) | ((ri // GLA_SUB == var) & (ci < var * GLA_SUB))

    units = [(s, h) for s in seqs for h in range(C_HEADS)]
    idx = range(len(units))
    csl = [slice(h * C_HEAD, (h + 1) * C_HEAD) for _, h in units]
    att = [jnp.where(att_mask, _mm(q_loc[rsl[s], csl[i]], k_stack[s][:, csl[i]], NT), 0.0)
           for i, (s, _) in enumerate(units)]
    s_old = [s_scr[s, h] for s, h in units]
    outs = [_mm(q_in[rsl[s], csl[i]], s_old[i], NT) + _mm(att[i], v_stack[s][:, csl[i]])
            for i, (s, _) in enumerate(units)]
    for i, (s, h) in enumerate(units):
        sl = csl[i]
        s_scr[s, h] = s_old[i] * g_end[s][:, sl] + _mm(_pad_rows_to_tile(v[rsl[s], sl]).T,
                                                       _pad_rows_to_tile(k_end[s][:, sl]))
    o = jnp.concatenate([jnp.concatenate(outs[s * C_HEADS:(s + 1) * C_HEADS], axis=-1) for s in seqs], axis=0)
    out_ref[...] = (_rms(o, gn_ref[...]) * (xg * _sigmoid(xg))).reshape(nb, chunk, D_MODEL)

    @pl.when(c == nc - 1)
    def _():
        for s in seqs:
            for h in range(C_HEADS):
                s_out_ref[s, h] = s_scr[s, h].T


def _gla(pc, s0, lb_raw, layer, gn, chunk, t_real):
    bsz, t, _ = pc.shape
    nc = t // chunk
    nb = min(bsz, MAX_SEQS_PER_STEP)
    idx = np.arange(chunk)
    eye = np.eye(nb)
    tri = np.kron(eye, np.tril(np.ones((chunk, chunk)))).astype(np.float32)
    sel = np.kron(eye, idx[None, :] < (idx[:, None] // GLA_SUB) * GLA_SUB).astype(np.float32)
    return pl.pallas_call(
        functools.partial(_gla_kernel, nb=nb, chunk=chunk, t_real=t_real, layer=layer),
        grid=(bsz // nb, nc),
        in_specs=[pl.BlockSpec((nb, chunk, 4 * D_MODEL), lambda b, c: (b, c, 0)),
                  pl.BlockSpec((nb, C_HEADS, C_HEAD, C_HEAD), lambda b, c: (b, 0, 0, 0)),
                  _const_spec(lb_raw.shape), _const_spec((1, D_MODEL)),
                  _const_spec((2 * nb * chunk, nb * chunk))],
        out_specs=[pl.BlockSpec((nb, chunk, D_MODEL), lambda b, c: (b, c, 0)),
                   pl.BlockSpec((nb, C_HEADS, C_HEAD, C_HEAD), lambda b, c: (b, 0, 0, 0))],
        out_shape=[jax.ShapeDtypeStruct((bsz, t, D_MODEL), F32),
                   jax.ShapeDtypeStruct((bsz, C_HEADS, C_HEAD, C_HEAD), F32)],
        scratch_shapes=[pltpu.VMEM((nb, C_HEADS, C_HEAD, C_HEAD), F32)],
        compiler_params=_params(2),
        name="hgrn2",
    )(pc, s0, lb_raw, gn.reshape(1, -1), jnp.asarray(np.concatenate([tri, sel]), BF16))


def _post_kernel(*refs, n_mix, final):
    x_ref = refs[0]
    mix_refs = refs[1:1 + n_mix]
    w_refs = refs[1 + n_mix:1 + 2 * n_mix]
    gm_ref, up_ref, down_ref = refs[1 + 2 * n_mix:4 + 2 * n_mix]
    gf_ref = refs[4 + 2 * n_mix] if final else None
    o_ref = refs[-1]
    x = x_ref[...]
    for m_ref, w_ref in zip(mix_refs, w_refs):
        x = x + jnp.dot(m_ref[...].astype(BF16), w_ref[...], preferred_element_type=F32)
    h = _rms(x, gm_ref[...]).astype(BF16)
    u = jnp.dot(h, up_ref[...], preferred_element_type=F32)
    u = jnp.square(jnp.maximum(u, 0.0)).astype(BF16)
    x = x + jnp.dot(u, down_ref[...], preferred_element_type=F32)
    if final:
        x = _rms(x, gf_ref[...])
    o_ref[...] = x


def _post(x, mixes, ws, gm, up, down, layer, gf=None, tm=512):
    m = x.shape[0]
    final = gf is not None
    row_spec = lambda n: pl.BlockSpec((tm, n), lambda i: (i, 0))
    layer_spec = lambda w: pl.BlockSpec((None,) + w.shape[1:], lambda i: (layer, 0, 0),
                                        pipeline_mode=pl.Buffered(1))
    rows = [x] + list(mixes)
    args = rows + [*ws, gm.reshape(1, -1), up, down]
    specs = ([row_spec(a.shape[1]) for a in rows] + [_const_spec(w.shape) for w in ws]
             + [_const_spec((1, D_MODEL)), layer_spec(up), layer_spec(down)])
    if final:
        args.append(gf.reshape(1, -1))
        specs.append(_const_spec((1, D_MODEL)))
    return pl.pallas_call(
        functools.partial(_post_kernel, n_mix=len(mixes), final=final),
        grid=(m // tm,), in_specs=specs, out_specs=row_spec(D_MODEL),
        out_shape=jax.ShapeDtypeStruct((m, D_MODEL), F32), compiler_params=_params(1), name="post_mlp",
    )(*args)


def _lora_pad_cols(x):
    parts, start = [], 0
    for n, tiles in zip(LORA_DIMS, LORA_TILES):
        part = x[..., start:start + n]
        parts.append(jnp.pad(part, [(0, 0)] * (part.ndim - 1) + [(0, tiles * LANES - n)]))
        start += n
    return jnp.concatenate(parts, axis=-1)


def _lora_unpad_cols(x):
    parts, start = [], 0
    for n, tiles in zip(LORA_DIMS, LORA_TILES):
        parts.append(x[..., start:start + n])
        start += tiles * LANES
    return jnp.concatenate(parts, axis=-1)


def _prepare(rel_bias, norm_mix, norm_mlp, norm_final, e_w_in, e_mu, e_w0, e_w2, e_a0, e_a2, e_g2, e_k_k, e_k_a,
             e_r_k, e_lnx_w, e_lnx_b, e_w_out, c_w_in, c_lb_raw, c_norm, c_w_out, mlp_up, mlp_down):
    pad_rows = lambda a, n: jnp.pad(a, [(0, n - a.shape[0]), (0, 0)])
    w_in = e_w_in[0]
    return dict(
        rel_bias=rel_bias,
        w_in=w_in.astype(BF16), w_lora=_lora_pad_cols(w_in[:, A_QKV + 3 * B_WIDTH:]).astype(BF16),
        rwkv=dict(mu_rkv=e_mu[0, :3 * B_WIDTH], mu_lora=_lora_pad_cols(e_mu[0, 3 * B_WIDTH:]),
                  w0=e_w0[0], w2=jnp.stack(_split2(pad_rows(e_w2[0], LORA_TILES[0] * LANES))), a0=e_a0[0],
                  a2=pad_rows(e_a2[0], LORA_TILES[1] * LANES).astype(BF16),
                  g2=pad_rows(e_g2[0], LORA_TILES[2] * LANES).astype(BF16),
                  k_k=e_k_k[0], k_a=e_k_a[0], r_k=e_r_k[0].reshape(-1),
                  lnx_w=e_lnx_w[0], lnx_b=e_lnx_b[0]),
        w_out_a=e_w_out[0, :A_OUT].astype(BF16), w_out_b=e_w_out[0, A_OUT:].astype(BF16),
        c_w_in=c_w_in[0].astype(BF16), c_w_out=c_w_out[0].astype(BF16), lb_raw=c_lb_raw, c_norm=c_norm[0],
        norm_mix=norm_mix, norm_mlp=norm_mlp, norm_final=norm_final,
        up=mlp_up.astype(BF16), down=mlp_down.astype(BF16))


def _trunk(x, prm, attend, sh_rkv, sh_lora, wkv0, c0, chunk, t_real):
    bsz, t, _ = x.shape
    flat = lambda a: a.reshape(bsz * t, a.shape[-1])
    xf = flat(x)
    assert A_QKV == 3 * B_WIDTH
    qkv, rkv, lora = _norm_proj(xf, prm["norm_mix"][0],
                                [(prm["w_in"], A_QKV, 0), (prm["w_in"], A_QKV, 1), (prm["w_lora"], LORA_PAD, 0)])
    qkv = qkv.reshape(bsz, t, A_QKV)
    rkv = rkv.reshape(bsz, t, 3 * B_WIDTH)
    lora = lora.reshape(bsz, t, LORA_PAD)
    a_out = attend(qkv)
    b_out, wkv = _rwkv(rkv, lora, sh_rkv, sh_lora, wkv0, prm["rwkv"], chunk, t_real)
    x1 = _post(xf, [flat(a_out), flat(b_out)], [prm["w_out_a"], prm["w_out_b"]], prm["norm_mlp"][0],
               prm["up"], prm["down"], 0)
    (pc,) = _norm_proj(x1, prm["norm_mix"][1], [(prm["c_w_in"], 4 * D_MODEL, 0)])
    c_out, c_state = _gla(pc.reshape(bsz, t, 4 * D_MODEL), c0, prm["lb_raw"], 1, prm["c_norm"], chunk, t_real)
    y = _post(x1, [flat(c_out)], [prm["c_w_out"]], prm["norm_mlp"][1], prm["up"], prm["down"], 1,
              gf=prm["norm_final"])
    kv_rows = []
    for g, (w, _) in enumerate(A_GROUPS):
        n = min(w, t_real)
        rows = qkv[:, t_real - n:t_real]
        kv = jnp.stack([rows[..., (3 * part + g) * A_OUT:(3 * part + g + 1) * A_OUT] for part in (1, 2)], axis=2)
        kv_rows.append(kv.reshape(1, bsz, n, 2, 4, HEAD))
    shift = jnp.concatenate([rkv[:, t_real - 1], _lora_unpad_cols(lora[:, t_real - 1])], axis=-1)[None]
    return y.reshape(bsz, t, D_MODEL)[:, :t_real], kv_rows, shift, wkv[None], c_state[None]


def kernel(x_prompt, x_sample, cache_a0, cache_a1, cache_a2, state_b_shift, state_b_wkv, state_c, rel_bias, norm_mix, norm_mlp, norm_final, e_w_in, e_mu, e_w0, e_w2, e_a0, e_a2, e_g2, e_k_k, e_k_a, e_r_k, e_lnx_w, e_lnx_b, e_w_out, c_w_in, c_lb_raw, c_norm, c_w_out, mlp_up, mlp_down):
    prm = _prepare(rel_bias, norm_mix, norm_mlp, norm_final, e_w_in, e_mu, e_w0, e_w2, e_a0, e_a2, e_g2, e_k_k,
                   e_k_a, e_r_k, e_lnx_w, e_lnx_b, e_w_out, c_w_in, c_lb_raw, c_norm, c_w_out, mlp_up, mlp_down)

    bp, tp, _ = x_prompt.shape

    def attend_prompt(qkv):
        return _attn_prompt(qkv, prm["rel_bias"])

    y_p, p_kv, p_shift, p_wkv, p_c = _trunk(
        x_prompt, prm, attend_prompt,
        jnp.zeros((bp, 3 * B_WIDTH), F32), jnp.zeros((bp, LORA_PAD), F32),
        jnp.zeros((bp, 2 * B_PAIRS, HEAD, HEAD), F32), jnp.zeros((bp, C_HEADS, C_HEAD, C_HEAD), F32),
        chunk=CHUNK, t_real=tp)

    bs, ts, _ = x_sample.shape
    t_pad = SAMPLE_ROWS
    x_s = jnp.pad(x_sample, ((0, 0), (0, t_pad - ts), (0, 0)))
    caches = [jnp.transpose(c[0], (0, 2, 3, 4, 1)) for c in (cache_a0, cache_a1, cache_a2)]

    def attend_sample(qkv):
        return _attn_sample(qkv, caches, prm["rel_bias"], ts)

    sh = state_b_shift[0]
    y_s, s_kv, s_shift, s_wkv, s_c = _trunk(
        x_s, prm, attend_sample, sh[:, :3 * B_WIDTH], _lora_pad_cols(sh[:, 3 * B_WIDTH:]),
        state_b_wkv[0], state_c[0], chunk=t_pad, t_real=ts)

    return (y_p, y_s, p_kv[0], p_kv[1], p_kv[2], p_shift, p_wkv, p_c,
            s_kv[0], s_kv[1], s_kv[2], s_shift, s_wkv, s_c)
```

```python
import functools
import math

import jax
import jax.numpy as jnp
import numpy as np
from jax import lax
from jax.experimental import pallas as pl
from jax.experimental.pallas import tpu as pltpu

F32 = jnp.float32
BF16 = jnp.bfloat16

LANES = 128
D_MODEL = 1024
EPS = 1e-6
LNX_EPS = 64e-5
HEAD = 64
CHUNK = 64
SAMPLE_ROWS = 16
A_GROUPS = ((128, 1), (512, 4), (2048, 16))
A_KEYS = 128
A_TILES = 4
A_BLOCK = A_KEYS * A_GROUPS[-1][1]
A_STRIDE = 4
A_QKV = 2304
A_OUT = 256
N_BUCKETS = 32
BUCKET_MAX_DIST = 2048
B_WIDTH = 768
B_PAIRS = B_WIDTH // LANES
LORA_DIMS = (64, 64, 160)
LORA_TILES = (1, 1, 2)
LORA_PAD = sum(LORA_TILES) * LANES
C_FDIM = 1024
C_HEADS = 8
C_HEAD = 128
NEG = -1e30
LOG2E = math.log2(math.e)
DECAY_SCALE = math.exp(-0.5)
GLA_SUB = 16
EXP_CLAMP = 80.0
MAX_SEQS_PER_STEP = 4

VMEM_LIMIT = 56 * 1024 * 1024
assert 2 * HEAD == LANES and C_HEAD == LANES and A_KEYS == LANES

NN = (((1,), (0,)), ((), ()))
NT = (((1,), (1,)), ((), ()))


def _dg(a, b, dims):
    return lax.dot_general(a, b, dims, preferred_element_type=F32)


def _split2(x):
    hi = x.astype(BF16)
    lo = (x - hi.astype(F32)).astype(BF16)
    return hi, lo


def _split3(x):
    hi = x.astype(BF16)
    r1 = x - hi.astype(F32)
    mid = r1.astype(BF16)
    lo = (r1 - mid.astype(F32)).astype(BF16)
    return hi, mid, lo


def _mm(a, b, dims=NN):
    return _dg(a.astype(BF16), b.astype(BF16), dims)


def _mm_exact_lhs(a_bf16, b, terms=3):
    if terms == 2:
        bh, bl = _split2(b)
        return _dg(a_bf16, bh, NN) + _dg(a_bf16, bl, NN)
    bh, bm, bl = _split3(b)
    return _dg(a_bf16, bh, NN) + (_dg(a_bf16, bm, NN) + _dg(a_bf16, bl, NN))


def _pad_rows_to_tile(x):
    n = x.shape[0]
    if n == LANES:
        return x
    return jnp.concatenate([x, jnp.zeros((LANES - n, LANES), F32)], axis=0)


def _stack_halves(top, bottom):
    n = top.shape[0]
    half = LANES // 2
    if n == half:
        return jnp.concatenate([top, bottom], axis=0)
    pad = jnp.zeros((half - n, LANES), F32)
    return jnp.concatenate([top, pad, bottom, pad], axis=0)


def _sigmoid(x):
    return 0.5 * jnp.tanh(0.5 * x) + 0.5


def _rms(x, g):
    ms = jnp.mean(x * x, axis=-1, keepdims=True)
    return x * lax.rsqrt(ms + EPS) * g


def _const_spec(shape):
    nd = len(shape)
    return pl.BlockSpec(shape, lambda *_: (0,) * nd, pipeline_mode=pl.Buffered(1))


def _params(n_grid):
    return pltpu.CompilerParams(dimension_semantics=("arbitrary",) * n_grid,
                                vmem_limit_bytes=VMEM_LIMIT)


def _norm_proj_kernel(x_ref, g_ref, *refs, tiled):
    n = len(refs) // 2
    h = _rms(x_ref[...], g_ref[...]).astype(BF16)
    for w_ref, o_ref, by_tile in zip(refs[:n], refs[n:], tiled):
        res = jnp.dot(h, w_ref[...], preferred_element_type=F32)
        if by_tile:
            for c in range(o_ref.shape[0]):
                o_ref[c] = res[:, c * LANES:(c + 1) * LANES]
        else:
            o_ref[...] = res


def _norm_proj(x, g, ws, tiled=(), tm=512):
    m = x.shape[0]
    by_tile = [i in tiled for i in range(len(ws))]
    out_specs = [pl.BlockSpec((n // LANES, tm, LANES), lambda i: (0, i, 0)) if t else
                 pl.BlockSpec((tm, n), lambda i: (i, 0)) for (_, n, _), t in zip(ws, by_tile)]
    out_shape = [jax.ShapeDtypeStruct((n // LANES, m, LANES) if t else (m, n), F32)
                 for (_, n, _), t in zip(ws, by_tile)]
    return pl.pallas_call(
        functools.partial(_norm_proj_kernel, tiled=tuple(by_tile)),
        grid=(m // tm,),
        in_specs=[pl.BlockSpec((tm, D_MODEL), lambda i: (i, 0)), _const_spec((1, D_MODEL))]
        + [pl.BlockSpec((D_MODEL, n), lambda i, j=j: (0, j), pipeline_mode=pl.Buffered(1)) for _, n, j in ws],
        out_specs=out_specs,
        out_shape=out_shape,
        compiler_params=_params(1),
        name="norm_proj",
    )(x, g.reshape(1, D_MODEL), *[w for w, _, _ in ws])


def _t5_bucket_np(dist):
    max_exact = N_BUCKETS // 2
    d = np.maximum(dist, 1).astype(np.float32)
    large = max_exact + (np.log(d / np.float32(max_exact)) / np.float32(math.log(BUCKET_MAX_DIST / max_exact))
                         * np.float32(N_BUCKETS - max_exact)).astype(np.int32)
    large = np.minimum(large, N_BUCKETS - 1)
    return np.where(dist < max_exact, dist, large).astype(np.int32)


def _bias_from_buckets(idx, rb_ref, head):
    acc = jnp.full(idx.shape, NEG, F32)
    for b in range(N_BUCKETS):
        acc = jnp.where(idx == b, rb_ref[b, head], acc)
    return acc


def _tile_rows(start, dil):
    if dil > 1:
        return pl.ds(start, A_KEYS, stride=dil)
    return pl.ds(start if isinstance(start, int) else pl.multiple_of(start, A_KEYS), A_KEYS)


def _attn_prompt_kernel(rb_ref, bkt_ref, *refs):
    ng = len(A_GROUPS)
    q_refs, kc_refs, vc_refs, kp_refs, vp_refs = (refs[i * ng:(i + 1) * ng] for i in range(5))
    out_ref, bias_scr, o_scr, m_scr, d_scr, stage_scr = refs[5 * ng:]
    pair = pl.program_id(2)
    first = (pl.program_id(0) == 0) & (pl.program_id(1) == 0) & (pair == 0)

    @pl.when(first)
    def _():
        for g in range(ng):
            for h in range(4):
                bias_scr[g * 4 + h] = _bias_from_buckets(bkt_ref[g], rb_ref, g * 4 + h) * LOG2E

    col = lax.broadcasted_iota(jnp.int32, (1, 2 * A_KEYS), 1)
    prev_pen = jnp.where((col < A_KEYS) & (pl.program_id(1) == 0), NEG, 0.0).astype(F32)
    h1 = lax.broadcasted_iota(jnp.int32, (1, LANES), 1) < HEAD
    h2 = jnp.logical_not(h1)

    def attend(g, tiles):
        n = range(len(tiles))
        get = lambda src: src[0][src[1], :]
        bias_in = [bias_scr[g * 4 + 2 * pair + j] for j in range(2)]
        bias_edge = [b + prev_pen for b in bias_in]
        q = [get(t[0]) * (HEAD ** -0.5 * LOG2E) for t in tiles]
        k = [jnp.concatenate([get(t[3]), get(t[1])], axis=0).astype(BF16) for t in tiles]
        v = [jnp.concatenate([get(t[4]), get(t[2])], axis=0).astype(BF16) for t in tiles]
        s = [[_dg(jnp.where(mine, q[i], 0.0).astype(BF16), k[i], NT) + (bias_edge if tiles[i][6] else bias_in)[j]
              for j, mine in enumerate((h1, h2))] for i in n]
        m = [[jnp.max(x, axis=-1, keepdims=True) for x in s[i]] for i in n]
        p = [[jnp.exp2(x - mx) for x, mx in zip(s[i], m[i])] for i in n]
        den = [[jnp.sum(x, axis=-1, keepdims=True) for x in p[i]] for i in n]
        o = [[_dg(x.astype(BF16), v[i], NN) for x in p[i]] for i in n]
        for i in n:
            (o_dst, m_dst, d_dst), rows = tiles[i][5]
            o_dst[rows, :] = jnp.where(h1, o[i][0], o[i][1])
            m_dst[rows, :] = jnp.where(h1, m[i][0], m[i][1])
            d_dst[rows, :] = jnp.where(h1, den[i][0], den[i][1])

    n_tiles = A_BLOCK // A_KEYS
    for g, (_, dil) in enumerate(A_GROUPS):
        span = A_KEYS * dil
        single_span = span == A_BLOCK
        cur = (q_refs[g], kc_refs[g], vc_refs[g])
        before = (kp_refs[g], vp_refs[g])
        outs = (o_scr.at[g], m_scr.at[g], d_scr.at[g])

        def batch(it, static, g=g, dil=dil, span=span, single_span=single_span, cur=cur, before=before, outs=outs):
            if dil > A_STRIDE:
                inner = dil // A_STRIDE
                assert single_span and inner == A_TILES and inner <= A_STRIDE
                union = pl.ds(it, A_KEYS * inner, stride=A_STRIDE)
                n_in = len(cur + before)
                for a, ref in enumerate(cur + before):
                    stage_scr[a] = ref[union, :]
                slabs = [stage_scr.at[a] for a in range(n_in + len(outs))]
                attend(g, [tuple((slab, pl.ds(u, A_KEYS, stride=inner)) for slab in slabs[:n_in])
                           + ((tuple(slabs[n_in:]), pl.ds(u, A_KEYS, stride=inner)), True) for u in range(A_TILES)])
                for slab, dst in zip(slabs[n_in:], outs):
                    dst[union, :] = slab[...]
                return
            tiles = []
            for u in range(A_TILES):
                t = it * A_TILES + u
                s_idx, r = (t // dil, t % dil) if static else _span_residue(it, u, dil)
                edge = single_span or (static and s_idx == 0)
                rows = _tile_rows(s_idx * span + r, dil)
                prows = _tile_rows(r if edge else (s_idx - 1) * span + r, dil)
                tiles.append(tuple((ref, rows) for ref in cur)
                             + tuple((ref, prows) for ref in (before if edge else cur[1:])) + ((outs, rows), edge))
            attend(g, tiles)

        batch(0, True)

        def body(it, carry, batch=batch):
            batch(it, False)
            return carry
        lax.fori_loop(1, n_tiles // A_TILES, body, 0)

    def merge(i, carry):
        rows = pl.ds(pl.multiple_of(i * A_KEYS, A_KEYS), A_KEYS)
        ms = [m_scr[g, rows, :] for g in range(ng)]
        top = functools.reduce(jnp.maximum, ms)
        ws = [jnp.exp2(x - top) for x in ms]
        num = functools.reduce(lambda a, b: a + b, [w * o_scr[g, rows, :] for g, w in enumerate(ws)])
        den = functools.reduce(lambda a, b: a + b, [w * d_scr[g, rows, :] for g, w in enumerate(ws)])
        out_ref[rows, :] = num / den
        return carry
    lax.fori_loop(0, n_tiles, merge, 0)


def _span_residue(it, u, dil):
    if dil >= A_TILES:
        per = dil // A_TILES
        return it // per, (it % per) * A_TILES + u
    return it * (A_TILES // dil) + u // dil, u % dil


def _prompt_bucket_map(dil):
    qi = np.arange(A_KEYS)[:, None]
    ki = np.arange(2 * A_KEYS)[None, :]
    j = qi + A_KEYS - ki
    return np.where((j >= 0) & (j <= A_KEYS), _t5_bucket_np(np.clip(j, 0, A_KEYS) * dil), -1).astype(np.int32)


def _attn_prompt(qkv, rel_bias):
    _, bsz, t, _ = qkv.shape
    ng = len(A_GROUPS)

    def cur(part, g):
        return pl.BlockSpec((None, None, A_BLOCK, LANES), lambda b, i, pair: ((part * ng + g) * 2 + pair, b, i, 0))

    def before(part, g):
        span = A_KEYS * A_GROUPS[g][1]
        per = A_BLOCK // span
        return pl.BlockSpec((None, None, span, LANES),
                            lambda b, i, pair: ((part * ng + g) * 2 + pair, b, jnp.maximum(i * per - 1, 0), 0))

    specs = [cur(part, g) for part in range(3) for g in range(ng)] + \
            [before(part, g) for part in (1, 2) for g in range(ng)]
    maps = jnp.asarray(np.stack([_prompt_bucket_map(d) for _, d in A_GROUPS]))
    return pl.pallas_call(
        _attn_prompt_kernel,
        grid=(bsz, t // A_BLOCK, 2),
        in_specs=[pl.BlockSpec(memory_space=pltpu.SMEM), _const_spec(maps.shape)] + specs,
        out_specs=pl.BlockSpec((None, A_BLOCK, LANES), lambda b, i, pair: (b, i, pair)),
        out_shape=jax.ShapeDtypeStruct((bsz, t, A_OUT), F32),
        scratch_shapes=[pltpu.VMEM((4 * ng, A_KEYS, 2 * A_KEYS), F32)] + [pltpu.VMEM((ng, A_BLOCK, LANES), F32)] * 3
        + [pltpu.VMEM((8, A_KEYS * A_TILES, LANES), F32)],
        compiler_params=_params(3),
        name="attn_prompt",
    )(rel_bias, maps, *([qkv] * (5 * ng)))


def _merge_groups(outs, lses):
    m = functools.reduce(jnp.maximum, lses)
    ws = [jnp.exp(l - m) for l in lses]
    num = functools.reduce(lambda a, b: a + b, [w * o for w, o in zip(ws, outs)])
    return num / functools.reduce(lambda a, b: a + b, ws)


def _attn_sample_kernel(rb_ref, bc0, bc1, bc2, bn0, bn1, bn2, qkv_ref, c0_ref, c1_ref, c2_ref, out_ref,
                        bias_c0, bias_c1, bias_c2, bias_n):
    bias_c = (bias_c0, bias_c1, bias_c2)

    @pl.when(pl.program_id(0) == 0)
    def _():
        for g, (bc, bn) in enumerate(((bc0, bn0), (bc1, bn1), (bc2, bn2))):
            for h in range(4):
                bias_c[g][h] = _bias_from_buckets(bc[...], rb_ref, g * 4 + h)
                bias_n[g * 4 + h] = _bias_from_buckets(bn[...], rb_ref, g * 4 + h)

    qkv = qkv_ref[...]
    c_refs = (c0_ref, c1_ref, c2_ref)
    units = [(g, h) for g in range(len(A_GROUPS)) for h in range(4)]
    cols = [(g * 4 + h) * HEAD for g, h in units]
    q = [(qkv[:, c:c + HEAD] * (HEAD ** -0.5)).astype(BF16) for c in cols]
    kn = [qkv[:, 3 * A_OUT + c:3 * A_OUT + c + HEAD].astype(BF16) for c in cols]
    vn = [qkv[:, 6 * A_OUT + c:6 * A_OUT + c + HEAD].astype(BF16) for c in cols]
    n = range(len(units))
    s_c = [_dg(q[i], c_refs[g][0, h].astype(BF16), NN) + bias_c[g][h] for i, (g, h) in enumerate(units)]
    s_n = [_dg(q[i], kn[i], NT) + bias_n[i] for i in n]
    m = [jnp.maximum(jnp.max(s_c[i], axis=-1, keepdims=True), jnp.max(s_n[i], axis=-1, keepdims=True)) for i in n]
    p_c = [jnp.exp(s_c[i] - m[i]) for i in n]
    p_n = [jnp.exp(s_n[i] - m[i]) for i in n]
    den = [jnp.sum(p_c[i], axis=-1, keepdims=True) + jnp.sum(p_n[i], axis=-1, keepdims=True) for i in n]
    o = [(_dg(p_c[i].astype(BF16), c_refs[g][1, h].astype(BF16), NT) + _dg(p_n[i].astype(BF16), vn[i], NN)) / den[i]
         for i, (g, h) in enumerate(units)]
    lse = [m[i] + jnp.log(den[i]) for i in n]
    out_ref[...] = jnp.concatenate([_merge_groups(o[h::4], lse[h::4]) for h in range(4)], axis=-1)


def _sample_bucket_maps(window, dil, tp, t_real):
    t = np.arange(tp)[:, None]
    dist_c = window + t - np.arange(window)[None, :]
    ok_c = (dist_c % dil == 0) & (dist_c // dil <= A_KEYS) & (t < t_real)
    dist_n = t - np.arange(tp)[None, :]
    ok_n = (dist_n >= 0) & (dist_n % dil == 0) & (dist_n // dil <= A_KEYS)
    mc = np.where(ok_c, _t5_bucket_np(np.maximum(dist_c, 0)), -1).astype(np.int32)
    mn = np.where(ok_n, _t5_bucket_np(np.maximum(dist_n, 0)), -1).astype(np.int32)
    return mc, mn


def _attn_sample(qkv, caches, rel_bias, t_real):
    bsz, tp, _ = qkv.shape
    maps = [_sample_bucket_maps(w, d, tp, t_real) for w, d in A_GROUPS]
    mcs = [jnp.asarray(m[0]) for m in maps]
    mns = [jnp.asarray(m[1]) for m in maps]
    return pl.pallas_call(
        _attn_sample_kernel,
        grid=(bsz,),
        in_specs=[pl.BlockSpec(memory_space=pltpu.SMEM)]
        + [_const_spec(m.shape) for m in mcs] + [_const_spec(m.shape) for m in mns]
        + [pl.BlockSpec((None, tp, A_QKV), lambda b: (b, 0, 0))]
        + [pl.BlockSpec((None, 2, 4, HEAD, w), lambda b: (b, 0, 0, 0, 0)) for w, _ in A_GROUPS],
        out_specs=pl.BlockSpec((None, tp, A_OUT), lambda b: (b, 0, 0)),
        out_shape=jax.ShapeDtypeStruct((bsz, tp, A_OUT), F32),
        scratch_shapes=[pltpu.VMEM((4, tp, w), F32) for w, _ in A_GROUPS] + [pltpu.VMEM((12, tp, tp), F32)],
        compiler_params=_params(1),
        name="attn_sample",
    )(rel_bias, *mcs, *mns, qkv, *caches)


def _head_sums(x, h1, fn=lambda s: s):
    first = jnp.sum(jnp.where(h1, x, 0.0), axis=-1, keepdims=True)
    second = jnp.sum(jnp.where(h1, 0.0, x), axis=-1, keepdims=True)
    return jnp.where(h1, fn(first), fn(second))


def _rwkv_kernel(rkv_ref, lora_ref, sh_rkv_ref, sh_lora_ref, s0_ref, mu_rkv_ref, mu_lora_ref, w0_ref, w2_ref,
                 a0_ref, a2_ref, g2_ref, kk_ref, ka_ref, rk_ref, lnw_ref, lnb_ref, tri_ref,
                 out_ref, s_out_ref, s_scr, prev_rkv, prev_lora, *, nb, chunk, t_real):
    c = pl.program_id(1)
    nc = pl.num_programs(1)
    seqs = range(nb)

    @pl.when(c == 0)
    def _():
        zero_blk = jnp.zeros((HEAD, HEAD), F32)
        for s in seqs:
            for p in range(B_PAIRS):
                s_scr[s, p] = jnp.concatenate([jnp.concatenate([s0_ref[s, 2 * p], zero_blk], axis=1),
                                               jnp.concatenate([zero_blk, s0_ref[s, 2 * p + 1]], axis=1)], axis=0)
        prev_rkv[...] = sh_rkv_ref[...]
        prev_lora[...] = sh_lora_ref[...]

    n_rows = nb * chunk
    rsl = [slice(s * chunk, (s + 1) * chunk) for s in seqs]
    rows = lax.broadcasted_iota(jnp.int32, (n_rows, 1), 0)
    pb = rkv_ref[...].reshape(n_rows, 3 * B_WIDTH)
    lr = lora_ref[...].reshape(n_rows, LORA_PAD)
    pb_prev = pltpu.roll(pb, 1, 0)
    lr_prev = pltpu.roll(lr, 1, 0)
    for s in seqs:
        at = rows == s * chunk
        pb_prev = jnp.where(at, prev_rkv[s], pb_prev)
        lr_prev = jnp.where(at, prev_lora[s], lr_prev)
        prev_rkv[s] = pb[(s + 1) * chunk - 1:(s + 1) * chunk, :]
        prev_lora[s] = lr[(s + 1) * chunk - 1:(s + 1) * chunk, :]
    xs = pb + (pb_prev - pb) * mu_rkv_ref[...]
    xl = lr + (lr_prev - lr) * mu_lora_ref[...]
    r = xs[:, :B_WIDTH]
    k = xs[:, B_WIDTH:2 * B_WIDTH]
    v = xs[:, 2 * B_WIDTH:]
    o_aaa, o_gate = LORA_TILES[0] * LANES, (LORA_TILES[0] + LORA_TILES[1]) * LANES
    th_hi, th_lo = _split2(jnp.tanh(xl[:, :o_aaa]))
    z = w0_ref[...] + (_dg(th_hi, w2_ref[0], NN) + (_dg(th_hi, w2_ref[1], NN) + _dg(th_lo, w2_ref[0], NN)))
    lam = -DECAY_SCALE * _sigmoid(z)
    a = _sigmoid(a0_ref[...] + _mm(xl[:, o_aaa:o_gate], a2_ref[...]))
    gate = _mm(_sigmoid(xl[:, o_gate:]), g2_ref[...])
    kk = k * kk_ref[...]
    kmod = k * (1.0 + (a - 1.0) * ka_ref[...])
    if t_real < chunk:
        live = rows % chunk < t_real
        lam = jnp.where(live, lam, 0.0)
        kk = jnp.where(live, kk, 0.0)
        kmod = jnp.where(live, kmod, 0.0)
        v = jnp.where(live, v, 0.0)

    cum = _mm_exact_lhs(tri_ref[...], lam, terms=2)
    cum_end = [cum[(s + 1) * chunk - 1:(s + 1) * chunk, :] for s in seqs]
    e_in = jnp.exp(cum)
    e_prev = jnp.exp(cum - lam)
    e_neg = jnp.exp(-cum)
    e_end = [jnp.exp(cum_end[s] - cum[rsl[s]]) for s in seqs]
    g_end = [jnp.exp(cum_end[s]) for s in seqs]

    ri = lax.broadcasted_iota(jnp.int32, (chunk, LANES), 0)
    ci = lax.broadcasted_iota(jnp.int32, (chunk, LANES), 1) % HEAD
    strict = ci < ri
    lower = ci <= ri
    lane = lax.broadcasted_iota(jnp.int32, (1, LANES), 1)
    h1 = lane < HEAD
    bi = lax.broadcasted_iota(jnp.int32, (LANES, LANES), 0) // HEAD
    bj = lax.broadcasted_iota(jnp.int32, (LANES, LANES), 1) // HEAD
    block_diag = bi == bj
    levels = int(math.log2(chunk))
    zero = jnp.zeros((chunk, LANES), F32)
    units = [(s, p) for s in seqs for p in range(B_PAIRS)]
    idx = range(len(units))
    csl = [slice(p * LANES, (p + 1) * LANES) for _, p in units]
    take = lambda arr: [arr[rsl[s], csl[i]] for i, (s, _) in enumerate(units)]

    kappa = [x * _head_sums(x * x, h1, lambda q: lax.rsqrt(jnp.maximum(q, 1e-24))) for x in take(kk)]
    a_u, r_u, k_u, v_u = take(a), take(r), take(kmod), take(v)
    e_neg_u = take(e_neg)
    bb = [kappa[i] * a_u[i] for i in idx]
    a_t = [-x * e for x, e in zip(kappa, take(e_prev))]
    r_t = [x * e for x, e in zip(r_u, take(e_in))]
    m4 = []
    for i in idx:
        l4 = jnp.concatenate([jnp.where(h1, a_t[i], zero), jnp.where(h1, zero, a_t[i]),
                              jnp.where(h1, r_t[i], zero), jnp.where(h1, zero, r_t[i])], axis=0)
        m4.append(_mm(l4, _stack_halves(bb[i] * e_neg_u[i], k_u[i] * e_neg_u[i]), NT))
    na = [[jnp.where(strict, m4[i][j * chunk:(j + 1) * chunk], 0.0) for j in range(2)] for i in idx]
    nr = [[jnp.where(lower, m4[i][(2 + j) * chunk:(3 + j) * chunk], 0.0) for j in range(2)] for i in idx]
    zv = [_stack_halves(zero, pltpu.roll(v_u[i], HEAD, 1)).astype(BF16) for i in idx]
    zs = [[jnp.where(h1, a_t[i], _mm(na[i][0], zv[i])), jnp.where(h1, _mm(na[i][1], zv[i]), a_t[i])] for i in idx]
    ps = [[na[i][j][:, :chunk].astype(BF16) for j in range(2)] for i in idx]
    for lvl in range(levels):
        last = lvl == levels - 1
        for i in idx:
            for j in range(2):
                z_b = zs[i][j].astype(BF16)
                upd = _dg(ps[i][j], z_b if last else jnp.concatenate([z_b, ps[i][j]], axis=-1), NN)
                zs[i][j] = zs[i][j] + upd[:, :LANES]
                if not last:
                    ps[i][j] = upd[:, LANES:].astype(BF16)
    ta = [jnp.where(h1, zs[i][0], zs[i][1]) for i in idx]
    pv = [pltpu.roll(jnp.where(h1, zs[i][1], zs[i][0]), HEAD, 1) for i in idx]
    s_old = [s_scr[s, p] for s, p in units]
    s_b = [x.astype(BF16) for x in s_old]
    u = [_mm(ta[i], s_b[i], NT) + pv[i] for i in idx]
    uv = [_stack_halves(u[i], v_u[i]) for i in idx]
    uv_b = [x.astype(BF16) for x in uv]
    y = [_mm(r_t[i], s_b[i], NT) + jnp.where(h1, _mm(nr[i][0], uv_b[i]), _mm(nr[i][1], uv_b[i])) for i in idx]
    for i, (s, p) in enumerate(units):
        e = e_end[s][:, csl[i]]
        s_new = s_old[i] * g_end[s][:, csl[i]] + _mm(uv[i].T, _stack_halves(bb[i] * e, k_u[i] * e))
        s_scr[s, p] = jnp.where(block_diag, s_new, 0.0)

    mean = [_head_sums(y[i], h1) * (1.0 / HEAD) for i in idx]
    dlt = [y[i] - mean[i] for i in idx]
    var = [_head_sums(d * d, h1) * (1.0 / HEAD) for d in dlt]
    gate_u = take(gate)
    for i, (s, p) in enumerate(units):
        sl = csl[i]
        bonus = _head_sums(r_u[i] * k_u[i] * rk_ref[:, sl], h1)
        yn = dlt[i] * lax.rsqrt(var[i] + LNX_EPS) * lnw_ref[:, sl] + lnb_ref[:, sl]
        out_ref[s, :, sl] = (yn + bonus * v_u[i]) * gate_u[i]

    @pl.when(c == nc - 1)
    def _():
        for s in seqs:
            for p in range(B_PAIRS):
                s_pair = s_scr[s, p]
                s_out_ref[s, 2 * p] = s_pair[:HEAD, :HEAD]
                s_out_ref[s, 2 * p + 1] = s_pair[HEAD:, HEAD:]


def _rwkv(rkv, lora, sh_rkv, sh_lora, s0, prm, chunk, t_real):
    bsz, t, _ = rkv.shape
    nc = t // chunk
    nb = min(bsz, MAX_SEQS_PER_STEP)
    tri = np.kron(np.eye(nb), np.tril(np.ones((chunk, chunk)))).astype(np.float32)
    vec = lambda n: _const_spec((1, n))
    row = lambda x: x.reshape(1, -1)
    out, s_out = pl.pallas_call(
        functools.partial(_rwkv_kernel, nb=nb, chunk=chunk, t_real=t_real),
        grid=(bsz // nb, nc),
        in_specs=[pl.BlockSpec((nb, chunk, 3 * B_WIDTH), lambda b, c: (b, c, 0)),
                  pl.BlockSpec((nb, chunk, LORA_PAD), lambda b, c: (b, c, 0)),
                  pl.BlockSpec((nb, 1, 3 * B_WIDTH), lambda b, c: (b, 0, 0)),
                  pl.BlockSpec((nb, 1, LORA_PAD), lambda b, c: (b, 0, 0)),
                  pl.BlockSpec((nb, 2 * B_PAIRS, HEAD, HEAD), lambda b, c: (b, 0, 0, 0)),
                  vec(3 * B_WIDTH), vec(LORA_PAD), vec(B_WIDTH), _const_spec(prm["w2"].shape),
                  vec(B_WIDTH), _const_spec(prm["a2"].shape), _const_spec(prm["g2"].shape),
                  vec(B_WIDTH), vec(B_WIDTH), vec(B_WIDTH), vec(B_WIDTH), vec(B_WIDTH),
                  _const_spec(tri.shape)],
        out_specs=[pl.BlockSpec((nb, chunk, B_WIDTH), lambda b, c: (b, c, 0)),
                   pl.BlockSpec((nb, 2 * B_PAIRS, HEAD, HEAD), lambda b, c: (b, 0, 0, 0))],
        out_shape=[jax.ShapeDtypeStruct((bsz, t, B_WIDTH), F32),
                   jax.ShapeDtypeStruct((bsz, 2 * B_PAIRS, HEAD, HEAD), F32)],
        scratch_shapes=[pltpu.VMEM((nb, B_PAIRS, LANES, LANES), F32), pltpu.VMEM((nb, 1, 3 * B_WIDTH), F32),
                        pltpu.VMEM((nb, 1, LORA_PAD), F32)],
        compiler_params=_params(2),
        name="rwkv7",
    )(rkv, lora, sh_rkv[:, None], sh_lora[:, None], s0,
      row(prm["mu_rkv"]), row(prm["mu_lora"]), row(prm["w0"]), prm["w2"], row(prm["a0"]), prm["a2"], prm["g2"],
      row(prm["k_k"]), row(prm["k_a"]), row(prm["r_k"]), row(prm["lnx_w"]), row(prm["lnx_b"]),
      jnp.asarray(tri, BF16))
    return out, s_out


def _gla_kernel(pc_ref, s0_ref, lb_ref, gn_ref, sums_ref, out_ref, s_out_ref, s_scr, *, nb, chunk, t_real, layer):
    c = pl.program_id(1)
    nc = pl.num_programs(1)
    seqs = range(nb)

    @pl.when(c == 0)
    def _():
        for s in seqs:
            for h in range(C_HEADS):
                s_scr[s, h] = s0_ref[s, h].T

    n_rows = nb * chunk
    rsl = [slice(s * chunk, (s + 1) * chunk) for s in seqs]
    pc = pc_ref[...].reshape(n_rows, 4 * D_MODEL)
    raw = lb_ref[...]
    e = jnp.exp(raw - jnp.max(raw, axis=0, keepdims=True))
    sm = e / jnp.sum(e, axis=0, keepdims=True)
    lb = jnp.sum(sm[:layer + 1], axis=0, keepdims=True) - sm[0:1]
    xq = pc[:, :C_FDIM]
    q = xq * _sigmoid(xq)
    fg = lb + (1.0 - lb) * _sigmoid(pc[:, C_FDIM:2 * C_FDIM])
    k = 1.0 - fg
    logf = jnp.log(fg)
    v = pc[:, 2 * C_FDIM:2 * C_FDIM + D_MODEL]
    xg = pc[:, 2 * C_FDIM + D_MODEL:]
    if t_real < chunk:
        live = lax.broadcasted_iota(jnp.int32, (n_rows, 1), 0) % chunk < t_real
        logf = jnp.where(live, logf, 0.0)
        k = jnp.where(live, k, 0.0)
        v = jnp.where(live, v, 0.0)

    sums = _mm_exact_lhs(sums_ref[...], logf)
    cum = sums[:n_rows]
    base = sums[n_rows:]
    q_in = q * jnp.exp(cum)
    q_loc = q * jnp.exp(cum - base)
    k_loc = k * jnp.exp(jnp.minimum(base - cum, EXP_CLAMP))
    nsub = chunk // GLA_SUB
    k_stack, v_stack, k_end, g_end = [], [], [], []
    for s in seqs:
        cum_s, k_s = cum[rsl[s]], k[rsl[s]]
        cum_end = cum_s[chunk - 1:chunk, :]
        k_end.append(k_s * jnp.exp(cum_end - cum_s))
        g_end.append(jnp.exp(cum_end))
        k_var = [k_loc[rsl[s]]]
        for i in range(1, nsub):
            ref_i = cum_s[i * GLA_SUB - 1:i * GLA_SUB, :]
            k_var.append(k_s * jnp.exp(jnp.minimum(ref_i - cum_s, 0.0)))
        k_stack.append(jnp.concatenate(k_var, axis=0) if nsub > 1 else k_var[0])
        v_stack.append(jnp.concatenate([v[rsl[s]]] * nsub, axis=0) if nsub > 1 else v[rsl[s]])

    ri = lax.broadcasted_iota(jnp.int32, (chunk, nsub * chunk), 0)
    cc = lax.broadcasted_iota(jnp.int32, (chunk, nsub * chunk), 1)
    var = cc // chunk
    ci = cc % chunk
    same_sub = ci // GLA_SUB == ri // GLA_SUB
    att_mask = ((var == 0) & same_sub & (ci <= ri)) | ((ri // GLA_SUB == var) & (ci < var * GLA_SUB))

    units = [(s, h) for s in seqs for h in range(C_HEADS)]
    idx = range(len(units))
    csl = [slice(h * C_HEAD, (h + 1) * C_HEAD) for _, h in units]
    att = [jnp.where(att_mask, _mm(q_loc[rsl[s], csl[i]], k_stack[s][:, csl[i]], NT), 0.0)
           for i, (s, _) in enumerate(units)]
    s_old = [s_scr[s, h] for s, h in units]
    outs = [_mm(q_in[rsl[s], csl[i]], s_old[i], NT) + _mm(att[i], v_stack[s][:, csl[i]])
            for i, (s, _) in enumerate(units)]
    for i, (s, h) in enumerate(units):
        sl = csl[i]
        s_scr[s, h] = s_old[i] * g_end[s][:, sl] + _mm(_pad_rows_to_tile(v[rsl[s], sl]).T,
                                                       _pad_rows_to_tile(k_end[s][:, sl]))
    o = jnp.concatenate([jnp.concatenate(outs[s * C_HEADS:(s + 1) * C_HEADS], axis=-1) for s in seqs], axis=0)
    out_ref[...] = (_rms(o, gn_ref[...]) * (xg * _sigmoid(xg))).reshape(nb, chunk, D_MODEL)

    @pl.when(c == nc - 1)
    def _():
        for s in seqs:
            for h in range(C_HEADS):
                s_out_ref[s, h] = s_scr[s, h].T


def _gla(pc, s0, lb_raw, layer, gn, chunk, t_real):
    bsz, t, _ = pc.shape
    nc = t // chunk
    nb = min(bsz, MAX_SEQS_PER_STEP)
    idx = np.arange(chunk)
    eye = np.eye(nb)
    tri = np.kron(eye, np.tril(np.ones((chunk, chunk)))).astype(np.float32)
    sel = np.kron(eye, idx[None, :] < (idx[:, None] // GLA_SUB) * GLA_SUB).astype(np.float32)
    return pl.pallas_call(
        functools.partial(_gla_kernel, nb=nb, chunk=chunk, t_real=t_real, layer=layer),
        grid=(bsz // nb, nc),
        in_specs=[pl.BlockSpec((nb, chunk, 4 * D_MODEL), lambda b, c: (b, c, 0)),
                  pl.BlockSpec((nb, C_HEADS, C_HEAD, C_HEAD), lambda b, c: (b, 0, 0, 0)),
                  _const_spec(lb_raw.shape), _const_spec((1, D_MODEL)),
                  _const_spec((2 * nb * chunk, nb * chunk))],
        out_specs=[pl.BlockSpec((nb, chunk, D_MODEL), lambda b, c: (b, c, 0)),
                   pl.BlockSpec((nb, C_HEADS, C_HEAD, C_HEAD), lambda b, c: (b, 0, 0, 0))],
        out_shape=[jax.ShapeDtypeStruct((bsz, t, D_MODEL), F32),
                   jax.ShapeDtypeStruct((bsz, C_HEADS, C_HEAD, C_HEAD), F32)],
        scratch_shapes=[pltpu.VMEM((nb, C_HEADS, C_HEAD, C_HEAD), F32)],
        compiler_params=_params(2),
        name="hgrn2",
    )(pc, s0, lb_raw, gn.reshape(1, -1), jnp.asarray(np.concatenate([tri, sel]), BF16))


def _post_kernel(*refs, n_mix, final):
    x_ref = refs[0]
    mix_refs = refs[1:1 + n_mix]
    w_refs = refs[1 + n_mix:1 + 2 * n_mix]
    gm_ref, up_ref, down_ref = refs[1 + 2 * n_mix:4 + 2 * n_mix]
    gf_ref = refs[4 + 2 * n_mix] if final else None
    o_ref = refs[-1]
    x = x_ref[...]
    for m_ref, w_ref in zip(mix_refs, w_refs):
        x = x + jnp.dot(m_ref[...].astype(BF16), w_ref[...], preferred_element_type=F32)
    h = _rms(x, gm_ref[...]).astype(BF16)
    u = jnp.dot(h, up_ref[...], preferred_element_type=F32)
    u = jnp.square(jnp.maximum(u, 0.0)).astype(BF16)
    x = x + jnp.dot(u, down_ref[...], preferred_element_type=F32)
    if final:
        x = _rms(x, gf_ref[...])
    o_ref[...] = x


def _post(x, mixes, ws, gm, up, down, layer, gf=None, tm=512):
    m = x.shape[0]
    final = gf is not None
    row_spec = lambda n: pl.BlockSpec((tm, n), lambda i: (i, 0))
    layer_spec = lambda w: pl.BlockSpec((None,) + w.shape[1:], lambda i: (layer, 0, 0),
                                        pipeline_mode=pl.Buffered(1))
    rows = [x] + list(mixes)
    args = rows + [*ws, gm.reshape(1, -1), up, down]
    specs = ([row_spec(a.shape[1]) for a in rows] + [_const_spec(w.shape) for w in ws]
             + [_const_spec((1, D_MODEL)), layer_spec(up), layer_spec(down)])
    if final:
        args.append(gf.reshape(1, -1))
        specs.append(_const_spec((1, D_MODEL)))
    return pl.pallas_call(
        functools.partial(_post_kernel, n_mix=len(mixes), final=final),
        grid=(m // tm,), in_specs=specs, out_specs=row_spec(D_MODEL),
        out_shape=jax.ShapeDtypeStruct((m, D_MODEL), F32), compiler_params=_params(1), name="post_mlp",
    )(*args)


def _lora_pad_cols(x):
    parts, start = [], 0
    for n, tiles in zip(LORA_DIMS, LORA_TILES):
        part = x[..., start:start + n]
        parts.append(jnp.pad(part, [(0, 0)] * (part.ndim - 1) + [(0, tiles * LANES - n)]))
        start += n
    return jnp.concatenate(parts, axis=-1)


def _lora_unpad_cols(x):
    parts, start = [], 0
    for n, tiles in zip(LORA_DIMS, LORA_TILES):
        parts.append(x[..., start:start + n])
        start += tiles * LANES
    return jnp.concatenate(parts, axis=-1)


def _prepare(rel_bias, norm_mix, norm_mlp, norm_final, e_w_in, e_mu, e_w0, e_w2, e_a0, e_a2, e_g2, e_k_k, e_k_a,
             e_r_k, e_lnx_w, e_lnx_b, e_w_out, c_w_in, c_lb_raw, c_norm, c_w_out, mlp_up, mlp_down):
    pad_rows = lambda a, n: jnp.pad(a, [(0, n - a.shape[0]), (0, 0)])
    w_in = e_w_in[0]
    return dict(
        rel_bias=rel_bias,
        w_in=w_in.astype(BF16), w_lora=_lora_pad_cols(w_in[:, A_QKV + 3 * B_WIDTH:]).astype(BF16),
        rwkv=dict(mu_rkv=e_mu[0, :3 * B_WIDTH], mu_lora=_lora_pad_cols(e_mu[0, 3 * B_WIDTH:]),
                  w0=e_w0[0], w2=jnp.stack(_split2(pad_rows(e_w2[0], LORA_TILES[0] * LANES))), a0=e_a0[0],
                  a2=pad_rows(e_a2[0], LORA_TILES[1] * LANES).astype(BF16),
                  g2=pad_rows(e_g2[0], LORA_TILES[2] * LANES).astype(BF16),
                  k_k=e_k_k[0], k_a=e_k_a[0], r_k=e_r_k[0].reshape(-1),
                  lnx_w=e_lnx_w[0], lnx_b=e_lnx_b[0]),
        w_out_a=e_w_out[0, :A_OUT].astype(BF16), w_out_b=e_w_out[0, A_OUT:].astype(BF16),
        c_w_in=c_w_in[0].astype(BF16), c_w_out=c_w_out[0].astype(BF16), lb_raw=c_lb_raw, c_norm=c_norm[0],
        norm_mix=norm_mix, norm_mlp=norm_mlp, norm_final=norm_final,
        up=mlp_up.astype(BF16), down=mlp_down.astype(BF16))


def _trunk(x, prm, attend, tiled_qkv, sh_rkv, sh_lora, wkv0, c0, chunk, t_real):
    bsz, t, _ = x.shape
    flat = lambda a: a.reshape(bsz * t, a.shape[-1])
    xf = flat(x)
    assert A_QKV == 3 * B_WIDTH
    qkv, rkv, lora = _norm_proj(xf, prm["norm_mix"][0],
                                [(prm["w_in"], A_QKV, 0), (prm["w_in"], A_QKV, 1), (prm["w_lora"], LORA_PAD, 0)],
                                tiled=(0,) if tiled_qkv else ())
    qkv = qkv.reshape((A_QKV // LANES, bsz, t, LANES) if tiled_qkv else (bsz, t, A_QKV))
    rkv = rkv.reshape(bsz, t, 3 * B_WIDTH)
    lora = lora.reshape(bsz, t, LORA_PAD)
    a_out = attend(qkv)
    b_out, wkv = _rwkv(rkv, lora, sh_rkv, sh_lora, wkv0, prm["rwkv"], chunk, t_real)
    x1 = _post(xf, [flat(a_out), flat(b_out)], [prm["w_out_a"], prm["w_out_b"]], prm["norm_mlp"][0],
               prm["up"], prm["down"], 0)
    (pc,) = _norm_proj(x1, prm["norm_mix"][1], [(prm["c_w_in"], 4 * D_MODEL, 0)])
    c_out, c_state = _gla(pc.reshape(bsz, t, 4 * D_MODEL), c0, prm["lb_raw"], 1, prm["c_norm"], chunk, t_real)
    y = _post(x1, [flat(c_out)], [prm["c_w_out"]], prm["norm_mlp"][1], prm["up"], prm["down"], 1,
              gf=prm["norm_final"])
    kv_rows = []
    tiles_per = A_OUT // LANES
    for g, (w, _) in enumerate(A_GROUPS):
        n = min(w, t_real)
        if tiled_qkv:
            rows = qkv[:, :, t_real - n:t_real]
            parts = [jnp.moveaxis(rows[(3 * part + g) * tiles_per:(3 * part + g + 1) * tiles_per], 0, 2)
                     .reshape(bsz, n, A_OUT) for part in (1, 2)]
        else:
            rows = qkv[:, t_real - n:t_real]
            parts = [rows[..., (3 * part + g) * A_OUT:(3 * part + g + 1) * A_OUT] for part in (1, 2)]
        kv_rows.append(jnp.stack(parts, axis=2).reshape(1, bsz, n, 2, 4, HEAD))
    shift = jnp.concatenate([rkv[:, t_real - 1], _lora_unpad_cols(lora[:, t_real - 1])], axis=-1)[None]
    return y.reshape(bsz, t, D_MODEL)[:, :t_real], kv_rows, shift, wkv[None], c_state[None]


def kernel(x_prompt, x_sample, cache_a0, cache_a1, cache_a2, state_b_shift, state_b_wkv, state_c, rel_bias, norm_mix, norm_mlp, norm_final, e_w_in, e_mu, e_w0, e_w2, e_a0, e_a2, e_g2, e_k_k, e_k_a, e_r_k, e_lnx_w, e_lnx_b, e_w_out, c_w_in, c_lb_raw, c_norm, c_w_out, mlp_up, mlp_down):
    prm = _prepare(rel_bias, norm_mix, norm_mlp, norm_final, e_w_in, e_mu, e_w0, e_w2, e_a0, e_a2, e_g2, e_k_k,
                   e_k_a, e_r_k, e_lnx_w, e_lnx_b, e_w_out, c_w_in, c_lb_raw, c_norm, c_w_out, mlp_up, mlp_down)

    bp, tp, _ = x_prompt.shape

    def attend_prompt(qkv):
        return _attn_prompt(qkv, prm["rel_bias"])

    y_p, p_kv, p_shift, p_wkv, p_c = _trunk(
        x_prompt, prm, attend_prompt, True,
        jnp.zeros((bp, 3 * B_WIDTH), F32), jnp.zeros((bp, LORA_PAD), F32),
        jnp.zeros((bp, 2 * B_PAIRS, HEAD, HEAD), F32), jnp.zeros((bp, C_HEADS, C_HEAD, C_HEAD), F32),
        chunk=CHUNK, t_real=tp)

    bs, ts, _ = x_sample.shape
    t_pad = SAMPLE_ROWS
    x_s = jnp.pad(x_sample, ((0, 0), (0, t_pad - ts), (0, 0)))
    caches = [jnp.transpose(c[0], (0, 2, 3, 4, 1)) for c in (cache_a0, cache_a1, cache_a2)]

    def attend_sample(qkv):
        return _attn_sample(qkv, caches, prm["rel_bias"], ts)

    sh = state_b_shift[0]
    y_s, s_kv, s_shift, s_wkv, s_c = _trunk(
        x_s, prm, attend_sample, False, sh[:, :3 * B_WIDTH], _lora_pad_cols(sh[:, 3 * B_WIDTH:]),
        state_b_wkv[0], state_c[0], chunk=t_pad, t_real=ts)

    return (y_p, y_s, p_kv[0], p_kv[1], p_kv[2], p_shift, p_wkv, p_c,
            s_kv[0], s_kv[1], s_kv[2], s_shift, s_wkv, s_c)
```

```python
import functools
import math

import jax
import jax.numpy as jnp
import numpy as np
from jax import lax
from jax.experimental import pallas as pl
from jax.experimental.pallas import tpu as pltpu

F32 = jnp.float32
BF16 = jnp.bfloat16

LANES = 128
D_MODEL = 1024
EPS = 1e-6
LNX_EPS = 64e-5
HEAD = 64
CHUNK = 64
SAMPLE_ROWS = 16
A_GROUPS = ((128, 1), (512, 4), (2048, 16))
A_KEYS = 128
A_TILES = 4
A_BLOCK = A_KEYS * A_GROUPS[-1][1]
A_STRIDE = 4
A_QKV = 2304
A_OUT = 256
N_BUCKETS = 32
BUCKET_MAX_DIST = 2048
B_WIDTH = 768
B_PAIRS = B_WIDTH // LANES
LORA_DIMS = (64, 64, 160)
LORA_TILES = (1, 1, 2)
LORA_PAD = sum(LORA_TILES) * LANES
C_FDIM = 1024
C_HEADS = 8
C_HEAD = 128
NEG = -1e30
LOG2E = math.log2(math.e)
DECAY_SCALE = math.exp(-0.5)
GLA_SUB = 16
EXP_CLAMP = 80.0
MAX_SEQS_PER_STEP = 4

VMEM_LIMIT = 56 * 1024 * 1024
assert 2 * HEAD == LANES and C_HEAD == LANES and A_KEYS == LANES

NN = (((1,), (0,)), ((), ()))
NT = (((1,), (1,)), ((), ()))


def _dg(a, b, dims):
    return lax.dot_general(a, b, dims, preferred_element_type=F32)


def _split2(x):
    hi = x.astype(BF16)
    lo = (x - hi.astype(F32)).astype(BF16)
    return hi, lo


def _split3(x):
    hi = x.astype(BF16)
    r1 = x - hi.astype(F32)
    mid = r1.astype(BF16)
    lo = (r1 - mid.astype(F32)).astype(BF16)
    return hi, mid, lo


def _mm(a, b, dims=NN):
    return _dg(a.astype(BF16), b.astype(BF16), dims)


def _mm_exact_lhs(a_bf16, b, terms=3):
    if terms == 2:
        bh, bl = _split2(b)
        return _dg(a_bf16, bh, NN) + _dg(a_bf16, bl, NN)
    bh, bm, bl = _split3(b)
    return _dg(a_bf16, bh, NN) + (_dg(a_bf16, bm, NN) + _dg(a_bf16, bl, NN))


def _pad_rows_to_tile(x):
    n = x.shape[0]
    if n == LANES:
        return x
    return jnp.concatenate([x, jnp.zeros((LANES - n, LANES), F32)], axis=0)


def _stack_halves(top, bottom):
    n = top.shape[0]
    half = LANES // 2
    if n == half:
        return jnp.concatenate([top, bottom], axis=0)
    pad = jnp.zeros((half - n, LANES), F32)
    return jnp.concatenate([top, pad, bottom, pad], axis=0)


def _sigmoid(x):
    return 0.5 * jnp.tanh(0.5 * x) + 0.5


def _rms(x, g):
    ms = jnp.mean(x * x, axis=-1, keepdims=True)
    return x * lax.rsqrt(ms + EPS) * g


def _const_spec(shape):
    nd = len(shape)
    return pl.BlockSpec(shape, lambda *_: (0,) * nd, pipeline_mode=pl.Buffered(1))


def _params(n_grid):
    return pltpu.CompilerParams(dimension_semantics=("arbitrary",) * n_grid,
                                vmem_limit_bytes=VMEM_LIMIT)


def _norm_proj_kernel(x_ref, g_ref, *refs):
    n = len(refs) // 2
    h = _rms(x_ref[...], g_ref[...]).astype(BF16)
    for w_ref, o_ref in zip(refs[:n], refs[n:]):
        o_ref[...] = jnp.dot(h, w_ref[...], preferred_element_type=F32)


def _norm_proj(x, g, ws, tm=512):
    m = x.shape[0]
    tm = min(tm, m)
    return pl.pallas_call(
        _norm_proj_kernel,
        grid=(m // tm,),
        in_specs=[pl.BlockSpec((tm, D_MODEL), lambda i: (i, 0)), _const_spec((1, D_MODEL))]
        + [pl.BlockSpec((D_MODEL, n), lambda i, j=j: (0, j), pipeline_mode=pl.Buffered(1)) for _, n, j in ws],
        out_specs=[pl.BlockSpec((tm, n), lambda i: (i, 0)) for _, n, _ in ws],
        out_shape=[jax.ShapeDtypeStruct((m, n), F32) for _, n, _ in ws],
        compiler_params=_params(1),
        name="norm_proj",
    )(x, g.reshape(1, D_MODEL), *[w for w, _, _ in ws])


def _t5_bucket_np(dist):
    max_exact = N_BUCKETS // 2
    d = np.maximum(dist, 1).astype(np.float32)
    large = max_exact + (np.log(d / np.float32(max_exact)) / np.float32(math.log(BUCKET_MAX_DIST / max_exact))
                         * np.float32(N_BUCKETS - max_exact)).astype(np.int32)
    large = np.minimum(large, N_BUCKETS - 1)
    return np.where(dist < max_exact, dist, large).astype(np.int32)


def _bias_from_buckets(idx, rb_ref, head):
    acc = jnp.full(idx.shape, NEG, F32)
    for b in range(N_BUCKETS):
        acc = jnp.where(idx == b, rb_ref[b, head], acc)
    return acc


def _tile_rows(start, dil):
    if dil > 1:
        return pl.ds(start, A_KEYS, stride=dil)
    return pl.ds(start if isinstance(start, int) else pl.multiple_of(start, A_KEYS), A_KEYS)


def _attn_prompt_kernel(rb_ref, bkt_ref, *refs):
    ng = len(A_GROUPS)
    q_refs, kc_refs, vc_refs, kp_refs, vp_refs = (refs[i * ng:(i + 1) * ng] for i in range(5))
    out_ref, bias_scr, o_scr, m_scr, d_scr, stage_scr = refs[5 * ng:]
    pair = pl.program_id(2)
    first = (pl.program_id(0) == 0) & (pl.program_id(1) == 0) & (pair == 0)

    @pl.when(first)
    def _():
        for g in range(ng):
            for h in range(4):
                bias_scr[g * 4 + h] = _bias_from_buckets(bkt_ref[g], rb_ref, g * 4 + h) * LOG2E

    col = lax.broadcasted_iota(jnp.int32, (1, 2 * A_KEYS), 1)
    prev_pen = jnp.where((col < A_KEYS) & (pl.program_id(1) == 0), NEG, 0.0).astype(F32)
    h1 = lax.broadcasted_iota(jnp.int32, (1, LANES), 1) < HEAD
    h2 = jnp.logical_not(h1)

    def attend(g, tiles):
        n = range(len(tiles))
        get = lambda src: src[0][src[1], :]
        bias_in = [bias_scr[g * 4 + 2 * pair + j] for j in range(2)]
        bias_edge = [b + prev_pen for b in bias_in]
        q = [get(t[0]) * (HEAD ** -0.5 * LOG2E) for t in tiles]
        k = [jnp.concatenate([get(t[3]), get(t[1])], axis=0).astype(BF16) for t in tiles]
        v = [jnp.concatenate([get(t[4]), get(t[2])], axis=0).astype(BF16) for t in tiles]
        s = [[_dg(jnp.where(mine, q[i], 0.0).astype(BF16), k[i], NT) + (bias_edge if tiles[i][6] else bias_in)[j]
              for j, mine in enumerate((h1, h2))] for i in n]
        m = [[jnp.max(x, axis=-1, keepdims=True) for x in s[i]] for i in n]
        p = [[jnp.exp2(x - mx) for x, mx in zip(s[i], m[i])] for i in n]
        den = [[jnp.sum(x, axis=-1, keepdims=True) for x in p[i]] for i in n]
        o = [[_dg(x.astype(BF16), v[i], NN) for x in p[i]] for i in n]
        for i in n:
            (o_dst, m_dst, d_dst), rows = tiles[i][5]
            o_dst[rows, :] = jnp.where(h1, o[i][0], o[i][1])
            m_dst[rows, :] = jnp.where(h1, m[i][0], m[i][1])
            d_dst[rows, :] = jnp.where(h1, den[i][0], den[i][1])

    n_tiles = A_BLOCK // A_KEYS
    for g, (_, dil) in enumerate(A_GROUPS):
        span = A_KEYS * dil
        single_span = span == A_BLOCK
        cur = (q_refs[g], kc_refs[g], vc_refs[g])
        before = (kp_refs[g], vp_refs[g])
        outs = (o_scr.at[g], m_scr.at[g], d_scr.at[g])

        def batch(it, static, g=g, dil=dil, span=span, single_span=single_span, cur=cur, before=before, outs=outs):
            if dil > A_STRIDE:
                inner = dil // A_STRIDE
                assert single_span and inner == A_TILES and inner <= A_STRIDE
                union = pl.ds(it, A_KEYS * inner, stride=A_STRIDE)
                n_in = len(cur + before)
                for a, ref in enumerate(cur + before):
                    stage_scr[a] = ref[union, :]
                slabs = [stage_scr.at[a] for a in range(n_in + len(outs))]
                attend(g, [tuple((slab, pl.ds(u, A_KEYS, stride=inner)) for slab in slabs[:n_in])
                           + ((tuple(slabs[n_in:]), pl.ds(u, A_KEYS, stride=inner)), True) for u in range(A_TILES)])
                for slab, dst in zip(slabs[n_in:], outs):
                    dst[union, :] = slab[...]
                return
            tiles = []
            for u in range(A_TILES):
                t = it * A_TILES + u
                s_idx, r = (t // dil, t % dil) if static else _span_residue(it, u, dil)
                edge = single_span or (static and s_idx == 0)
                rows = _tile_rows(s_idx * span + r, dil)
                prows = _tile_rows(r if edge else (s_idx - 1) * span + r, dil)
                tiles.append(tuple((ref, rows) for ref in cur)
                             + tuple((ref, prows) for ref in (before if edge else cur[1:])) + ((outs, rows), edge))
            attend(g, tiles)

        batch(0, True)

        def body(it, carry, batch=batch):
            batch(it, False)
            return carry
        lax.fori_loop(1, n_tiles // A_TILES, body, 0)

    def merge(i, carry):
        rows = pl.ds(pl.multiple_of(i * A_KEYS, A_KEYS), A_KEYS)
        ms = [m_scr[g, rows, :] for g in range(ng)]
        top = functools.reduce(jnp.maximum, ms)
        ws = [jnp.exp2(x - top) for x in ms]
        num = functools.reduce(lambda a, b: a + b, [w * o_scr[g, rows, :] for g, w in enumerate(ws)])
        den = functools.reduce(lambda a, b: a + b, [w * d_scr[g, rows, :] for g, w in enumerate(ws)])
        out_ref[rows, :] = num / den
        return carry
    lax.fori_loop(0, n_tiles, merge, 0)


def _span_residue(it, u, dil):
    if dil >= A_TILES:
        per = dil // A_TILES
        return it // per, (it % per) * A_TILES + u
    return it * (A_TILES // dil) + u // dil, u % dil


def _prompt_bucket_map(dil):
    qi = np.arange(A_KEYS)[:, None]
    ki = np.arange(2 * A_KEYS)[None, :]
    j = qi + A_KEYS - ki
    return np.where((j >= 0) & (j <= A_KEYS), _t5_bucket_np(np.clip(j, 0, A_KEYS) * dil), -1).astype(np.int32)


def _attn_prompt(qkv, rel_bias):
    bsz, t, _ = qkv.shape
    ng = len(A_GROUPS)

    def cur(part, g):
        return pl.BlockSpec((None, A_BLOCK, LANES), lambda b, i, pair: (b, i, (part * ng + g) * 2 + pair))

    def before(part, g):
        span = A_KEYS * A_GROUPS[g][1]
        per = A_BLOCK // span
        return pl.BlockSpec((None, span, LANES),
                            lambda b, i, pair: (b, jnp.maximum(i * per - 1, 0), (part * ng + g) * 2 + pair))

    specs = [cur(part, g) for part in range(3) for g in range(ng)] + \
            [before(part, g) for part in (1, 2) for g in range(ng)]
    maps = jnp.asarray(np.stack([_prompt_bucket_map(d) for _, d in A_GROUPS]))
    return pl.pallas_call(
        _attn_prompt_kernel,
        grid=(bsz, t // A_BLOCK, 2),
        in_specs=[pl.BlockSpec(memory_space=pltpu.SMEM), _const_spec(maps.shape)] + specs,
        out_specs=pl.BlockSpec((None, A_BLOCK, LANES), lambda b, i, pair: (b, i, pair)),
        out_shape=jax.ShapeDtypeStruct((bsz, t, A_OUT), F32),
        scratch_shapes=[pltpu.VMEM((4 * ng, A_KEYS, 2 * A_KEYS), F32)] + [pltpu.VMEM((ng, A_BLOCK, LANES), F32)] * 3
        + [pltpu.VMEM((8, A_KEYS * A_TILES, LANES), F32)],
        compiler_params=_params(3),
        name="attn_prompt",
    )(rel_bias, maps, *([qkv] * (5 * ng)))


def _merge_groups(outs, lses):
    m = functools.reduce(jnp.maximum, lses)
    ws = [jnp.exp(l - m) for l in lses]
    num = functools.reduce(lambda a, b: a + b, [w * o for w, o in zip(ws, outs)])
    return num / functools.reduce(lambda a, b: a + b, ws)


def _attn_sample_kernel(rb_ref, bc0, bc1, bc2, bn0, bn1, bn2, qkv_ref, c0_ref, c1_ref, c2_ref, out_ref,
                        bias_c0, bias_c1, bias_c2, bias_n):
    bias_c = (bias_c0, bias_c1, bias_c2)

    @pl.when(pl.program_id(0) == 0)
    def _():
        for g, (bc, bn) in enumerate(((bc0, bn0), (bc1, bn1), (bc2, bn2))):
            for h in range(4):
                bias_c[g][h] = _bias_from_buckets(bc[...], rb_ref, g * 4 + h)
                bias_n[g * 4 + h] = _bias_from_buckets(bn[...], rb_ref, g * 4 + h)

    qkv = qkv_ref[...]
    c_refs = (c0_ref, c1_ref, c2_ref)
    units = [(g, h) for g in range(len(A_GROUPS)) for h in range(4)]
    cols = [(g * 4 + h) * HEAD for g, h in units]
    q = [(qkv[:, c:c + HEAD] * (HEAD ** -0.5)).astype(BF16) for c in cols]
    kn = [qkv[:, 3 * A_OUT + c:3 * A_OUT + c + HEAD].astype(BF16) for c in cols]
    vn = [qkv[:, 6 * A_OUT + c:6 * A_OUT + c + HEAD].astype(BF16) for c in cols]
    n = range(len(units))
    s_c = [_dg(q[i], c_refs[g][0, h].astype(BF16), NN) + bias_c[g][h] for i, (g, h) in enumerate(units)]
    s_n = [_dg(q[i], kn[i], NT) + bias_n[i] for i in n]
    m = [jnp.maximum(jnp.max(s_c[i], axis=-1, keepdims=True), jnp.max(s_n[i], axis=-1, keepdims=True)) for i in n]
    p_c = [jnp.exp(s_c[i] - m[i]) for i in n]
    p_n = [jnp.exp(s_n[i] - m[i]) for i in n]
    den = [jnp.sum(p_c[i], axis=-1, keepdims=True) + jnp.sum(p_n[i], axis=-1, keepdims=True) for i in n]
    o = [(_dg(p_c[i].astype(BF16), c_refs[g][1, h].astype(BF16), NT) + _dg(p_n[i].astype(BF16), vn[i], NN)) / den[i]
         for i, (g, h) in enumerate(units)]
    lse = [m[i] + jnp.log(den[i]) for i in n]
    out_ref[...] = jnp.concatenate([_merge_groups(o[h::4], lse[h::4]) for h in range(4)], axis=-1)


def _sample_bucket_maps(window, dil, tp, t_real):
    t = np.arange(tp)[:, None]
    dist_c = window + t - np.arange(window)[None, :]
    ok_c = (dist_c % dil == 0) & (dist_c // dil <= A_KEYS) & (t < t_real)
    dist_n = t - np.arange(tp)[None, :]
    ok_n = (dist_n >= 0) & (dist_n % dil == 0) & (dist_n // dil <= A_KEYS)
    mc = np.where(ok_c, _t5_bucket_np(np.maximum(dist_c, 0)), -1).astype(np.int32)
    mn = np.where(ok_n, _t5_bucket_np(np.maximum(dist_n, 0)), -1).astype(np.int32)
    return mc, mn


def _attn_sample(qkv, caches, rel_bias, t_real):
    bsz, tp, _ = qkv.shape
    maps = [_sample_bucket_maps(w, d, tp, t_real) for w, d in A_GROUPS]
    mcs = [jnp.asarray(m[0]) for m in maps]
    mns = [jnp.asarray(m[1]) for m in maps]
    return pl.pallas_call(
        _attn_sample_kernel,
        grid=(bsz,),
        in_specs=[pl.BlockSpec(memory_space=pltpu.SMEM)]
        + [_const_spec(m.shape) for m in mcs] + [_const_spec(m.shape) for m in mns]
        + [pl.BlockSpec((None, tp, A_QKV), lambda b: (b, 0, 0))]
        + [pl.BlockSpec((None, 2, 4, HEAD, w), lambda b: (b, 0, 0, 0, 0)) for w, _ in A_GROUPS],
        out_specs=pl.BlockSpec((None, tp, A_OUT), lambda b: (b, 0, 0)),
        out_shape=jax.ShapeDtypeStruct((bsz, tp, A_OUT), F32),
        scratch_shapes=[pltpu.VMEM((4, tp, w), F32) for w, _ in A_GROUPS] + [pltpu.VMEM((12, tp, tp), F32)],
        compiler_params=_params(1),
        name="attn_sample",
    )(rel_bias, *mcs, *mns, qkv, *caches)


def _head_sums(x, h1, fn=lambda s: s):
    first = jnp.sum(jnp.where(h1, x, 0.0), axis=-1, keepdims=True)
    second = jnp.sum(jnp.where(h1, 0.0, x), axis=-1, keepdims=True)
    return jnp.where(h1, fn(first), fn(second))


def _rwkv_kernel(rkv_ref, lora_ref, sh_rkv_ref, sh_lora_ref, s0_ref, mu_rkv_ref, mu_lora_ref, w0_ref, w2_ref,
                 a0_ref, a2_ref, g2_ref, kk_ref, ka_ref, rk_ref, lnw_ref, lnb_ref, tri_ref,
                 out_ref, s_out_ref, s_scr, prev_rkv, prev_lora, *, nb, chunk, t_real):
    c = pl.program_id(1)
    nc = pl.num_programs(1)
    seqs = range(nb)

    @pl.when(c == 0)
    def _():
        zero_blk = jnp.zeros((HEAD, HEAD), F32)
        for s in seqs:
            for p in range(B_PAIRS):
                s_scr[s, p] = jnp.concatenate([jnp.concatenate([s0_ref[s, 2 * p], zero_blk], axis=1),
                                               jnp.concatenate([zero_blk, s0_ref[s, 2 * p + 1]], axis=1)], axis=0)
        prev_rkv[...] = sh_rkv_ref[...]
        prev_lora[...] = sh_lora_ref[...]

    n_rows = nb * chunk
    rsl = [slice(s * chunk, (s + 1) * chunk) for s in seqs]
    rows = lax.broadcasted_iota(jnp.int32, (n_rows, 1), 0)
    pb = rkv_ref[...].reshape(n_rows, 3 * B_WIDTH)
    lr = lora_ref[...].reshape(n_rows, LORA_PAD)
    pb_prev = pltpu.roll(pb, 1, 0)
    lr_prev = pltpu.roll(lr, 1, 0)
    for s in seqs:
        at = rows == s * chunk
        pb_prev = jnp.where(at, prev_rkv[s], pb_prev)
        lr_prev = jnp.where(at, prev_lora[s], lr_prev)
        prev_rkv[s] = pb[(s + 1) * chunk - 1:(s + 1) * chunk, :]
        prev_lora[s] = lr[(s + 1) * chunk - 1:(s + 1) * chunk, :]
    xs = pb + (pb_prev - pb) * mu_rkv_ref[...]
    xl = lr + (lr_prev - lr) * mu_lora_ref[...]
    r = xs[:, :B_WIDTH]
    k = xs[:, B_WIDTH:2 * B_WIDTH]
    v = xs[:, 2 * B_WIDTH:]
    o_aaa, o_gate = LORA_TILES[0] * LANES, (LORA_TILES[0] + LORA_TILES[1]) * LANES
    th_hi, th_lo = _split2(jnp.tanh(xl[:, :o_aaa]))
    z = w0_ref[...] + (_dg(th_hi, w2_ref[0], NN) + (_dg(th_hi, w2_ref[1], NN) + _dg(th_lo, w2_ref[0], NN)))
    lam = -DECAY_SCALE * _sigmoid(z)
    a = _sigmoid(a0_ref[...] + _mm(xl[:, o_aaa:o_gate], a2_ref[...]))
    gate = _mm(_sigmoid(xl[:, o_gate:]), g2_ref[...])
    kk = k * kk_ref[...]
    kmod = k * (1.0 + (a - 1.0) * ka_ref[...])
    if t_real < chunk:
        live = rows % chunk < t_real
        lam = jnp.where(live, lam, 0.0)
        kk = jnp.where(live, kk, 0.0)
        kmod = jnp.where(live, kmod, 0.0)
        v = jnp.where(live, v, 0.0)

    cum = _mm_exact_lhs(tri_ref[...], lam, terms=2)
    cum_end = [cum[(s + 1) * chunk - 1:(s + 1) * chunk, :] for s in seqs]
    e_in = jnp.exp(cum)
    e_prev = jnp.exp(cum - lam)
    e_neg = jnp.exp(-cum)
    e_end = [jnp.exp(cum_end[s] - cum[rsl[s]]) for s in seqs]
    g_end = [jnp.exp(cum_end[s]) for s in seqs]

    ri = lax.broadcasted_iota(jnp.int32, (chunk, LANES), 0)
    ci = lax.broadcasted_iota(jnp.int32, (chunk, LANES), 1) % HEAD
    strict = ci < ri
    lower = ci <= ri
    lane = lax.broadcasted_iota(jnp.int32, (1, LANES), 1)
    h1 = lane < HEAD
    bi = lax.broadcasted_iota(jnp.int32, (LANES, LANES), 0) // HEAD
    bj = lax.broadcasted_iota(jnp.int32, (LANES, LANES), 1) // HEAD
    block_diag = bi == bj
    levels = int(math.log2(chunk))
    zero = jnp.zeros((chunk, LANES), F32)
    units = [(s, p) for s in seqs for p in range(B_PAIRS)]
    idx = range(len(units))
    csl = [slice(p * LANES, (p + 1) * LANES) for _, p in units]
    take = lambda arr: [arr[rsl[s], csl[i]] for i, (s, _) in enumerate(units)]

    kappa = [x * _head_sums(x * x, h1, lambda q: lax.rsqrt(jnp.maximum(q, 1e-24))) for x in take(kk)]
    a_u, r_u, k_u, v_u = take(a), take(r), take(kmod), take(v)
    e_neg_u = take(e_neg)
    bb = [kappa[i] * a_u[i] for i in idx]
    a_t = [-x * e for x, e in zip(kappa, take(e_prev))]
    r_t = [x * e for x, e in zip(r_u, take(e_in))]
    m4 = []
    for i in idx:
        l4 = jnp.concatenate([jnp.where(h1, a_t[i], zero), jnp.where(h1, zero, a_t[i]),
                              jnp.where(h1, r_t[i], zero), jnp.where(h1, zero, r_t[i])], axis=0)
        m4.append(_mm(l4, _stack_halves(bb[i] * e_neg_u[i], k_u[i] * e_neg_u[i]), NT))
    na = [[jnp.where(strict, m4[i][j * chunk:(j + 1) * chunk], 0.0) for j in range(2)] for i in idx]
    nr = [[jnp.where(lower, m4[i][(2 + j) * chunk:(3 + j) * chunk], 0.0) for j in range(2)] for i in idx]
    zv = [_stack_halves(zero, pltpu.roll(v_u[i], HEAD, 1)).astype(BF16) for i in idx]
    zs = [[jnp.where(h1, a_t[i], _mm(na[i][0], zv[i])), jnp.where(h1, _mm(na[i][1], zv[i]), a_t[i])] for i in idx]
    ps = [[na[i][j][:, :chunk].astype(BF16) for j in range(2)] for i in idx]
    for lvl in range(levels):
        last = lvl == levels - 1
        for i in idx:
            for j in range(2):
                z_b = zs[i][j].astype(BF16)
                upd = _dg(ps[i][j], z_b if last else jnp.concatenate([z_b, ps[i][j]], axis=-1), NN)
                zs[i][j] = zs[i][j] + upd[:, :LANES]
                if not last:
                    ps[i][j] = upd[:, LANES:].astype(BF16)
    ta = [jnp.where(h1, zs[i][0], zs[i][1]) for i in idx]
    pv = [pltpu.roll(jnp.where(h1, zs[i][1], zs[i][0]), HEAD, 1) for i in idx]
    s_old = [s_scr[s, p] for s, p in units]
    s_b = [x.astype(BF16) for x in s_old]
    u = [_mm(ta[i], s_b[i], NT) + pv[i] for i in idx]
    uv = [_stack_halves(u[i], v_u[i]) for i in idx]
    uv_b = [x.astype(BF16) for x in uv]
    y = [_mm(r_t[i], s_b[i], NT) + jnp.where(h1, _mm(nr[i][0], uv_b[i]), _mm(nr[i][1], uv_b[i])) for i in idx]
    for i, (s, p) in enumerate(units):
        e = e_end[s][:, csl[i]]
        s_new = s_old[i] * g_end[s][:, csl[i]] + _mm(uv[i].T, _stack_halves(bb[i] * e, k_u[i] * e))
        s_scr[s, p] = jnp.where(block_diag, s_new, 0.0)

    mean = [_head_sums(y[i], h1) * (1.0 / HEAD) for i in idx]
    dlt = [y[i] - mean[i] for i in idx]
    var = [_head_sums(d * d, h1) * (1.0 / HEAD) for d in dlt]
    gate_u = take(gate)
    for i, (s, p) in enumerate(units):
        sl = csl[i]
        bonus = _head_sums(r_u[i] * k_u[i] * rk_ref[:, sl], h1)
        yn = dlt[i] * lax.rsqrt(var[i] + LNX_EPS) * lnw_ref[:, sl] + lnb_ref[:, sl]
        out_ref[s, :, sl] = (yn + bonus * v_u[i]) * gate_u[i]

    @pl.when(c == nc - 1)
    def _():
        for s in seqs:
            for p in range(B_PAIRS):
                s_pair = s_scr[s, p]
                s_out_ref[s, 2 * p] = s_pair[:HEAD, :HEAD]
                s_out_ref[s, 2 * p + 1] = s_pair[HEAD:, HEAD:]


def _rwkv(rkv, lora, sh_rkv, sh_lora, s0, prm, chunk, t_real):
    bsz, t, _ = rkv.shape
    nc = t // chunk
    nb = min(bsz, MAX_SEQS_PER_STEP)
    tri = np.kron(np.eye(nb), np.tril(np.ones((chunk, chunk)))).astype(np.float32)
    vec = lambda n: _const_spec((1, n))
    row = lambda x: x.reshape(1, -1)
    out, s_out = pl.pallas_call(
        functools.partial(_rwkv_kernel, nb=nb, chunk=chunk, t_real=t_real),
        grid=(bsz // nb, nc),
        in_specs=[pl.BlockSpec((nb, chunk, 3 * B_WIDTH), lambda b, c: (b, c, 0)),
                  pl.BlockSpec((nb, chunk, LORA_PAD), lambda b, c: (b, c, 0)),
                  pl.BlockSpec((nb, 1, 3 * B_WIDTH), lambda b, c: (b, 0, 0)),
                  pl.BlockSpec((nb, 1, LORA_PAD), lambda b, c: (b, 0, 0)),
                  pl.BlockSpec((nb, 2 * B_PAIRS, HEAD, HEAD), lambda b, c: (b, 0, 0, 0)),
                  vec(3 * B_WIDTH), vec(LORA_PAD), vec(B_WIDTH), _const_spec(prm["w2"].shape),
                  vec(B_WIDTH), _const_spec(prm["a2"].shape), _const_spec(prm["g2"].shape),
                  vec(B_WIDTH), vec(B_WIDTH), vec(B_WIDTH), vec(B_WIDTH), vec(B_WIDTH),
                  _const_spec(tri.shape)],
        out_specs=[pl.BlockSpec((nb, chunk, B_WIDTH), lambda b, c: (b, c, 0)),
                   pl.BlockSpec((nb, 2 * B_PAIRS, HEAD, HEAD), lambda b, c: (b, 0, 0, 0))],
        out_shape=[jax.ShapeDtypeStruct((bsz, t, B_WIDTH), F32),
                   jax.ShapeDtypeStruct((bsz, 2 * B_PAIRS, HEAD, HEAD), F32)],
        scratch_shapes=[pltpu.VMEM((nb, B_PAIRS, LANES, LANES), F32), pltpu.VMEM((nb, 1, 3 * B_WIDTH), F32),
                        pltpu.VMEM((nb, 1, LORA_PAD), F32)],
        compiler_params=_params(2),
        name="rwkv7",
    )(rkv, lora, sh_rkv[:, None], sh_lora[:, None], s0,
      row(prm["mu_rkv"]), row(prm["mu_lora"]), row(prm["w0"]), prm["w2"], row(prm["a0"]), prm["a2"], prm["g2"],
      row(prm["k_k"]), row(prm["k_a"]), row(prm["r_k"]), row(prm["lnx_w"]), row(prm["lnx_b"]),
      jnp.asarray(tri, BF16))
    return out, s_out


def _gla_kernel(pc_ref, s0_ref, lb_ref, gn_ref, sums_ref, out_ref, s_out_ref, s_scr, *, nb, chunk, t_real, layer):
    c = pl.program_id(1)
    nc = pl.num_programs(1)
    seqs = range(nb)

    @pl.when(c == 0)
    def _():
        for s in seqs:
            for h in range(C_HEADS):
                s_scr[s, h] = s0_ref[s, h].T

    n_rows = nb * chunk
    rsl = [slice(s * chunk, (s + 1) * chunk) for s in seqs]
    pc = pc_ref[...].reshape(n_rows, 4 * D_MODEL)
    raw = lb_ref[...]
    e = jnp.exp(raw - jnp.max(raw, axis=0, keepdims=True))
    sm = e / jnp.sum(e, axis=0, keepdims=True)
    lb = jnp.sum(sm[:layer + 1], axis=0, keepdims=True) - sm[0:1]
    xq = pc[:, :C_FDIM]
    q = xq * _sigmoid(xq)
    fg = lb + (1.0 - lb) * _sigmoid(pc[:, C_FDIM:2 * C_FDIM])
    k = 1.0 - fg
    logf = jnp.log(fg)
    v = pc[:, 2 * C_FDIM:2 * C_FDIM + D_MODEL]
    xg = pc[:, 2 * C_FDIM + D_MODEL:]
    if t_real < chunk:
        live = lax.broadcasted_iota(jnp.int32, (n_rows, 1), 0) % chunk < t_real
        logf = jnp.where(live, logf, 0.0)
        k = jnp.where(live, k, 0.0)
        v = jnp.where(live, v, 0.0)

    sums = _mm_exact_lhs(sums_ref[...], logf)
    cum = sums[:n_rows]
    base = sums[n_rows:]
    q_in = q * jnp.exp(cum)
    q_loc = q * jnp.exp(cum - base)
    k_loc = k * jnp.exp(jnp.minimum(base - cum, EXP_CLAMP))
    nsub = chunk // GLA_SUB
    k_stack, v_stack, k_end, g_end = [], [], [], []
    for s in seqs:
        cum_s, k_s = cum[rsl[s]], k[rsl[s]]
        cum_end = cum_s[chunk - 1:chunk, :]
        k_end.append(k_s * jnp.exp(cum_end - cum_s))
        g_end.append(jnp.exp(cum_end))
        k_var = [k_loc[rsl[s]]]
        for i in range(1, nsub):
            ref_i = cum_s[i * GLA_SUB - 1:i * GLA_SUB, :]
            k_var.append(k_s * jnp.exp(jnp.minimum(ref_i - cum_s, 0.0)))
        k_stack.append(jnp.concatenate(k_var, axis=0) if nsub > 1 else k_var[0])
        v_stack.append(jnp.concatenate([v[rsl[s]]] * nsub, axis=0) if nsub > 1 else v[rsl[s]])

    ri = lax.broadcasted_iota(jnp.int32, (chunk, nsub * chunk), 0)
    cc = lax.broadcasted_iota(jnp.int32, (chunk, nsub * chunk), 1)
    var = cc // chunk
    ci = cc % chunk
    same_sub = ci // GLA_SUB == ri // GLA_SUB
    att_mask = ((var == 0) & same_sub & (ci <= ri)) | ((ri // GLA_SUB == var) & (ci < var * GLA_SUB))

    units = [(s, h) for s in seqs for h in range(C_HEADS)]
    idx = range(len(units))
    csl = [slice(h * C_HEAD, (h + 1) * C_HEAD) for _, h in units]
    att = [jnp.where(att_mask, _mm(q_loc[rsl[s], csl[i]], k_stack[s][:, csl[i]], NT), 0.0)
           for i, (s, _) in enumerate(units)]
    s_old = [s_scr[s, h] for s, h in units]
    outs = [_mm(q_in[rsl[s], csl[i]], s_old[i], NT) + _mm(att[i], v_stack[s][:, csl[i]])
            for i, (s, _) in enumerate(units)]
    for i, (s, h) in enumerate(units):
        sl = csl[i]
        s_scr[s, h] = s_old[i] * g_end[s][:, sl] + _mm(_pad_rows_to_tile(v[rsl[s], sl]).T,
                                                       _pad_rows_to_tile(k_end[s][:, sl]))
    o = jnp.concatenate([jnp.concatenate(outs[s * C_HEADS:(s + 1) * C_HEADS], axis=-1) for s in seqs], axis=0)
    out_ref[...] = (_rms(o, gn_ref[...]) * (xg * _sigmoid(xg))).reshape(nb, chunk, D_MODEL)

    @pl.when(c == nc - 1)
    def _():
        for s in seqs:
            for h in range(C_HEADS):
                s_out_ref[s, h] = s_scr[s, h].T


def _gla(pc, s0, lb_raw, layer, gn, chunk, t_real):
    bsz, t, _ = pc.shape
    nc = t // chunk
    nb = min(bsz, MAX_SEQS_PER_STEP)
    idx = np.arange(chunk)
    eye = np.eye(nb)
    tri = np.kron(eye, np.tril(np.ones((chunk, chunk)))).astype(np.float32)
    sel = np.kron(eye, idx[None, :] < (idx[:, None] // GLA_SUB) * GLA_SUB).astype(np.float32)
    return pl.pallas_call(
        functools.partial(_gla_kernel, nb=nb, chunk=chunk, t_real=t_real, layer=layer),
        grid=(bsz // nb, nc),
        in_specs=[pl.BlockSpec((nb, chunk, 4 * D_MODEL), lambda b, c: (b, c, 0)),
                  pl.BlockSpec((nb, C_HEADS, C_HEAD, C_HEAD), lambda b, c: (b, 0, 0, 0)),
                  _const_spec(lb_raw.shape), _const_spec((1, D_MODEL)),
                  _const_spec((2 * nb * chunk, nb * chunk))],
        out_specs=[pl.BlockSpec((nb, chunk, D_MODEL), lambda b, c: (b, c, 0)),
                   pl.BlockSpec((nb, C_HEADS, C_HEAD, C_HEAD), lambda b, c: (b, 0, 0, 0))],
        out_shape=[jax.ShapeDtypeStruct((bsz, t, D_MODEL), F32),
                   jax.ShapeDtypeStruct((bsz, C_HEADS, C_HEAD, C_HEAD), F32)],
        scratch_shapes=[pltpu.VMEM((nb, C_HEADS, C_HEAD, C_HEAD), F32)],
        compiler_params=_params(2),
        name="hgrn2",
    )(pc, s0, lb_raw, gn.reshape(1, -1), jnp.asarray(np.concatenate([tri, sel]), BF16))


def _post_kernel(*refs, n_mix, final):
    x_ref = refs[0]
    mix_refs = refs[1:1 + n_mix]
    w_refs = refs[1 + n_mix:1 + 2 * n_mix]
    gm_ref, up_ref, down_ref = refs[1 + 2 * n_mix:4 + 2 * n_mix]
    gf_ref = refs[4 + 2 * n_mix] if final else None
    o_ref = refs[-1]
    x = x_ref[...]
    for m_ref, w_ref in zip(mix_refs, w_refs):
        x = x + jnp.dot(m_ref[...].astype(BF16), w_ref[...], preferred_element_type=F32)
    h = _rms(x, gm_ref[...]).astype(BF16)
    u = jnp.dot(h, up_ref[...], preferred_element_type=F32)
    u = jnp.square(jnp.maximum(u, 0.0)).astype(BF16)
    x = x + jnp.dot(u, down_ref[...], preferred_element_type=F32)
    if final:
        x = _rms(x, gf_ref[...])
    o_ref[...] = x


def _post(x, mixes, ws, gm, up, down, layer, gf=None, tm=512):
    m = x.shape[0]
    tm = min(tm, m)
    final = gf is not None
    row_spec = lambda n: pl.BlockSpec((tm, n), lambda i: (i, 0))
    layer_spec = lambda w: pl.BlockSpec((None,) + w.shape[1:], lambda i: (layer, 0, 0),
                                        pipeline_mode=pl.Buffered(1))
    rows = [x] + list(mixes)
    args = rows + [*ws, gm.reshape(1, -1), up, down]
    specs = ([row_spec(a.shape[1]) for a in rows] + [_const_spec(w.shape) for w in ws]
             + [_const_spec((1, D_MODEL)), layer_spec(up), layer_spec(down)])
    if final:
        args.append(gf.reshape(1, -1))
        specs.append(_const_spec((1, D_MODEL)))
    return pl.pallas_call(
        functools.partial(_post_kernel, n_mix=len(mixes), final=final),
        grid=(m // tm,), in_specs=specs, out_specs=row_spec(D_MODEL),
        out_shape=jax.ShapeDtypeStruct((m, D_MODEL), F32), compiler_params=_params(1), name="post_mlp",
    )(*args)


def _lora_pad_cols(x):
    parts, start = [], 0
    for n, tiles in zip(LORA_DIMS, LORA_TILES):
        part = x[..., start:start + n]
        parts.append(jnp.pad(part, [(0, 0)] * (part.ndim - 1) + [(0, tiles * LANES - n)]))
        start += n
    return jnp.concatenate(parts, axis=-1)


def _lora_unpad_cols(x):
    parts, start = [], 0
    for n, tiles in zip(LORA_DIMS, LORA_TILES):
        parts.append(x[..., start:start + n])
        start += tiles * LANES
    return jnp.concatenate(parts, axis=-1)


def _prepare(rel_bias, norm_mix, norm_mlp, norm_final, e_w_in, e_mu, e_w0, e_w2, e_a0, e_a2, e_g2, e_k_k, e_k_a,
             e_r_k, e_lnx_w, e_lnx_b, e_w_out, c_w_in, c_lb_raw, c_norm, c_w_out, mlp_up, mlp_down):
    pad_rows = lambda a, n: jnp.pad(a, [(0, n - a.shape[0]), (0, 0)])
    w_in = e_w_in[0]
    return dict(
        rel_bias=rel_bias,
        w_in=w_in.astype(BF16), w_lora=_lora_pad_cols(w_in[:, A_QKV + 3 * B_WIDTH:]).astype(BF16),
        rwkv=dict(mu_rkv=e_mu[0, :3 * B_WIDTH], mu_lora=_lora_pad_cols(e_mu[0, 3 * B_WIDTH:]),
                  w0=e_w0[0], w2=jnp.stack(_split2(pad_rows(e_w2[0], LORA_TILES[0] * LANES))), a0=e_a0[0],
                  a2=pad_rows(e_a2[0], LORA_TILES[1] * LANES).astype(BF16),
                  g2=pad_rows(e_g2[0], LORA_TILES[2] * LANES).astype(BF16),
                  k_k=e_k_k[0], k_a=e_k_a[0], r_k=e_r_k[0].reshape(-1),
                  lnx_w=e_lnx_w[0], lnx_b=e_lnx_b[0]),
        w_out_a=e_w_out[0, :A_OUT].astype(BF16), w_out_b=e_w_out[0, A_OUT:].astype(BF16),
        c_w_in=c_w_in[0].astype(BF16), c_w_out=c_w_out[0].astype(BF16), lb_raw=c_lb_raw, c_norm=c_norm[0],
        norm_mix=norm_mix, norm_mlp=norm_mlp, norm_final=norm_final,
        up=mlp_up.astype(BF16), down=mlp_down.astype(BF16))


def _trunk(x, prm, attend, sh_rkv, sh_lora, wkv0, c0, chunk):
    bsz, t_real, _ = x.shape
    t = -(-t_real // chunk) * chunk

    def seqs(a):
        a = a.reshape(bsz, t_real, a.shape[-1])
        return a if t == t_real else jnp.pad(a, ((0, 0), (0, t - t_real), (0, 0)))

    flat = lambda a: a[:, :t_real].reshape(bsz * t_real, a.shape[-1])
    xf = x.reshape(bsz * t_real, D_MODEL)
    assert A_QKV == 3 * B_WIDTH
    qkv, rkv, lora = _norm_proj(xf, prm["norm_mix"][0],
                                [(prm["w_in"], A_QKV, 0), (prm["w_in"], A_QKV, 1), (prm["w_lora"], LORA_PAD, 0)])
    qkv, rkv, lora = seqs(qkv), seqs(rkv), seqs(lora)
    a_out = attend(qkv)
    b_out, wkv = _rwkv(rkv, lora, sh_rkv, sh_lora, wkv0, prm["rwkv"], chunk, t_real)
    x1 = _post(xf, [flat(a_out), flat(b_out)], [prm["w_out_a"], prm["w_out_b"]], prm["norm_mlp"][0],
               prm["up"], prm["down"], 0)
    (pc,) = _norm_proj(x1, prm["norm_mix"][1], [(prm["c_w_in"], 4 * D_MODEL, 0)])
    c_out, c_state = _gla(seqs(pc), c0, prm["lb_raw"], 1, prm["c_norm"], chunk, t_real)
    y = _post(x1, [flat(c_out)], [prm["c_w_out"]], prm["norm_mlp"][1], prm["up"], prm["down"], 1,
              gf=prm["norm_final"])
    kv_rows = []
    for g, (w, _) in enumerate(A_GROUPS):
        n = min(w, t_real)
        rows = qkv[:, t_real - n:t_real]
        kv = jnp.stack([rows[..., (3 * part + g) * A_OUT:(3 * part + g + 1) * A_OUT] for part in (1, 2)], axis=2)
        kv_rows.append(kv.reshape(1, bsz, n, 2, 4, HEAD))
    shift = jnp.concatenate([rkv[:, t_real - 1], _lora_unpad_cols(lora[:, t_real - 1])], axis=-1)[None]
    return y.reshape(bsz, t_real, D_MODEL), kv_rows, shift, wkv[None], c_state[None]


def kernel(x_prompt, x_sample, cache_a0, cache_a1, cache_a2, state_b_shift, state_b_wkv, state_c, rel_bias, norm_mix, norm_mlp, norm_final, e_w_in, e_mu, e_w0, e_w2, e_a0, e_a2, e_g2, e_k_k, e_k_a, e_r_k, e_lnx_w, e_lnx_b, e_w_out, c_w_in, c_lb_raw, c_norm, c_w_out, mlp_up, mlp_down):
    prm = _prepare(rel_bias, norm_mix, norm_mlp, norm_final, e_w_in, e_mu, e_w0, e_w2, e_a0, e_a2, e_g2, e_k_k,
                   e_k_a, e_r_k, e_lnx_w, e_lnx_b, e_w_out, c_w_in, c_lb_raw, c_norm, c_w_out, mlp_up, mlp_down)

    bp, tp, _ = x_prompt.shape

    def attend_prompt(qkv):
        return _attn_prompt(qkv, prm["rel_bias"])

    y_p, p_kv, p_shift, p_wkv, p_c = _trunk(
        x_prompt, prm, attend_prompt,
        jnp.zeros((bp, 3 * B_WIDTH), F32), jnp.zeros((bp, LORA_PAD), F32),
        jnp.zeros((bp, 2 * B_PAIRS, HEAD, HEAD), F32), jnp.zeros((bp, C_HEADS, C_HEAD, C_HEAD), F32),
        chunk=CHUNK)

    bs, ts, _ = x_sample.shape
    caches = [jnp.transpose(c[0], (0, 2, 3, 4, 1)) for c in (cache_a0, cache_a1, cache_a2)]

    def attend_sample(qkv):
        return _attn_sample(qkv, caches, prm["rel_bias"], ts)

    sh = state_b_shift[0]
    y_s, s_kv, s_shift, s_wkv, s_c = _trunk(
        x_sample, prm, attend_sample, sh[:, :3 * B_WIDTH], _lora_pad_cols(sh[:, 3 * B_WIDTH:]),
        state_b_wkv[0], state_c[0], chunk=SAMPLE_ROWS)

    return (y_p, y_s, p_kv[0], p_kv[1], p_kv[2], p_shift, p_wkv, p_c,
            s_kv[0], s_kv[1], s_kv[2], s_shift, s_wkv, s_c)
```

```python
import functools
import math

import jax
import jax.numpy as jnp
import numpy as np
from jax import lax
from jax.experimental import pallas as pl
from jax.experimental.pallas import tpu as pltpu

F32 = jnp.float32
BF16 = jnp.bfloat16

LANES = 128
D_MODEL = 1024
EPS = 1e-6
LNX_EPS = 64e-5
HEAD = 64
CHUNK = 64
SAMPLE_ROWS = 16
A_GROUPS = ((128, 1), (512, 4), (2048, 16))
A_KEYS = 128
A_TILES = 4
A_BLOCK = A_KEYS * A_GROUPS[-1][1]
A_STRIDE = 4
A_QKV = 2304
A_OUT = 256
N_BUCKETS = 32
BUCKET_MAX_DIST = 2048
B_WIDTH = 768
B_PAIRS = B_WIDTH // LANES
LORA_DIMS = (64, 64, 160)
LORA_TILES = (1, 1, 2)
LORA_PAD = sum(LORA_TILES) * LANES
C_FDIM = 1024
C_HEADS = 8
C_HEAD = 128
NEG = -1e30
LOG2E = math.log2(math.e)
DECAY_SCALE = math.exp(-0.5)
GLA_SUB = 16
EXP_CLAMP = 80.0
MAX_SEQS_PER_STEP = 4

VMEM_LIMIT = 56 * 1024 * 1024
assert 2 * HEAD == LANES and C_HEAD == LANES and A_KEYS == LANES

NN = (((1,), (0,)), ((), ()))
NT = (((1,), (1,)), ((), ()))


def _dg(a, b, dims):
    return lax.dot_general(a, b, dims, preferred_element_type=F32)


def _split2(x):
    hi = x.astype(BF16)
    lo = (x - hi.astype(F32)).astype(BF16)
    return hi, lo


def _split3(x):
    hi = x.astype(BF16)
    r1 = x - hi.astype(F32)
    mid = r1.astype(BF16)
    lo = (r1 - mid.astype(F32)).astype(BF16)
    return hi, mid, lo


def _mm(a, b, dims=NN):
    return _dg(a.astype(BF16), b.astype(BF16), dims)


def _mm_exact_lhs(a_bf16, b, terms=3):
    if terms == 2:
        bh, bl = _split2(b)
        return _dg(a_bf16, bh, NN) + _dg(a_bf16, bl, NN)
    bh, bm, bl = _split3(b)
    return _dg(a_bf16, bh, NN) + (_dg(a_bf16, bm, NN) + _dg(a_bf16, bl, NN))


def _pad_rows_to_tile(x):
    n = x.shape[0]
    if n == LANES:
        return x
    return jnp.concatenate([x, jnp.zeros((LANES - n, LANES), F32)], axis=0)


def _stack_halves(top, bottom):
    n = top.shape[0]
    half = LANES // 2
    if n == half:
        return jnp.concatenate([top, bottom], axis=0)
    pad = jnp.zeros((half - n, LANES), F32)
    return jnp.concatenate([top, pad, bottom, pad], axis=0)


def _sigmoid(x):
    return 0.5 * jnp.tanh(0.5 * x) + 0.5


def _rms(x, g):
    ms = jnp.mean(x * x, axis=-1, keepdims=True)
    return x * lax.rsqrt(ms + EPS) * g


def _const_spec(shape):
    nd = len(shape)
    return pl.BlockSpec(shape, lambda *_: (0,) * nd, pipeline_mode=pl.Buffered(1))


def _params(n_grid):
    return pltpu.CompilerParams(dimension_semantics=("arbitrary",) * n_grid,
                                vmem_limit_bytes=VMEM_LIMIT)


def _norm_proj_kernel(x_ref, g_ref, *refs):
    n = len(refs) // 2
    h = _rms(x_ref[...], g_ref[...]).astype(BF16)
    for w_ref, o_ref in zip(refs[:n], refs[n:]):
        o_ref[...] = jnp.dot(h, w_ref[...], preferred_element_type=F32)


def _norm_proj(x, g, ws, tm=512):
    m = x.shape[0]
    tm = min(tm, m)
    return pl.pallas_call(
        _norm_proj_kernel,
        grid=(m // tm,),
        in_specs=[pl.BlockSpec((tm, D_MODEL), lambda i: (i, 0)), _const_spec((1, D_MODEL))]
        + [pl.BlockSpec((D_MODEL, n), lambda i, j=j: (0, j), pipeline_mode=pl.Buffered(1)) for _, n, j in ws],
        out_specs=[pl.BlockSpec((tm, n), lambda i: (i, 0)) for _, n, _ in ws],
        out_shape=[jax.ShapeDtypeStruct((m, n), F32) for _, n, _ in ws],
        compiler_params=_params(1),
        name="norm_proj",
    )(x, g.reshape(1, D_MODEL), *[w for w, _, _ in ws])


def _t5_bucket_np(dist):
    max_exact = N_BUCKETS // 2
    d = np.maximum(dist, 1).astype(np.float32)
    large = max_exact + (np.log(d / np.float32(max_exact)) / np.float32(math.log(BUCKET_MAX_DIST / max_exact))
                         * np.float32(N_BUCKETS - max_exact)).astype(np.int32)
    large = np.minimum(large, N_BUCKETS - 1)
    return np.where(dist < max_exact, dist, large).astype(np.int32)


def _bias_from_buckets(idx, rb_ref, head):
    acc = jnp.full(idx.shape, NEG, F32)
    for b in range(N_BUCKETS):
        acc = jnp.where(idx == b, rb_ref[b, head], acc)
    return acc


def _tile_rows(start, dil):
    if dil > 1:
        return pl.ds(start, A_KEYS, stride=dil)
    return pl.ds(start if isinstance(start, int) else pl.multiple_of(start, A_KEYS), A_KEYS)


def _attn_prompt_kernel(rb_ref, bkt_ref, *refs):
    ng = len(A_GROUPS)
    q_refs, kc_refs, vc_refs, kp_refs, vp_refs = (refs[i * ng:(i + 1) * ng] for i in range(5))
    out_ref, bias_scr, o_scr, m_scr, d_scr, stage_scr = refs[5 * ng:]
    pair = pl.program_id(2)
    first = (pl.program_id(0) == 0) & (pl.program_id(1) == 0) & (pair == 0)

    @pl.when(first)
    def _():
        for g in range(ng):
            for h in range(4):
                bias_scr[g * 4 + h] = _bias_from_buckets(bkt_ref[g], rb_ref, g * 4 + h) * LOG2E

    col = lax.broadcasted_iota(jnp.int32, (1, 2 * A_KEYS), 1)
    prev_pen = jnp.where((col < A_KEYS) & (pl.program_id(1) == 0), NEG, 0.0).astype(F32)
    h1 = lax.broadcasted_iota(jnp.int32, (1, LANES), 1) < HEAD
    h2 = jnp.logical_not(h1)

    def attend(g, tiles):
        n = range(len(tiles))
        get = lambda src: src[0][src[1], :]
        bias_in = [bias_scr[g * 4 + 2 * pair + j] for j in range(2)]
        bias_edge = [b + prev_pen for b in bias_in]
        q = [get(t[0]) * (HEAD ** -0.5 * LOG2E) for t in tiles]
        k = [jnp.concatenate([get(t[3]), get(t[1])], axis=0).astype(BF16) for t in tiles]
        v = [jnp.concatenate([get(t[4]), get(t[2])], axis=0).astype(BF16) for t in tiles]
        s = [[_dg(jnp.where(mine, q[i], 0.0).astype(BF16), k[i], NT) + (bias_edge if tiles[i][6] else bias_in)[j]
              for j, mine in enumerate((h1, h2))] for i in n]
        m = [[jnp.max(x, axis=-1, keepdims=True) for x in s[i]] for i in n]
        p = [[jnp.exp2(x - mx) for x, mx in zip(s[i], m[i])] for i in n]
        den = [[jnp.sum(x, axis=-1, keepdims=True) for x in p[i]] for i in n]
        o = [[_dg(x.astype(BF16), v[i], NN) for x in p[i]] for i in n]
        for i in n:
            (o_dst, m_dst, d_dst), rows = tiles[i][5]
            o_dst[rows, :] = jnp.where(h1, o[i][0], o[i][1])
            m_dst[rows, :] = jnp.where(h1, m[i][0], m[i][1])
            d_dst[rows, :] = jnp.where(h1, den[i][0], den[i][1])

    n_tiles = A_BLOCK // A_KEYS
    for g, (_, dil) in enumerate(A_GROUPS):
        span = A_KEYS * dil
        single_span = span == A_BLOCK
        cur = (q_refs[g], kc_refs[g], vc_refs[g])
        before = (kp_refs[g], vp_refs[g])
        outs = (o_scr.at[g], m_scr.at[g], d_scr.at[g])

        def batch(it, static, g=g, dil=dil, span=span, single_span=single_span, cur=cur, before=before, outs=outs):
            if dil > A_STRIDE:
                inner = dil // A_STRIDE
                assert single_span and inner == A_TILES and inner <= A_STRIDE
                union = pl.ds(it, A_KEYS * inner, stride=A_STRIDE)
                n_in = len(cur + before)
                for a, ref in enumerate(cur + before):
                    stage_scr[a] = ref[union, :]
                slabs = [stage_scr.at[a] for a in range(n_in + len(outs))]
                attend(g, [tuple((slab, pl.ds(u, A_KEYS, stride=inner)) for slab in slabs[:n_in])
                           + ((tuple(slabs[n_in:]), pl.ds(u, A_KEYS, stride=inner)), True) for u in range(A_TILES)])
                for slab, dst in zip(slabs[n_in:], outs):
                    dst[union, :] = slab[...]
                return
            tiles = []
            for u in range(A_TILES):
                t = it * A_TILES + u
                s_idx, r = (t // dil, t % dil) if static else _span_residue(it, u, dil)
                edge = single_span or (static and s_idx == 0)
                rows = _tile_rows(s_idx * span + r, dil)
                prows = _tile_rows(r if edge else (s_idx - 1) * span + r, dil)
                tiles.append(tuple((ref, rows) for ref in cur)
                             + tuple((ref, prows) for ref in (before if edge else cur[1:])) + ((outs, rows), edge))
            attend(g, tiles)

        batch(0, True)

        for it in range(1, n_tiles // A_TILES):
            batch(it, True)

    def merge(i, carry):
        rows = pl.ds(pl.multiple_of(i * A_KEYS, A_KEYS), A_KEYS)
        ms = [m_scr[g, rows, :] for g in range(ng)]
        top = functools.reduce(jnp.maximum, ms)
        ws = [jnp.exp2(x - top) for x in ms]
        num = functools.reduce(lambda a, b: a + b, [w * o_scr[g, rows, :] for g, w in enumerate(ws)])
        den = functools.reduce(lambda a, b: a + b, [w * d_scr[g, rows, :] for g, w in enumerate(ws)])
        out_ref[rows, :] = num / den
        return carry
    lax.fori_loop(0, n_tiles, merge, 0)


def _span_residue(it, u, dil):
    if dil >= A_TILES:
        per = dil // A_TILES
        return it // per, (it % per) * A_TILES + u
    return it * (A_TILES // dil) + u // dil, u % dil


def _prompt_bucket_map(dil):
    qi = np.arange(A_KEYS)[:, None]
    ki = np.arange(2 * A_KEYS)[None, :]
    j = qi + A_KEYS - ki
    return np.where((j >= 0) & (j <= A_KEYS), _t5_bucket_np(np.clip(j, 0, A_KEYS) * dil), -1).astype(np.int32)


def _attn_prompt(qkv, rel_bias):
    bsz, t, _ = qkv.shape
    ng = len(A_GROUPS)

    def cur(part, g):
        return pl.BlockSpec((None, A_BLOCK, LANES), lambda b, i, pair: (b, i, (part * ng + g) * 2 + pair))

    def before(part, g):
        span = A_KEYS * A_GROUPS[g][1]
        per = A_BLOCK // span
        return pl.BlockSpec((None, span, LANES),
                            lambda b, i, pair: (b, jnp.maximum(i * per - 1, 0), (part * ng + g) * 2 + pair))

    specs = [cur(part, g) for part in range(3) for g in range(ng)] + \
            [before(part, g) for part in (1, 2) for g in range(ng)]
    maps = jnp.asarray(np.stack([_prompt_bucket_map(d) for _, d in A_GROUPS]))
    return pl.pallas_call(
        _attn_prompt_kernel,
        grid=(bsz, t // A_BLOCK, 2),
        in_specs=[pl.BlockSpec(memory_space=pltpu.SMEM), _const_spec(maps.shape)] + specs,
        out_specs=pl.BlockSpec((None, A_BLOCK, LANES), lambda b, i, pair: (b, i, pair)),
        out_shape=jax.ShapeDtypeStruct((bsz, t, A_OUT), F32),
        scratch_shapes=[pltpu.VMEM((4 * ng, A_KEYS, 2 * A_KEYS), F32)] + [pltpu.VMEM((ng, A_BLOCK, LANES), F32)] * 3
        + [pltpu.VMEM((8, A_KEYS * A_TILES, LANES), F32)],
        compiler_params=_params(3),
        name="attn_prompt",
    )(rel_bias, maps, *([qkv] * (5 * ng)))


def _merge_groups(outs, lses):
    m = functools.reduce(jnp.maximum, lses)
    ws = [jnp.exp(l - m) for l in lses]
    num = functools.reduce(lambda a, b: a + b, [w * o for w, o in zip(ws, outs)])
    return num / functools.reduce(lambda a, b: a + b, ws)


def _attn_sample_kernel(rb_ref, bc0, bc1, bc2, bn0, bn1, bn2, qkv_ref, c0_ref, c1_ref, c2_ref, out_ref,
                        bias_c0, bias_c1, bias_c2, bias_n):
    bias_c = (bias_c0, bias_c1, bias_c2)

    @pl.when(pl.program_id(0) == 0)
    def _():
        for g, (bc, bn) in enumerate(((bc0, bn0), (bc1, bn1), (bc2, bn2))):
            for h in range(4):
                bias_c[g][h] = _bias_from_buckets(bc[...], rb_ref, g * 4 + h)
                bias_n[g * 4 + h] = _bias_from_buckets(bn[...], rb_ref, g * 4 + h)

    qkv = qkv_ref[...]
    c_refs = (c0_ref, c1_ref, c2_ref)
    units = [(g, h) for g in range(len(A_GROUPS)) for h in range(4)]
    cols = [(g * 4 + h) * HEAD for g, h in units]
    q = [(qkv[:, c:c + HEAD] * (HEAD ** -0.5)).astype(BF16) for c in cols]
    kn = [qkv[:, 3 * A_OUT + c:3 * A_OUT + c + HEAD].astype(BF16) for c in cols]
    vn = [qkv[:, 6 * A_OUT + c:6 * A_OUT + c + HEAD].astype(BF16) for c in cols]
    n = range(len(units))
    s_c = [_dg(q[i], c_refs[g][0, h].astype(BF16), NN) + bias_c[g][h] for i, (g, h) in enumerate(units)]
    s_n = [_dg(q[i], kn[i], NT) + bias_n[i] for i in n]
    m = [jnp.maximum(jnp.max(s_c[i], axis=-1, keepdims=True), jnp.max(s_n[i], axis=-1, keepdims=True)) for i in n]
    p_c = [jnp.exp(s_c[i] - m[i]) for i in n]
    p_n = [jnp.exp(s_n[i] - m[i]) for i in n]
    den = [jnp.sum(p_c[i], axis=-1, keepdims=True) + jnp.sum(p_n[i], axis=-1, keepdims=True) for i in n]
    o = [(_dg(p_c[i].astype(BF16), c_refs[g][1, h].astype(BF16), NT) + _dg(p_n[i].astype(BF16), vn[i], NN)) / den[i]
         for i, (g, h) in enumerate(units)]
    lse = [m[i] + jnp.log(den[i]) for i in n]
    out_ref[...] = jnp.concatenate([_merge_groups(o[h::4], lse[h::4]) for h in range(4)], axis=-1)


def _sample_bucket_maps(window, dil, tp, t_real):
    t = np.arange(tp)[:, None]
    dist_c = window + t - np.arange(window)[None, :]
    ok_c = (dist_c % dil == 0) & (dist_c // dil <= A_KEYS) & (t < t_real)
    dist_n = t - np.arange(tp)[None, :]
    ok_n = (dist_n >= 0) & (dist_n % dil == 0) & (dist_n // dil <= A_KEYS)
    mc = np.where(ok_c, _t5_bucket_np(np.maximum(dist_c, 0)), -1).astype(np.int32)
    mn = np.where(ok_n, _t5_bucket_np(np.maximum(dist_n, 0)), -1).astype(np.int32)
    return mc, mn


def _attn_sample(qkv, caches, rel_bias, t_real):
    bsz, tp, _ = qkv.shape
    maps = [_sample_bucket_maps(w, d, tp, t_real) for w, d in A_GROUPS]
    mcs = [jnp.asarray(m[0]) for m in maps]
    mns = [jnp.asarray(m[1]) for m in maps]
    return pl.pallas_call(
        _attn_sample_kernel,
        grid=(bsz,),
        in_specs=[pl.BlockSpec(memory_space=pltpu.SMEM)]
        + [_const_spec(m.shape) for m in mcs] + [_const_spec(m.shape) for m in mns]
        + [pl.BlockSpec((None, tp, A_QKV), lambda b: (b, 0, 0))]
        + [pl.BlockSpec((None, 2, 4, HEAD, w), lambda b: (b, 0, 0, 0, 0)) for w, _ in A_GROUPS],
        out_specs=pl.BlockSpec((None, tp, A_OUT), lambda b: (b, 0, 0)),
        out_shape=jax.ShapeDtypeStruct((bsz, tp, A_OUT), F32),
        scratch_shapes=[pltpu.VMEM((4, tp, w), F32) for w, _ in A_GROUPS] + [pltpu.VMEM((12, tp, tp), F32)],
        compiler_params=_params(1),
        name="attn_sample",
    )(rel_bias, *mcs, *mns, qkv, *caches)


def _head_sums(x, h1, fn=lambda s: s):
    first = jnp.sum(jnp.where(h1, x, 0.0), axis=-1, keepdims=True)
    second = jnp.sum(jnp.where(h1, 0.0, x), axis=-1, keepdims=True)
    return jnp.where(h1, fn(first), fn(second))


def _rwkv_kernel(rkv_ref, lora_ref, sh_rkv_ref, sh_lora_ref, s0_ref, mu_rkv_ref, mu_lora_ref, w0_ref, w2_ref,
                 a0_ref, a2_ref, g2_ref, kk_ref, ka_ref, rk_ref, lnw_ref, lnb_ref, tri_ref,
                 out_ref, s_out_ref, s_scr, prev_rkv, prev_lora, *, nb, chunk, t_real):
    c = pl.program_id(1)
    nc = pl.num_programs(1)
    seqs = range(nb)

    @pl.when(c == 0)
    def _():
        zero_blk = jnp.zeros((HEAD, HEAD), F32)
        for s in seqs:
            for p in range(B_PAIRS):
                s_scr[s, p] = jnp.concatenate([jnp.concatenate([s0_ref[s, 2 * p], zero_blk], axis=1),
                                               jnp.concatenate([zero_blk, s0_ref[s, 2 * p + 1]], axis=1)], axis=0)
        prev_rkv[...] = sh_rkv_ref[...]
        prev_lora[...] = sh_lora_ref[...]

    n_rows = nb * chunk
    rsl = [slice(s * chunk, (s + 1) * chunk) for s in seqs]
    rows = lax.broadcasted_iota(jnp.int32, (n_rows, 1), 0)
    pb = rkv_ref[...].reshape(n_rows, 3 * B_WIDTH)
    lr = lora_ref[...].reshape(n_rows, LORA_PAD)
    pb_prev = pltpu.roll(pb, 1, 0)
    lr_prev = pltpu.roll(lr, 1, 0)
    for s in seqs:
        at = rows == s * chunk
        pb_prev = jnp.where(at, prev_rkv[s], pb_prev)
        lr_prev = jnp.where(at, prev_lora[s], lr_prev)
        prev_rkv[s] = pb[(s + 1) * chunk - 1:(s + 1) * chunk, :]
        prev_lora[s] = lr[(s + 1) * chunk - 1:(s + 1) * chunk, :]
    xs = pb + (pb_prev - pb) * mu_rkv_ref[...]
    xl = lr + (lr_prev - lr) * mu_lora_ref[...]
    r = xs[:, :B_WIDTH]
    k = xs[:, B_WIDTH:2 * B_WIDTH]
    v = xs[:, 2 * B_WIDTH:]
    o_aaa, o_gate = LORA_TILES[0] * LANES, (LORA_TILES[0] + LORA_TILES[1]) * LANES
    th_hi, th_lo = _split2(jnp.tanh(xl[:, :o_aaa]))
    z = w0_ref[...] + (_dg(th_hi, w2_ref[0], NN) + (_dg(th_hi, w2_ref[1], NN) + _dg(th_lo, w2_ref[0], NN)))
    lam = -DECAY_SCALE * _sigmoid(z)
    a = _sigmoid(a0_ref[...] + _mm(xl[:, o_aaa:o_gate], a2_ref[...]))
    gate = _mm(_sigmoid(xl[:, o_gate:]), g2_ref[...])
    kk = k * kk_ref[...]
    kmod = k * (1.0 + (a - 1.0) * ka_ref[...])
    if t_real < chunk:
        live = rows % chunk < t_real
        lam = jnp.where(live, lam, 0.0)
        kk = jnp.where(live, kk, 0.0)
        kmod = jnp.where(live, kmod, 0.0)
        v = jnp.where(live, v, 0.0)

    cum = _mm_exact_lhs(tri_ref[...], lam, terms=2)
    cum_end = [cum[(s + 1) * chunk - 1:(s + 1) * chunk, :] for s in seqs]
    e_in = jnp.exp(cum)
    e_prev = jnp.exp(cum - lam)
    e_neg = jnp.exp(-cum)
    e_end = [jnp.exp(cum_end[s] - cum[rsl[s]]) for s in seqs]
    g_end = [jnp.exp(cum_end[s]) for s in seqs]

    ri = lax.broadcasted_iota(jnp.int32, (chunk, LANES), 0)
    ci = lax.broadcasted_iota(jnp.int32, (chunk, LANES), 1) % HEAD
    strict = ci < ri
    lower = ci <= ri
    lane = lax.broadcasted_iota(jnp.int32, (1, LANES), 1)
    h1 = lane < HEAD
    bi = lax.broadcasted_iota(jnp.int32, (LANES, LANES), 0) // HEAD
    bj = lax.broadcasted_iota(jnp.int32, (LANES, LANES), 1) // HEAD
    block_diag = bi == bj
    levels = int(math.log2(chunk))
    zero = jnp.zeros((chunk, LANES), F32)
    units = [(s, p) for s in seqs for p in range(B_PAIRS)]
    idx = range(len(units))
    csl = [slice(p * LANES, (p + 1) * LANES) for _, p in units]
    take = lambda arr: [arr[rsl[s], csl[i]] for i, (s, _) in enumerate(units)]

    kappa = [x * _head_sums(x * x, h1, lambda q: lax.rsqrt(jnp.maximum(q, 1e-24))) for x in take(kk)]
    a_u, r_u, k_u, v_u = take(a), take(r), take(kmod), take(v)
    e_neg_u = take(e_neg)
    bb = [kappa[i] * a_u[i] for i in idx]
    a_t = [-x * e for x, e in zip(kappa, take(e_prev))]
    r_t = [x * e for x, e in zip(r_u, take(e_in))]
    m4 = []
    for i in idx:
        l4 = jnp.concatenate([jnp.where(h1, a_t[i], zero), jnp.where(h1, zero, a_t[i]),
                              jnp.where(h1, r_t[i], zero), jnp.where(h1, zero, r_t[i])], axis=0)
        m4.append(_mm(l4, _stack_halves(bb[i] * e_neg_u[i], k_u[i] * e_neg_u[i]), NT))
    na = [[jnp.where(strict, m4[i][j * chunk:(j + 1) * chunk], 0.0) for j in range(2)] for i in idx]
    nr = [[jnp.where(lower, m4[i][(2 + j) * chunk:(3 + j) * chunk], 0.0) for j in range(2)] for i in idx]
    zv = [_stack_halves(zero, pltpu.roll(v_u[i], HEAD, 1)).astype(BF16) for i in idx]
    zs = [[jnp.where(h1, a_t[i], _mm(na[i][0], zv[i])), jnp.where(h1, _mm(na[i][1], zv[i]), a_t[i])] for i in idx]
    ps = [[na[i][j][:, :chunk].astype(BF16) for j in range(2)] for i in idx]
    for lvl in range(levels):
        last = lvl == levels - 1
        for i in idx:
            for j in range(2):
                z_b = zs[i][j].astype(BF16)
                upd = _dg(ps[i][j], z_b if last else jnp.concatenate([z_b, ps[i][j]], axis=-1), NN)
                zs[i][j] = zs[i][j] + upd[:, :LANES]
                if not last:
                    ps[i][j] = upd[:, LANES:].astype(BF16)
    ta = [jnp.where(h1, zs[i][0], zs[i][1]) for i in idx]
    pv = [pltpu.roll(jnp.where(h1, zs[i][1], zs[i][0]), HEAD, 1) for i in idx]
    s_old = [s_scr[s, p] for s, p in units]
    s_b = [x.astype(BF16) for x in s_old]
    u = [_mm(ta[i], s_b[i], NT) + pv[i] for i in idx]
    uv = [_stack_halves(u[i], v_u[i]) for i in idx]
    uv_b = [x.astype(BF16) for x in uv]
    y = [_mm(r_t[i], s_b[i], NT) + jnp.where(h1, _mm(nr[i][0], uv_b[i]), _mm(nr[i][1], uv_b[i])) for i in idx]
    for i, (s, p) in enumerate(units):
        e = e_end[s][:, csl[i]]
        s_new = s_old[i] * g_end[s][:, csl[i]] + _mm(uv[i].T, _stack_halves(bb[i] * e, k_u[i] * e))
        s_scr[s, p] = jnp.where(block_diag, s_new, 0.0)

    mean = [_head_sums(y[i], h1) * (1.0 / HEAD) for i in idx]
    dlt = [y[i] - mean[i] for i in idx]
    var = [_head_sums(d * d, h1) * (1.0 / HEAD) for d in dlt]
    gate_u = take(gate)
    for i, (s, p) in enumerate(units):
        sl = csl[i]
        bonus = _head_sums(r_u[i] * k_u[i] * rk_ref[:, sl], h1)
        yn = dlt[i] * lax.rsqrt(var[i] + LNX_EPS) * lnw_ref[:, sl] + lnb_ref[:, sl]
        out_ref[s, :, sl] = (yn + bonus * v_u[i]) * gate_u[i]

    @pl.when(c == nc - 1)
    def _():
        for s in seqs:
            for p in range(B_PAIRS):
                s_pair = s_scr[s, p]
                s_out_ref[s, 2 * p] = s_pair[:HEAD, :HEAD]
                s_out_ref[s, 2 * p + 1] = s_pair[HEAD:, HEAD:]


def _rwkv(rkv, lora, sh_rkv, sh_lora, s0, prm, chunk, t_real):
    bsz, t, _ = rkv.shape
    nc = t // chunk
    nb = min(bsz, MAX_SEQS_PER_STEP)
    tri = np.kron(np.eye(nb), np.tril(np.ones((chunk, chunk)))).astype(np.float32)
    vec = lambda n: _const_spec((1, n))
    row = lambda x: x.reshape(1, -1)
    out, s_out = pl.pallas_call(
        functools.partial(_rwkv_kernel, nb=nb, chunk=chunk, t_real=t_real),
        grid=(bsz // nb, nc),
        in_specs=[pl.BlockSpec((nb, chunk, 3 * B_WIDTH), lambda b, c: (b, c, 0)),
                  pl.BlockSpec((nb, chunk, LORA_PAD), lambda b, c: (b, c, 0)),
                  pl.BlockSpec((nb, 1, 3 * B_WIDTH), lambda b, c: (b, 0, 0)),
                  pl.BlockSpec((nb, 1, LORA_PAD), lambda b, c: (b, 0, 0)),
                  pl.BlockSpec((nb, 2 * B_PAIRS, HEAD, HEAD), lambda b, c: (b, 0, 0, 0)),
                  vec(3 * B_WIDTH), vec(LORA_PAD), vec(B_WIDTH), _const_spec(prm["w2"].shape),
                  vec(B_WIDTH), _const_spec(prm["a2"].shape), _const_spec(prm["g2"].shape),
                  vec(B_WIDTH), vec(B_WIDTH), vec(B_WIDTH), vec(B_WIDTH), vec(B_WIDTH),
                  _const_spec(tri.shape)],
        out_specs=[pl.BlockSpec((nb, chunk, B_WIDTH), lambda b, c: (b, c, 0)),
                   pl.BlockSpec((nb, 2 * B_PAIRS, HEAD, HEAD), lambda b, c: (b, 0, 0, 0))],
        out_shape=[jax.ShapeDtypeStruct((bsz, t, B_WIDTH), F32),
                   jax.ShapeDtypeStruct((bsz, 2 * B_PAIRS, HEAD, HEAD), F32)],
        scratch_shapes=[pltpu.VMEM((nb, B_PAIRS, LANES, LANES), F32), pltpu.VMEM((nb, 1, 3 * B_WIDTH), F32),
                        pltpu.VMEM((nb, 1, LORA_PAD), F32)],
        compiler_params=_params(2),
        name="rwkv7",
    )(rkv, lora, sh_rkv[:, None], sh_lora[:, None], s0,
      row(prm["mu_rkv"]), row(prm["mu_lora"]), row(prm["w0"]), prm["w2"], row(prm["a0"]), prm["a2"], prm["g2"],
      row(prm["k_k"]), row(prm["k_a"]), row(prm["r_k"]), row(prm["lnx_w"]), row(prm["lnx_b"]),
      jnp.asarray(tri, BF16))
    return out, s_out


def _gla_kernel(pc_ref, s0_ref, lb_ref, gn_ref, sums_ref, out_ref, s_out_ref, s_scr, *, nb, chunk, t_real, layer):
    c = pl.program_id(1)
    nc = pl.num_programs(1)
    seqs = range(nb)

    @pl.when(c == 0)
    def _():
        for s in seqs:
            for h in range(C_HEADS):
                s_scr[s, h] = s0_ref[s, h].T

    n_rows = nb * chunk
    rsl = [slice(s * chunk, (s + 1) * chunk) for s in seqs]
    pc = pc_ref[...].reshape(n_rows, 4 * D_MODEL)
    raw = lb_ref[...]
    e = jnp.exp(raw - jnp.max(raw, axis=0, keepdims=True))
    sm = e / jnp.sum(e, axis=0, keepdims=True)
    lb = jnp.sum(sm[:layer + 1], axis=0, keepdims=True) - sm[0:1]
    xq = pc[:, :C_FDIM]
    q = xq * _sigmoid(xq)
    fg = lb + (1.0 - lb) * _sigmoid(pc[:, C_FDIM:2 * C_FDIM])
    k = 1.0 - fg
    logf = jnp.log(fg)
    v = pc[:, 2 * C_FDIM:2 * C_FDIM + D_MODEL]
    xg = pc[:, 2 * C_FDIM + D_MODEL:]
    if t_real < chunk:
        live = lax.broadcasted_iota(jnp.int32, (n_rows, 1), 0) % chunk < t_real
        logf = jnp.where(live, logf, 0.0)
        k = jnp.where(live, k, 0.0)
        v = jnp.where(live, v, 0.0)

    sums = _mm_exact_lhs(sums_ref[...], logf)
    cum = sums[:n_rows]
    base = sums[n_rows:]
    q_in = q * jnp.exp(cum)
    q_loc = q * jnp.exp(cum - base)
    k_loc = k * jnp.exp(jnp.minimum(base - cum, EXP_CLAMP))
    nsub = chunk // GLA_SUB
    k_stack, v_stack, k_end, g_end = [], [], [], []
    for s in seqs:
        cum_s, k_s = cum[rsl[s]], k[rsl[s]]
        cum_end = cum_s[chunk - 1:chunk, :]
        k_end.append(k_s * jnp.exp(cum_end - cum_s))
        g_end.append(jnp.exp(cum_end))
        k_var = [k_loc[rsl[s]]]
        for i in range(1, nsub):
            ref_i = cum_s[i * GLA_SUB - 1:i * GLA_SUB, :]
            k_var.append(k_s * jnp.exp(jnp.minimum(ref_i - cum_s, 0.0)))
        k_stack.append(jnp.concatenate(k_var, axis=0) if nsub > 1 else k_var[0])
        v_stack.append(jnp.concatenate([v[rsl[s]]] * nsub, axis=0) if nsub > 1 else v[rsl[s]])

    ri = lax.broadcasted_iota(jnp.int32, (chunk, nsub * chunk), 0)
    cc = lax.broadcasted_iota(jnp.int32, (chunk, nsub * chunk), 1)
    var = cc // chunk
    ci = cc % chunk
    same_sub = ci // GLA_SUB == ri // GLA_SUB
    att_mask = ((var == 0) & same_sub & (ci <= ri)) | ((ri // GLA_SUB == var) & (ci < var * GLA_SUB))

    units = [(s, h) for s in seqs for h in range(C_HEADS)]
    idx = range(len(units))
    csl = [slice(h * C_HEAD, (h + 1) * C_HEAD) for _, h in units]
    att = [jnp.where(att_mask, _mm(q_loc[rsl[s], csl[i]], k_stack[s][:, csl[i]], NT), 0.0)
           for i, (s, _) in enumerate(units)]
    s_old = [s_scr[s, h] for s, h in units]
    outs = [_mm(q_in[rsl[s], csl[i]], s_old[i], NT) + _mm(att[i], v_stack[s][:, csl[i]])
            for i, (s, _) in enumerate(units)]
    for i, (s, h) in enumerate(units):
        sl = csl[i]
        s_scr[s, h] = s_old[i] * g_end[s][:, sl] + _mm(_pad_rows_to_tile(v[rsl[s], sl]).T,
                                                       _pad_rows_to_tile(k_end[s][:, sl]))
    o = jnp.concatenate([jnp.concatenate(outs[s * C_HEADS:(s + 1) * C_HEADS], axis=-1) for s in seqs], axis=0)
    out_ref[...] = (_rms(o, gn_ref[...]) * (xg * _sigmoid(xg))).reshape(nb, chunk, D_MODEL)

    @pl.when(c == nc - 1)
    def _():
        for s in seqs:
            for h in range(C_HEADS):
                s_out_ref[s, h] = s_scr[s, h].T


def _gla(pc, s0, lb_raw, layer, gn, chunk, t_real):
    bsz, t, _ = pc.shape
    nc = t // chunk
    nb = min(bsz, MAX_SEQS_PER_STEP)
    idx = np.arange(chunk)
    eye = np.eye(nb)
    tri = np.kron(eye, np.tril(np.ones((chunk, chunk)))).astype(np.float32)
    sel = np.kron(eye, idx[None, :] < (idx[:, None] // GLA_SUB) * GLA_SUB).astype(np.float32)
    return pl.pallas_call(
        functools.partial(_gla_kernel, nb=nb, chunk=chunk, t_real=t_real, layer=layer),
        grid=(bsz // nb, nc),
        in_specs=[pl.BlockSpec((nb, chunk, 4 * D_MODEL), lambda b, c: (b, c, 0)),
                  pl.BlockSpec((nb, C_HEADS, C_HEAD, C_HEAD), lambda b, c: (b, 0, 0, 0)),
                  _const_spec(lb_raw.shape), _const_spec((1, D_MODEL)),
                  _const_spec((2 * nb * chunk, nb * chunk))],
        out_specs=[pl.BlockSpec((nb, chunk, D_MODEL), lambda b, c: (b, c, 0)),
                   pl.BlockSpec((nb, C_HEADS, C_HEAD, C_HEAD), lambda b, c: (b, 0, 0, 0))],
        out_shape=[jax.ShapeDtypeStruct((bsz, t, D_MODEL), F32),
                   jax.ShapeDtypeStruct((bsz, C_HEADS, C_HEAD, C_HEAD), F32)],
        scratch_shapes=[pltpu.VMEM((nb, C_HEADS, C_HEAD, C_HEAD), F32)],
        compiler_params=_params(2),
        name="hgrn2",
    )(pc, s0, lb_raw, gn.reshape(1, -1), jnp.asarray(np.concatenate([tri, sel]), BF16))


def _post_kernel(*refs, n_mix, final):
    x_ref = refs[0]
    mix_refs = refs[1:1 + n_mix]
    w_refs = refs[1 + n_mix:1 + 2 * n_mix]
    gm_ref, up_ref, down_ref = refs[1 + 2 * n_mix:4 + 2 * n_mix]
    gf_ref = refs[4 + 2 * n_mix] if final else None
    o_ref = refs[-1]
    x = x_ref[...]
    for m_ref, w_ref in zip(mix_refs, w_refs):
        x = x + jnp.dot(m_ref[...].astype(BF16), w_ref[...], preferred_element_type=F32)
    h = _rms(x, gm_ref[...]).astype(BF16)
    u = jnp.dot(h, up_ref[...], preferred_element_type=F32)
    u = jnp.square(jnp.maximum(u, 0.0)).astype(BF16)
    x = x + jnp.dot(u, down_ref[...], preferred_element_type=F32)
    if final:
        x = _rms(x, gf_ref[...])
    o_ref[...] = x


def _post(x, mixes, ws, gm, up, down, layer, gf=None, tm=512):
    m = x.shape[0]
    tm = min(tm, m)
    final = gf is not None
    row_spec = lambda n: pl.BlockSpec((tm, n), lambda i: (i, 0))
    layer_spec = lambda w: pl.BlockSpec((None,) + w.shape[1:], lambda i: (layer, 0, 0),
                                        pipeline_mode=pl.Buffered(1))
    rows = [x] + list(mixes)
    args = rows + [*ws, gm.reshape(1, -1), up, down]
    specs = ([row_spec(a.shape[1]) for a in rows] + [_const_spec(w.shape) for w in ws]
             + [_const_spec((1, D_MODEL)), layer_spec(up), layer_spec(down)])
    if final:
        args.append(gf.reshape(1, -1))
        specs.append(_const_spec((1, D_MODEL)))
    return pl.pallas_call(
        functools.partial(_post_kernel, n_mix=len(mixes), final=final),
        grid=(m // tm,), in_specs=specs, out_specs=row_spec(D_MODEL),
        out_shape=jax.ShapeDtypeStruct((m, D_MODEL), F32), compiler_params=_params(1), name="post_mlp",
    )(*args)


def _lora_pad_cols(x):
    parts, start = [], 0
    for n, tiles in zip(LORA_DIMS, LORA_TILES):
        part = x[..., start:start + n]
        parts.append(jnp.pad(part, [(0, 0)] * (part.ndim - 1) + [(0, tiles * LANES - n)]))
        start += n
    return jnp.concatenate(parts, axis=-1)


def _lora_unpad_cols(x):
    parts, start = [], 0
    for n, tiles in zip(LORA_DIMS, LORA_TILES):
        parts.append(x[..., start:start + n])
        start += tiles * LANES
    return jnp.concatenate(parts, axis=-1)


def _prepare(rel_bias, norm_mix, norm_mlp, norm_final, e_w_in, e_mu, e_w0, e_w2, e_a0, e_a2, e_g2, e_k_k, e_k_a,
             e_r_k, e_lnx_w, e_lnx_b, e_w_out, c_w_in, c_lb_raw, c_norm, c_w_out, mlp_up, mlp_down):
    pad_rows = lambda a, n: jnp.pad(a, [(0, n - a.shape[0]), (0, 0)])
    w_in = e_w_in[0]
    return dict(
        rel_bias=rel_bias,
        w_in=w_in.astype(BF16), w_lora=_lora_pad_cols(w_in[:, A_QKV + 3 * B_WIDTH:]).astype(BF16),
        rwkv=dict(mu_rkv=e_mu[0, :3 * B_WIDTH], mu_lora=_lora_pad_cols(e_mu[0, 3 * B_WIDTH:]),
                  w0=e_w0[0], w2=jnp.stack(_split2(pad_rows(e_w2[0], LORA_TILES[0] * LANES))), a0=e_a0[0],
                  a2=pad_rows(e_a2[0], LORA_TILES[1] * LANES).astype(BF16),
                  g2=pad_rows(e_g2[0], LORA_TILES[2] * LANES).astype(BF16),
                  k_k=e_k_k[0], k_a=e_k_a[0], r_k=e_r_k[0].reshape(-1),
                  lnx_w=e_lnx_w[0], lnx_b=e_lnx_b[0]),
        w_out_a=e_w_out[0, :A_OUT].astype(BF16), w_out_b=e_w_out[0, A_OUT:].astype(BF16),
        c_w_in=c_w_in[0].astype(BF16), c_w_out=c_w_out[0].astype(BF16), lb_raw=c_lb_raw, c_norm=c_norm[0],
        norm_mix=norm_mix, norm_mlp=norm_mlp, norm_final=norm_final,
        up=mlp_up.astype(BF16), down=mlp_down.astype(BF16))


def _trunk(x, prm, attend, sh_rkv, sh_lora, wkv0, c0, chunk):
    bsz, t_real, _ = x.shape
    t = -(-t_real // chunk) * chunk

    def seqs(a):
        a = a.reshape(bsz, t_real, a.shape[-1])
        return a if t == t_real else jnp.pad(a, ((0, 0), (0, t - t_real), (0, 0)))

    flat = lambda a: a[:, :t_real].reshape(bsz * t_real, a.shape[-1])
    xf = x.reshape(bsz * t_real, D_MODEL)
    assert A_QKV == 3 * B_WIDTH
    qkv, rkv, lora = _norm_proj(xf, prm["norm_mix"][0],
                                [(prm["w_in"], A_QKV, 0), (prm["w_in"], A_QKV, 1), (prm["w_lora"], LORA_PAD, 0)])
    qkv, rkv, lora = seqs(qkv), seqs(rkv), seqs(lora)
    a_out = attend(qkv)
    b_out, wkv = _rwkv(rkv, lora, sh_rkv, sh_lora, wkv0, prm["rwkv"], chunk, t_real)
    x1 = _post(xf, [flat(a_out), flat(b_out)], [prm["w_out_a"], prm["w_out_b"]], prm["norm_mlp"][0],
               prm["up"], prm["down"], 0)
    (pc,) = _norm_proj(x1, prm["norm_mix"][1], [(prm["c_w_in"], 4 * D_MODEL, 0)])
    c_out, c_state = _gla(seqs(pc), c0, prm["lb_raw"], 1, prm["c_norm"], chunk, t_real)
    y = _post(x1, [flat(c_out)], [prm["c_w_out"]], prm["norm_mlp"][1], prm["up"], prm["down"], 1,
              gf=prm["norm_final"])
    kv_rows = []
    for g, (w, _) in enumerate(A_GROUPS):
        n = min(w, t_real)
        rows = qkv[:, t_real - n:t_real]
        kv = jnp.stack([rows[..., (3 * part + g) * A_OUT:(3 * part + g + 1) * A_OUT] for part in (1, 2)], axis=2)
        kv_rows.append(kv.reshape(1, bsz, n, 2, 4, HEAD))
    shift = jnp.concatenate([rkv[:, t_real - 1], _lora_unpad_cols(lora[:, t_real - 1])], axis=-1)[None]
    return y.reshape(bsz, t_real, D_MODEL), kv_rows, shift, wkv[None], c_state[None]


def kernel(x_prompt, x_sample, cache_a0, cache_a1, cache_a2, state_b_shift, state_b_wkv, state_c, rel_bias, norm_mix, norm_mlp, norm_final, e_w_in, e_mu, e_w0, e_w2, e_a0, e_a2, e_g2, e_k_k, e_k_a, e_r_k, e_lnx_w, e_lnx_b, e_w_out, c_w_in, c_lb_raw, c_norm, c_w_out, mlp_up, mlp_down):
    prm = _prepare(rel_bias, norm_mix, norm_mlp, norm_final, e_w_in, e_mu, e_w0, e_w2, e_a0, e_a2, e_g2, e_k_k,
                   e_k_a, e_r_k, e_lnx_w, e_lnx_b, e_w_out, c_w_in, c_lb_raw, c_norm, c_w_out, mlp_up, mlp_down)

    bp, tp, _ = x_prompt.shape

    def attend_prompt(qkv):
        return _attn_prompt(qkv, prm["rel_bias"])

    y_p, p_kv, p_shift, p_wkv, p_c = _trunk(
        x_prompt, prm, attend_prompt,
        jnp.zeros((bp, 3 * B_WIDTH), F32), jnp.zeros((bp, LORA_PAD), F32),
        jnp.zeros((bp, 2 * B_PAIRS, HEAD, HEAD), F32), jnp.zeros((bp, C_HEADS, C_HEAD, C_HEAD), F32),
        chunk=CHUNK)

    bs, ts, _ = x_sample.shape
    caches = [jnp.transpose(c[0], (0, 2, 3, 4, 1)) for c in (cache_a0, cache_a1, cache_a2)]

    def attend_sample(qkv):
        return _attn_sample(qkv, caches, prm["rel_bias"], ts)

    sh = state_b_shift[0]
    y_s, s_kv, s_shift, s_wkv, s_c = _trunk(
        x_sample, prm, attend_sample, sh[:, :3 * B_WIDTH], _lora_pad_cols(sh[:, 3 * B_WIDTH:]),
        state_b_wkv[0], state_c[0], chunk=SAMPLE_ROWS)

    return (y_p, y_s, p_kv[0], p_kv[1], p_kv[2], p_shift, p_wkv, p_c,
            s_kv[0], s_kv[1], s_kv[2], s_shift, s_wkv, s_c)
```
